```python
import jax, jax.numpy as jnp
from jax import lax
import numpy as np

D_MODEL = 2048
BATCH = 4
SEQ = 2048
DEPTH = 4

GRID_W = 64
CTX_LEN = 256
N_MIXERS = 3
EPS = 1e-6
Q_BLOCK = 128
ROPE_THETA = 10000.0
NA_HEAD_DIM = 128
NA_HEADS = D_MODEL // NA_HEAD_DIM
WIN_H = 8
WIN_W = 16
MLA_HEADS = D_MODEL // 128
MLA_NOPE = 128
MLA_ROPE = 64
MLA_V = 128
MLA_Q_RANK = 512
MLA_KV_RANK = 512
CHUNK = 128
SG_WIDTH = D_MODEL
SG_GROUP_DIM = 128
SG_GROUPS = SG_WIDTH // SG_GROUP_DIM
D_FF = ((8 * D_MODEL // 3 + 255) // 256) * 256
CONV_W = 3

kernel_name = "hybrid_natten_mla_sgu_convffn_prefix_ctx"


def rmsnorm(z, g):
    zf = z.astype(jnp.float32)
    y = zf * lax.rsqrt(jnp.mean(zf * zf, axis=-1, keepdims=True) + EPS)
    return y.astype(z.dtype) * g


def layernorm(z, g, b):
    zf = z.astype(jnp.float32)
    mu = jnp.mean(zf, axis=-1, keepdims=True)
    var = jnp.mean(jnp.square(zf - mu), axis=-1, keepdims=True)
    return ((zf - mu) * lax.rsqrt(var + EPS)).astype(z.dtype) * g + b


def modulate(z, g, shift, scale):
    return rmsnorm(z, g) * (1 + scale) + shift


def merge_heads(o):
    B, H, N, dh = o.shape
    return o.transpose(0, 2, 1, 3).reshape(B, N, H * dh)


def dense_attention(q, k, v, scale):
    s = jnp.einsum('bhqd,bhkd->bhqk', q, k) * scale
    p = jax.nn.softmax(s.astype(jnp.float32), axis=-1).astype(v.dtype)
    return jnp.einsum('bhqk,bhkd->bhqd', p, v)


def apply_axial_rope(z):
    n, dim = z.shape[-2], z.shape[-1]
    half = dim // 2
    freqs = ROPE_THETA ** (-jnp.arange(0, half, 2, dtype=jnp.float32) / half)
    t = jnp.arange(n)
    rows = (t // GRID_W).astype(jnp.float32)
    cols = (t % GRID_W).astype(jnp.float32)

    def rot(u, ang):
        cs, sn = jnp.cos(ang).astype(u.dtype), jnp.sin(ang).astype(u.dtype)
        u1, u2 = jnp.split(u, 2, axis=-1)
        return jnp.concatenate([u1 * cs - u2 * sn, u1 * sn + u2 * cs], axis=-1)

    return jnp.concatenate([rot(z[..., :half], rows[:, None] * freqs),
                            rot(z[..., half:], cols[:, None] * freqs)], axis=-1)


def neighborhood_attention(h, hc, w_qkv, w_o, rpb, need_ctx_out):
    B, S, D = h.shape
    rows = S // GRID_W
    kh, kw = min(WIN_H, rows), WIN_W
    H, dh = NA_HEADS, NA_HEAD_DIM
    scale = dh ** -0.5

    def heads(z):
        n = z.shape[1]
        qkv = (z @ w_qkv).reshape(B, n, 3, H, dh).transpose(2, 0, 3, 1, 4)
        return qkv[0], qkv[1], qkv[2]

    q, k, v = heads(h)
    qc, kc, vc = heads(hc)
    q, k, v = (t.reshape(B, H, rows, GRID_W, dh) for t in (q, k, v))
    r_ar, w_ar = np.arange(rows), np.arange(GRID_W)
    row_idx = np.clip(r_ar - kh // 2, 0, rows - kh)[:, None] + np.arange(kh)
    col_idx = np.clip(w_ar - kw // 2, 0, GRID_W - kw)[:, None] + np.arange(kw)
    k_band = k[:, :, row_idx]
    v_band = v[:, :, row_idx]
    col_sel = jnp.asarray(np.eye(GRID_W)[col_idx], h.dtype)
    s_band = jnp.einsum('bhrqd,bhrakd->bhrqak', q, k_band)
    s_win = jnp.einsum('bhrqak,qjk->bhrqaj', s_band, col_sel)
    row_off = row_idx - r_ar[:, None] + WIN_H - 1
    col_off = col_idx - w_ar[:, None] + WIN_W - 1
    bias = rpb[:, row_off[:, None, :, None], col_off[None, :, None, :]]
    s_win = s_win * scale + bias
    s_ctx = jnp.einsum('bhrqd,bhcd->bhrqc', q, kc) * scale
    logits = jnp.concatenate([s_win.reshape(B, H, rows, GRID_W, kh * kw), s_ctx], axis=-1)
    p = jax.nn.softmax(logits.astype(jnp.float32), axis=-1).astype(v.dtype)
    p_win = p[..., :kh * kw].reshape(B, H, rows, GRID_W, kh, kw)
    p_band = jnp.einsum('bhrqaj,qjk->bhrqak', p_win, col_sel)
    o = (jnp.einsum('bhrqak,bhrakd->bhrqd', p_band, v_band)
         + jnp.einsum('bhrqc,bhcd->bhrqd', p[..., kh * kw:], vc))
    y = o.transpose(0, 2, 3, 1, 4).reshape(B, S, H * dh) @ w_o
    yc = merge_heads(dense_attention(qc, kc, vc, scale)) @ w_o if need_ctx_out else None
    return y, yc


def mla_attend(qn, qp, kn, kpe, v, scale):
    s = (jnp.einsum('bhqd,bhkd->bhqk', qn, kn) + jnp.einsum('bhqr,bkr->bhqk', qp, kpe)) * scale
    p = jax.nn.softmax(s.astype(jnp.float32), axis=-1).astype(v.dtype)
    return jnp.einsum('bhqk,bhkd->bhqd', p, v)


def latent_attention(h, hc, w_in, q_norm, kv_norm, w_uq, w_ukv, w_o, need_ctx_out):
    H = MLA_HEADS
    scale = (MLA_NOPE + MLA_ROPE) ** -0.5

    def project(z, rope):
        B, n, _ = z.shape
        cq, ckv, kpe = jnp.split(z @ w_in, [MLA_Q_RANK, MLA_Q_RANK + MLA_KV_RANK], axis=-1)
        q = (rmsnorm(cq, q_norm) @ w_uq).reshape(B, n, H, MLA_NOPE + MLA_ROPE).transpose(0, 2, 1, 3)
        kv = (rmsnorm(ckv, kv_norm) @ w_ukv).reshape(B, n, H, MLA_NOPE + MLA_V).transpose(0, 2, 1, 3)
        qn, qp = q[..., :MLA_NOPE], q[..., MLA_NOPE:]
        kn, v = kv[..., :MLA_NOPE], kv[..., MLA_NOPE:]
        if rope:
            qp, kpe = apply_axial_rope(qp), apply_axial_rope(kpe)
        return qn, qp, kn, kpe, v

    qn, qp, kn, kpe, v = project(h, True)
    qnc, qpc, knc, kpec, vc = project(hc, False)
    kn_all = jnp.concatenate([kn, knc], axis=2)
    kpe_all = jnp.concatenate([kpe, kpec], axis=1)
    v_all = jnp.concatenate([v, vc], axis=2)
    B, _, S, _ = qn.shape
    nb = S // Q_BLOCK

    def blocks(t):
        return t.reshape(B, H, nb, Q_BLOCK, t.shape[-1]).transpose(2, 0, 1, 3, 4)

    o = lax.map(lambda qs: mla_attend(qs[0], qs[1], kn_all, kpe_all, v_all, scale), (blocks(qn), blocks(qp)))
    o = o.transpose(1, 2, 0, 3, 4).reshape(B, H, S, MLA_V)
    y = merge_heads(o) @ w_o
    yc = merge_heads(mla_attend(qnc, qpc, knc, kpec, vc, scale)) @ w_o if need_ctx_out else None
    return y, yc


def spatial_gating(z, w_in, b_in, ln_g, ln_b, ws, bs, w_o):
    B, n, _ = z.shape
    u, v = jnp.split(jax.nn.gelu(z @ w_in + b_in, approximate=False), 2, axis=-1)
    v = layernorm(v, ln_g, ln_b).reshape(B, n // CHUNK, CHUNK, SG_GROUPS, SG_GROUP_DIM)
    mix = jnp.einsum('gpk,bnkgd->bnpgd', ws, v) + bs.T[:, :, None]
    return (u * mix.reshape(B, n, SG_WIDTH)) @ w_o


def conv_ffn(z, w_in, conv_w, conv_b, w_out):
    n = z.shape[1]
    u = z @ w_in
    pad = CONV_W // 2
    up = jnp.pad(u, ((0, 0), (pad, pad), (0, 0)))
    u = sum(up[:, t:t + n] * conv_w[t] for t in range(CONV_W)) + conv_b
    a, g = jnp.split(u, 2, axis=-1)
    return (a * jax.nn.silu(g)) @ w_out


def _normal(key, shape, scale):
    return jax.random.normal(key, shape, jnp.float32) * scale


def setup_inputs(seed: int = 0) -> dict:
    key = jax.random.key(seed)
    ks = iter(jax.random.split(key, 32))
    D = D_MODEL
    nA, nB, nC = (len(range(kind, DEPTH, N_MIXERS)) for kind in range(N_MIXERS))
    return {
        "x": _normal(next(ks), (BATCH, SEQ, D), 1.0),
        "c": _normal(next(ks), (BATCH, D), 1.0),
        "ctx": _normal(next(ks), (BATCH, CTX_LEN, D), 1.0),
        "c_ctx": _normal(next(ks), (D,), 1.0),
        "ada_w": _normal(next(ks), (DEPTH, D, 6 * D), 0.5 * D ** -0.5),
        "ada_b": _normal(next(ks), (DEPTH, 6 * D), 0.02),
        "norm_g": 1.0 + _normal(next(ks), (DEPTH, 2, D), 0.02),
        "final_g": 1.0 + _normal(next(ks), (D,), 0.02),
        "a_w_qkv": _normal(next(ks), (nA, D, 3 * NA_HEADS * NA_HEAD_DIM), D ** -0.5),
        "a_w_o": _normal(next(ks), (nA, NA_HEADS * NA_HEAD_DIM, D), (NA_HEADS * NA_HEAD_DIM) ** -0.5),
        "a_rpb": _normal(next(ks), (nA, NA_HEADS, 2 * WIN_H - 1, 2 * WIN_W - 1), 0.1),
        "b_w_in": _normal(next(ks), (nB, D, MLA_Q_RANK + MLA_KV_RANK + MLA_ROPE), D ** -0.5),
        "b_q_norm": 1.0 + _normal(next(ks), (nB, MLA_Q_RANK), 0.02),
        "b_kv_norm": 1.0 + _normal(next(ks), (nB, MLA_KV_RANK), 0.02),
        "b_w_uq": _normal(next(ks), (nB, MLA_Q_RANK, MLA_HEADS * (MLA_NOPE + MLA_ROPE)), MLA_Q_RANK ** -0.5),
        "b_w_ukv": _normal(next(ks), (nB, MLA_KV_RANK, MLA_HEADS * (MLA_NOPE + MLA_V)), MLA_KV_RANK ** -0.5),
        "b_w_o": _normal(next(ks), (nB, MLA_HEADS * MLA_V, D), (MLA_HEADS * MLA_V) ** -0.5),
        "c_w_in": _normal(next(ks), (nC, D, 2 * SG_WIDTH), D ** -0.5),
        "c_b_in": _normal(next(ks), (nC, 2 * SG_WIDTH), 0.02),
        "c_ln_g": 1.0 + _normal(next(ks), (nC, SG_WIDTH), 0.02),
        "c_ln_b": _normal(next(ks), (nC, SG_WIDTH), 0.02),
        "c_ws": _normal(next(ks), (nC, SG_GROUPS, CHUNK, CHUNK), CHUNK ** -0.5),
        "c_bs": 1.0 + _normal(next(ks), (nC, SG_GROUPS, CHUNK), 0.02),
        "c_w_o": _normal(next(ks), (nC, SG_WIDTH, D), SG_WIDTH ** -0.5),
        "f_w_in": _normal(next(ks), (DEPTH, D, 2 * D_FF), D ** -0.5),
        "f_conv_w": _normal(next(ks), (DEPTH, CONV_W, 2 * D_FF), CONV_W ** -0.5),
        "f_conv_b": _normal(next(ks), (DEPTH, 2 * D_FF), 0.02),
        "f_w_out": _normal(next(ks), (DEPTH, D_FF, D), D_FF ** -0.5),
    }


def reference(x, c, ctx, c_ctx, ada_w, ada_b, norm_g, final_g,
              a_w_qkv, a_w_o, a_rpb,
              b_w_in, b_q_norm, b_kv_norm, b_w_uq, b_w_ukv, b_w_o,
              c_w_in, c_b_in, c_ln_g, c_ln_b, c_ws, c_bs, c_w_o,
              f_w_in, f_conv_w, f_conv_b, f_w_out):
    s_lat = jax.nn.silu(c)
    s_ctx = jax.nn.silu(c_ctx)
    xc = ctx
    for i in range(DEPTH):
        kind, j = i % N_MIXERS, i // N_MIXERS
        last = i == DEPTH - 1
        m = [t[:, None, :] for t in jnp.split(s_lat @ ada_w[i] + ada_b[i], 6, axis=-1)]
        mc = jnp.split(s_ctx @ ada_w[i] + ada_b[i], 6, axis=-1)
        h = modulate(x, norm_g[i, 0], m[0], m[1])
        hc = modulate(xc, norm_g[i, 0], mc[0], mc[1]) if (not last or kind != 2) else None
        if kind == 0:
            y, yc = neighborhood_attention(h, hc, a_w_qkv[j], a_w_o[j], a_rpb[j], not last)
        elif kind == 1:
            y, yc = latent_attention(h, hc, b_w_in[j], b_q_norm[j], b_kv_norm[j], b_w_uq[j],
                                     b_w_ukv[j], b_w_o[j], not last)
        else:
            sg = (c_w_in[j], c_b_in[j], c_ln_g[j], c_ln_b[j], c_ws[j], c_bs[j], c_w_o[j])
            y = spatial_gating(h, *sg)
            yc = spatial_gating(hc, *sg) if not last else None
        x = x + m[2] * y
        h = modulate(x, norm_g[i, 1], m[3], m[4])
        x = x + m[5] * conv_ffn(h, f_w_in[i], f_conv_w[i], f_conv_b[i], f_w_out[i])
        if not last:
            xc = xc + mc[2] * yc
            hc = modulate(xc, norm_g[i, 1], mc[3], mc[4])
            xc = xc + mc[5] * conv_ffn(hc, f_w_in[i], f_conv_w[i], f_conv_b[i], f_w_out[i])
    return rmsnorm(x, final_g)
```

```python
import functools

import jax
import jax.numpy as jnp
import numpy as np
from jax import lax
from jax.experimental import pallas as pl
from jax.experimental.pallas import tpu as pltpu

F32 = jnp.float32
BF16 = jnp.bfloat16

D = 2048
BATCH = 4
SEQ = 2048
DEPTH = 4
GRID_W = 64
GRID_H = SEQ // GRID_W
CTX_LEN = 256
N_MIXERS = 3
EPS = 1e-6
ROPE_THETA = 10000.0
HEADS = 16
HEAD_DIM = 128
WIN_H = 8
WIN_W = 16
MLA_NOPE = 128
MLA_ROPE = 64
MLA_V = 128
MLA_Q_RANK = 512
MLA_KV_RANK = 512
CHUNK = 128
SG_GROUPS = 16
D_FF = 5632
CONV_W = 3

T_LAT = BATCH * SEQ
T_CTX = BATCH * CTX_LEN
T_ALL = T_LAT + T_CTX
CTX_ROW_BLOCK0 = T_LAT // CTX_LEN
MASK_VALUE = -1e30
VMEM_LIMIT = 56 * 1024 * 1024


def _params(*semantics):
    return pltpu.CompilerParams(dimension_semantics=semantics, vmem_limit_bytes=VMEM_LIMIT)


def _mod_row(tm):
    per_batch = SEQ // tm
    return lambda m: jnp.minimum(m // per_batch, BATCH)


def _ada_kernel(s_ref, w_ref, b_ref, o_ref):
    s = s_ref[...]
    s = s * jax.nn.sigmoid(s)
    o_ref[...] = jnp.dot(s.astype(BF16), w_ref[...].astype(BF16), preferred_element_type=F32) + b_ref[...]


def _ada(s, ada_w, ada_b):
    tn = 1024
    n6 = 6 * D
    return pl.pallas_call(
        _ada_kernel,
        grid=(DEPTH, n6 // tn),
        in_specs=[
            pl.BlockSpec((8, D), lambda i, n: (0, 0)),
            pl.BlockSpec((None, D, tn), lambda i, n: (i, 0, n)),
            pl.BlockSpec((None, 1, tn), lambda i, n: (i, 0, n)),
        ],
        out_specs=pl.BlockSpec((None, 8, tn), lambda i, n: (i, 0, n)),
        out_shape=jax.ShapeDtypeStruct((DEPTH, 8, n6), F32),
        compiler_params=_params("parallel", "parallel"),
        name="ada",
    )(s, ada_w, ada_b.reshape(DEPTH, 1, n6))


def _gelu_exact(z):
    return 0.5 * z * (1.0 + lax.erf(z * np.float32(np.sqrt(0.5))))


def _modulate(xf, g, shift, scale):
    y = xf * lax.rsqrt(jnp.mean(xf * xf, axis=-1, keepdims=True) + EPS)
    return (y * g) * (1.0 + scale) + shift


def _mod_mm_kernel(x_ref, g_ref, sh_ref, sc_ref, w_ref, *rest, has_bias, act):
    if has_bias:
        b_ref, o_ref, h_ref = rest
    else:
        o_ref, h_ref = rest

    @pl.when(pl.program_id(1) == 0)
    def _():
        h_ref[...] = _modulate(x_ref[...], g_ref[...], sh_ref[...], sc_ref[...]).astype(BF16)

    acc = jnp.dot(h_ref[...], w_ref[...], preferred_element_type=F32)
    if has_bias:
        acc = acc + b_ref[...]
    if act == "gelu":
        acc = _gelu_exact(acc)
    o_ref[...] = acc.astype(o_ref.dtype)


def _mod_matmul(x, g, mods, shift_col, w, *, rows, tm, tn, out_dtype, bias=None, act=None, name):
    n = w.shape[1]
    brow = _mod_row(tm)
    in_specs = [
        pl.BlockSpec((tm, D), lambda m, j: (m, 0)),
        pl.BlockSpec((1, D), lambda m, j: (0, 0)),
        pl.BlockSpec((None, 1, D), lambda m, j: (brow(m), 0, shift_col)),
        pl.BlockSpec((None, 1, D), lambda m, j: (brow(m), 0, shift_col + 1)),
        pl.BlockSpec((D, tn), lambda m, j: (0, j)),
    ]
    args = [x, g.reshape(1, D), mods, mods, w]
    if bias is not None:
        in_specs.append(pl.BlockSpec((1, tn), lambda m, j: (0, j)))
        args.append(bias.reshape(1, n))
    return pl.pallas_call(
        functools.partial(_mod_mm_kernel, has_bias=bias is not None, act=act),
        grid=(rows // tm, n // tn),
        in_specs=in_specs,
        out_specs=pl.BlockSpec((tm, tn), lambda m, j: (m, j)),
        out_shape=jax.ShapeDtypeStruct((rows, n), out_dtype),
        scratch_shapes=[pltpu.VMEM((tm, D), BF16)],
        compiler_params=_params("parallel", "arbitrary"),
        name=name,
    )(*args)


def _mm_resid_kernel(a_ref, w_ref, r_ref, gate_ref, o_ref):
    acc = jnp.dot(a_ref[...], w_ref[...], preferred_element_type=F32)
    o_ref[...] = r_ref[...] + gate_ref[...] * acc


def _matmul_resid(a, w, resid, mods, gate_col, *, rows, tm, tn, name):
    k, n = w.shape
    brow = _mod_row(tm)
    ntile = D // tn
    return pl.pallas_call(
        _mm_resid_kernel,
        grid=(rows // tm, n // tn),
        in_specs=[
            pl.BlockSpec((tm, k), lambda m, j: (m, 0)),
            pl.BlockSpec((k, tn), lambda m, j: (0, j)),
            pl.BlockSpec((tm, tn), lambda m, j: (m, j)),
            pl.BlockSpec((None, 1, tn), lambda m, j: (brow(m), 0, gate_col * ntile + j)),
        ],
        out_specs=pl.BlockSpec((tm, tn), lambda m, j: (m, j)),
        out_shape=jax.ShapeDtypeStruct((rows, n), F32),
        compiler_params=_params("parallel", "arbitrary"),
        name=name,
    )(a, w, resid, mods)


NA_QBLOCK = 256
NA_QROWS = NA_QBLOCK // GRID_W
NA_LAT_BLOCKS = SEQ // NA_QBLOCK
NA_BAND = WIN_H * GRID_W


def _nt_dot(a, b):
    return lax.dot_general(a, b, (((1,), (1,)), ((), ())), preferred_element_type=F32)


def _na_kernel(q_ref, k_ref, v_ref, kc_ref, vc_ref, bias_ref, o_ref):
    i = pl.program_id(2)
    scale = np.float32(HEAD_DIM ** -0.5)
    kc = kc_ref[...]
    vc = vc_ref[...]

    @pl.when(i < NA_LAT_BLOCKS)
    def _():
        for j in range(NA_QROWS):
            r = i * NA_QROWS + j
            rs = jnp.clip(r - WIN_H // 2, 0, GRID_H - WIN_H)
            key0 = pl.multiple_of(rs * GRID_W, GRID_W)
            q = q_ref[j * GRID_W:(j + 1) * GRID_W, :]
            kb = k_ref[pl.ds(key0, NA_BAND), :]
            vb = v_ref[pl.ds(key0, NA_BAND), :]
            ro = rs - r + WIN_H - 1
            bias = jnp.concatenate([bias_ref[ro + 2 * p] for p in range(WIN_H // 2)], axis=1)
            s = _nt_dot(q, kb) * scale + bias
            sc = _nt_dot(q, kc) * scale
            mx = jnp.maximum(jnp.max(s, axis=-1, keepdims=True), jnp.max(sc, axis=-1, keepdims=True))
            e = jnp.exp(s - mx)
            ec = jnp.exp(sc - mx)
            denom = jnp.sum(e, axis=-1, keepdims=True) + jnp.sum(ec, axis=-1, keepdims=True)
            o = (jnp.dot(e.astype(BF16), vb, preferred_element_type=F32)
                 + jnp.dot(ec.astype(BF16), vc, preferred_element_type=F32))
            o_ref[j * GRID_W:(j + 1) * GRID_W, :] = (o / denom).astype(o_ref.dtype)

    @pl.when(i >= NA_LAT_BLOCKS)
    def _():
        sc = _nt_dot(q_ref[...], kc) * scale
        ec = jnp.exp(sc - jnp.max(sc, axis=-1, keepdims=True))
        o = jnp.dot(ec.astype(BF16), vc, preferred_element_type=F32)
        o_ref[...] = (o / jnp.sum(ec, axis=-1, keepdims=True)).astype(o_ref.dtype)


def _na_bias_table(rpb):
    w = np.arange(GRID_W)
    col0 = np.clip(w - WIN_W // 2, 0, GRID_W - WIN_W)
    kcol = np.arange(GRID_W)
    in_win = (kcol[None, :] >= col0[:, None]) & (kcol[None, :] < col0[:, None] + WIN_W)
    off = np.clip(kcol[None, :] - w[:, None] + WIN_W - 1, 0, 2 * WIN_W - 2)
    t = jnp.where(jnp.asarray(in_win), rpb[:, :, off], MASK_VALUE)
    return jnp.concatenate([t[:, :-1], t[:, 1:]], axis=-1)


def _na_attention(qkv, bias_tab, *, with_ctx_queries, out_rows):
    nq = NA_LAT_BLOCKS + (1 if with_ctx_queries else 0)

    def qrow(b, h, i):
        return jnp.where(i < NA_LAT_BLOCKS, b * NA_LAT_BLOCKS + i, CTX_ROW_BLOCK0 + b)

    return pl.pallas_call(
        _na_kernel,
        grid=(BATCH, HEADS, nq),
        in_specs=[
            pl.BlockSpec((NA_QBLOCK, HEAD_DIM), lambda b, h, i: (qrow(b, h, i), h)),
            pl.BlockSpec((SEQ, HEAD_DIM), lambda b, h, i: (b, HEADS + h)),
            pl.BlockSpec((SEQ, HEAD_DIM), lambda b, h, i: (b, 2 * HEADS + h)),
            pl.BlockSpec((CTX_LEN, HEAD_DIM), lambda b, h, i: (CTX_ROW_BLOCK0 + b, HEADS + h)),
            pl.BlockSpec((CTX_LEN, HEAD_DIM), lambda b, h, i: (CTX_ROW_BLOCK0 + b, 2 * HEADS + h)),
            pl.BlockSpec((None, 2 * WIN_H - 2, GRID_W, 2 * GRID_W), lambda b, h, i: (h, 0, 0, 0)),
        ],
        out_specs=pl.BlockSpec((NA_QBLOCK, HEAD_DIM), lambda b, h, i: (qrow(b, h, i), h)),
        out_shape=jax.ShapeDtypeStruct((out_rows, D), BF16),
        compiler_params=_params("parallel", "parallel", "arbitrary"),
        name="na_attention",
    )(qkv, qkv, qkv, qkv, qkv, bias_tab)


MLA_QK = 2 * HEAD_DIM
MLA_QBLOCK = 256
MLA_LAT_BLOCKS = SEQ // MLA_QBLOCK
MLA_ZCOLS = MLA_Q_RANK + MLA_KV_RANK + 2 * HEAD_DIM


def _rms(z, g):
    return (z * lax.rsqrt(jnp.mean(z * z, axis=-1, keepdims=True) + EPS)) * g


def _mla_up_kernel(z_ref, qn_ref, kvn_ref, cos_ref, sin_ref, wq_ref, wqs_ref, wkv_ref,
                   q_ref, kv_ref, kpe_ref, cq_ref, ckv_ref):
    @pl.when(pl.program_id(1) == 0)
    def _():
        cq_ref[...] = _rms(z_ref[:, :MLA_Q_RANK], qn_ref[...]).astype(BF16)
        ckv_ref[...] = _rms(z_ref[:, MLA_Q_RANK:MLA_Q_RANK + MLA_KV_RANK], kvn_ref[...]).astype(BF16)
        c0 = MLA_Q_RANK + MLA_KV_RANK
        kpe = z_ref[:, c0:c0 + HEAD_DIM] * cos_ref[...] + z_ref[:, c0 + HEAD_DIM:] * sin_ref[...]
        kpe_ref[...] = kpe.astype(BF16)

    cq = cq_ref[...]
    q = jnp.dot(cq, wq_ref[...], preferred_element_type=F32)
    qs = jnp.dot(cq, wqs_ref[...], preferred_element_type=F32)
    q_ref[:, :HEAD_DIM] = q[:, :HEAD_DIM].astype(BF16)
    q_ref[:, HEAD_DIM:] = (q[:, HEAD_DIM:] * cos_ref[...] + qs * sin_ref[...]).astype(BF16)
    kv_ref[...] = jnp.dot(ckv_ref[...], wkv_ref[...], preferred_element_type=F32).astype(BF16)


def _mla_up(z, q_norm, kv_norm, cos, sin, wq, wqs, wkv):
    tm = 512
    return pl.pallas_call(
        _mla_up_kernel,
        grid=(T_ALL // tm, HEADS),
        in_specs=[
            pl.BlockSpec((tm, MLA_ZCOLS), lambda m, h: (m, 0)),
            pl.BlockSpec((1, MLA_Q_RANK), lambda m, h: (0, 0)),
            pl.BlockSpec((1, MLA_KV_RANK), lambda m, h: (0, 0)),
            pl.BlockSpec((tm, HEAD_DIM), lambda m, h: (m, 0)),
            pl.BlockSpec((tm, HEAD_DIM), lambda m, h: (m, 0)),
            pl.BlockSpec((MLA_Q_RANK, MLA_QK), lambda m, h: (0, h)),
            pl.BlockSpec((MLA_Q_RANK, HEAD_DIM), lambda m, h: (0, h)),
            pl.BlockSpec((MLA_KV_RANK, 2 * HEAD_DIM), lambda m, h: (0, h)),
        ],
        out_specs=[
            pl.BlockSpec((tm, MLA_QK), lambda m, h: (m, h)),
            pl.BlockSpec((tm, 2 * HEAD_DIM), lambda m, h: (m, h)),
            pl.BlockSpec((tm, HEAD_DIM), lambda m, h: (m, 0)),
        ],
        out_shape=[
            jax.ShapeDtypeStruct((T_ALL, HEADS * MLA_QK), BF16),
            jax.ShapeDtypeStruct((T_ALL, HEADS * 2 * HEAD_DIM), BF16),
            jax.ShapeDtypeStruct((T_ALL, HEAD_DIM), BF16),
        ],
        scratch_shapes=[pltpu.VMEM((tm, MLA_Q_RANK), BF16), pltpu.VMEM((tm, MLA_KV_RANK), BF16)],
        compiler_params=_params("parallel", "arbitrary"),
        name="mla_up",
    )(z, q_norm.reshape(1, -1), kv_norm.reshape(1, -1), cos, sin, wq, wqs, wkv)


def _mla_kernel(q_ref, kn_ref, v_ref, kpe_ref, knc_ref, vc_ref, kpec_ref, o_ref, kcat_ref):
    i = pl.program_id(2)
    scale = np.float32((MLA_NOPE + MLA_ROPE) ** -0.5)

    @pl.when(i == 0)
    def _():
        kcat_ref[:SEQ, :HEAD_DIM] = kn_ref[...]
        kcat_ref[:SEQ, HEAD_DIM:] = kpe_ref[...]
        kcat_ref[SEQ:, :HEAD_DIM] = knc_ref[...]
        kcat_ref[SEQ:, HEAD_DIM:] = kpec_ref[...]

    q = q_ref[...]

    @pl.when(i < MLA_LAT_BLOCKS)
    def _():
        s = _nt_dot(q, kcat_ref[...]) * scale
        e = jnp.exp(s - jnp.max(s, axis=-1, keepdims=True))
        o = (jnp.dot(e[:, :SEQ].astype(BF16), v_ref[...], preferred_element_type=F32)
             + jnp.dot(e[:, SEQ:].astype(BF16), vc_ref[...], preferred_element_type=F32))
        o_ref[...] = (o / jnp.sum(e, axis=-1, keepdims=True)).astype(o_ref.dtype)

    @pl.when(i >= MLA_LAT_BLOCKS)
    def _():
        s = _nt_dot(q, kcat_ref[SEQ:, :]) * scale
        e = jnp.exp(s - jnp.max(s, axis=-1, keepdims=True))
        o = jnp.dot(e.astype(BF16), vc_ref[...], preferred_element_type=F32)
        o_ref[...] = (o / jnp.sum(e, axis=-1, keepdims=True)).astype(o_ref.dtype)


def _mla_attention(q, kv, kpe):
    nq = MLA_LAT_BLOCKS + 1

    def qrow(b, h, i):
        return jnp.where(i < MLA_LAT_BLOCKS, b * MLA_LAT_BLOCKS + i, CTX_ROW_BLOCK0 + b)

    return pl.pallas_call(
        _mla_kernel,
        grid=(BATCH, HEADS, nq),
        in_specs=[
            pl.BlockSpec((MLA_QBLOCK, MLA_QK), lambda b, h, i: (qrow(b, h, i), h)),
            pl.BlockSpec((SEQ, HEAD_DIM), lambda b, h, i: (b, 2 * h)),
            pl.BlockSpec((SEQ, HEAD_DIM), lambda b, h, i: (b, 2 * h + 1)),
            pl.BlockSpec((SEQ, HEAD_DIM), lambda b, h, i: (b, 0)),
            pl.BlockSpec((CTX_LEN, HEAD_DIM), lambda b, h, i: (CTX_ROW_BLOCK0 + b, 2 * h)),
            pl.BlockSpec((CTX_LEN, HEAD_DIM), lambda b, h, i: (CTX_ROW_BLOCK0 + b, 2 * h + 1)),
            pl.BlockSpec((CTX_LEN, HEAD_DIM), lambda b, h, i: (CTX_ROW_BLOCK0 + b, 0)),
        ],
        out_specs=pl.BlockSpec((MLA_QBLOCK, HEAD_DIM), lambda b, h, i: (qrow(b, h, i), h)),
        out_shape=jax.ShapeDtypeStruct((T_ALL, D), BF16),
        scratch_shapes=[pltpu.VMEM((SEQ + CTX_LEN, MLA_QK), BF16)],
        compiler_params=_params("parallel", "parallel", "arbitrary"),
        name="mla_attention",
    )(q, kv, kv, kpe, kv, kv, kpe)


def _rope_tables():
    half = MLA_ROPE // 2
    freqs = ROPE_THETA ** (-jnp.arange(0, half, 2, dtype=F32) / half)
    t = jnp.arange(SEQ)
    rows = (t // GRID_W).astype(F32)[:, None] * freqs
    cols = (t % GRID_W).astype(F32)[:, None] * freqs
    cr, sr, cc, sn = jnp.cos(rows), jnp.sin(rows), jnp.cos(cols), jnp.sin(cols)
    pad = jnp.zeros((SEQ, HEAD_DIM - MLA_ROPE), F32)
    cos = jnp.concatenate([cr, cr, cc, cc, pad], axis=1)
    sin = jnp.concatenate([-sr, sr, -sn, sn, pad], axis=1)
    cos_c = jnp.concatenate([jnp.ones((T_CTX, MLA_ROPE), F32), jnp.zeros((T_CTX, HEAD_DIM - MLA_ROPE), F32)], axis=1)
    return (jnp.concatenate([jnp.tile(cos, (BATCH, 1)), cos_c], axis=0),
            jnp.concatenate([jnp.tile(sin, (BATCH, 1)), jnp.zeros((T_CTX, HEAD_DIM), F32)], axis=0))


def _swap_rope_halves(w):
    q = MLA_ROPE // 4
    return jnp.concatenate([w[..., q:2 * q], w[..., :q], w[..., 3 * q:], w[..., 2 * q:3 * q]], axis=-1)


def _sgu_kernel(u_ref, v_ref, g_ref, b_ref, ws_ref, bs_ref, o_ref):
    v = v_ref[...]
    mu = jnp.mean(v, axis=-1, keepdims=True)
    vc = v - mu
    var = jnp.mean(vc * vc, axis=-1, keepdims=True)
    vn = ((vc * lax.rsqrt(var + EPS)) * g_ref[...] + b_ref[...]).astype(BF16)
    for g in range(SG_GROUPS):
        cols = slice(g * CHUNK, (g + 1) * CHUNK)
        mix = jnp.dot(ws_ref[g], vn[:, cols], preferred_element_type=F32) + bs_ref[:, cols]
        o_ref[:, cols] = (u_ref[:, cols] * mix).astype(o_ref.dtype)


def _sgu(uv, ln_g, ln_b, ws, bs_full):
    rows = uv.shape[0]
    return pl.pallas_call(
        _sgu_kernel,
        grid=(rows // CHUNK,),
        in_specs=[
            pl.BlockSpec((CHUNK, D), lambda m: (m, 0)),
            pl.BlockSpec((CHUNK, D), lambda m: (m, 1)),
            pl.BlockSpec((1, D), lambda m: (0, 0)),
            pl.BlockSpec((1, D), lambda m: (0, 0)),
            pl.BlockSpec((SG_GROUPS, CHUNK, CHUNK), lambda m: (0, 0, 0)),
            pl.BlockSpec((CHUNK, D), lambda m: (0, 0)),
        ],
        out_specs=pl.BlockSpec((CHUNK, D), lambda m: (m, 0)),
        out_shape=jax.ShapeDtypeStruct((rows, D), BF16),
        compiler_params=_params("parallel"),
        name="sgu",
    )(uv, uv, ln_g.reshape(1, D), ln_b.reshape(1, D), ws, bs_full)


def _conv_gate_kernel(ua_ref, ug_ref, pa_ref, pg_ref, na_ref, ng_ref, wa_ref, wg_ref, ba_ref, bg_ref, o_ref, *, tm):
    row = pl.program_id(0) * tm + lax.broadcasted_iota(jnp.int32, (tm, 1), 0)
    seq_pos = jnp.where(row < T_LAT, row % SEQ, row % CTX_LEN)
    seq_len = jnp.where(row < T_LAT, SEQ, CTX_LEN)
    local = lax.broadcasted_iota(jnp.int32, (tm, 1), 0)

    def conv(u_ref, p_ref, n_ref, w_ref, b_ref):
        cur = u_ref[...]
        up = jnp.where(local == 0, p_ref[7:8, :], pltpu.roll(cur, 1, axis=0))
        up = jnp.where(seq_pos == 0, 0.0, up)
        dn = jnp.where(local == tm - 1, n_ref[0:1, :], pltpu.roll(cur, tm - 1, axis=0))
        dn = jnp.where(seq_pos == seq_len - 1, 0.0, dn)
        return up * w_ref[0:1, :] + cur * w_ref[1:2, :] + dn * w_ref[2:3, :] + b_ref[...]

    a = conv(ua_ref, pa_ref, na_ref, wa_ref, ba_ref)
    g = conv(ug_ref, pg_ref, ng_ref, wg_ref, bg_ref)
    o_ref[...] = (a * (g * jax.nn.sigmoid(g))).astype(o_ref.dtype)


def _conv_gate(u, conv_w, conv_b, *, rows):
    tm, tf = 512, 512
    nf = D_FF // tf
    last8 = rows // 8 - 1

    def prev8(m, f):
        return jnp.maximum(m * (tm // 8) - 1, 0)

    def next8(m, f):
        return jnp.minimum((m + 1) * (tm // 8), last8)

    cb = conv_b.reshape(1, 2 * D_FF)
    return pl.pallas_call(
        functools.partial(_conv_gate_kernel, tm=tm),
        grid=(rows // tm, nf),
        in_specs=[
            pl.BlockSpec((tm, tf), lambda m, f: (m, f)),
            pl.BlockSpec((tm, tf), lambda m, f: (m, f + nf)),
            pl.BlockSpec((8, tf), lambda m, f: (prev8(m, f), f)),
            pl.BlockSpec((8, tf), lambda m, f: (prev8(m, f), f + nf)),
            pl.BlockSpec((8, tf), lambda m, f: (next8(m, f), f)),
            pl.BlockSpec((8, tf), lambda m, f: (next8(m, f), f + nf)),
            pl.BlockSpec((CONV_W, tf), lambda m, f: (0, f)),
            pl.BlockSpec((CONV_W, tf), lambda m, f: (0, f + nf)),
            pl.BlockSpec((1, tf), lambda m, f: (0, f)),
            pl.BlockSpec((1, tf), lambda m, f: (0, f + nf)),
        ],
        out_specs=pl.BlockSpec((tm, tf), lambda m, f: (m, f)),
        out_shape=jax.ShapeDtypeStruct((rows, D_FF), BF16),
        compiler_params=_params("parallel", "parallel"),
        name="conv_gate",
    )(u, u, u, u, u, u, conv_w, conv_w, cb, cb)


def _final_norm_kernel(x_ref, g_ref, o_ref):
    xf = x_ref[...]
    o_ref[...] = (xf * lax.rsqrt(jnp.mean(xf * xf, axis=-1, keepdims=True) + EPS)) * g_ref[...]


def _final_norm(x, g):
    tm = 512
    return pl.pallas_call(
        _final_norm_kernel,
        grid=(T_LAT // tm,),
        in_specs=[pl.BlockSpec((tm, D), lambda m: (m, 0)), pl.BlockSpec((1, D), lambda m: (0, 0))],
        out_specs=pl.BlockSpec((tm, D), lambda m: (m, 0)),
        out_shape=jax.ShapeDtypeStruct((T_LAT, D), F32),
        compiler_params=_params("parallel"),
        name="final_norm",
    )(x, g.reshape(1, D))


def kernel(x, c, ctx, c_ctx, ada_w, ada_b, norm_g, final_g, a_w_qkv, a_w_o, a_rpb, b_w_in, b_q_norm, b_kv_norm,
           b_w_uq, b_w_ukv, b_w_o, c_w_in, c_b_in, c_ln_g, c_ln_b, c_ws, c_bs, c_w_o, f_w_in, f_conv_w, f_conv_b,
           f_w_out):
    xs = jnp.concatenate([x.reshape(T_LAT, D), ctx.reshape(T_CTX, D)], axis=0)
    cond = jnp.concatenate([c, c_ctx[None, :], jnp.zeros((8 - BATCH - 1, D), F32)], axis=0)
    mods_all = _ada(cond, ada_w, ada_b).reshape(DEPTH, 8, 1, 6 * D)

    for i in range(DEPTH):
        kind, j = i % N_MIXERS, i // N_MIXERS
        last = i == DEPTH - 1
        mods = mods_all[i]
        rows_out = T_LAT if last else T_ALL

        if kind == 0:
            qkv = _mod_matmul(xs, norm_g[i, 0], mods, 0, a_w_qkv[j].astype(BF16), rows=T_ALL, tm=512, tn=512,
                              out_dtype=BF16, name="na_qkv")
            att = _na_attention(qkv, _na_bias_table(a_rpb[j]), with_ctx_queries=not last, out_rows=T_ALL)
            w_o = a_w_o[j]
        elif kind == 1:
            w_in = b_w_in[j]
            c0 = MLA_Q_RANK + MLA_KV_RANK
            zpad = jnp.zeros((D, HEAD_DIM - MLA_ROPE), F32)
            w_in_ext = jnp.concatenate([w_in, zpad, _swap_rope_halves(w_in[:, c0:]), zpad], axis=1)
            z = _mod_matmul(xs, norm_g[i, 0], mods, 0, w_in_ext.astype(BF16), rows=T_ALL, tm=512, tn=MLA_ZCOLS // 2,
                            out_dtype=F32, name="mla_in")
            wq = b_w_uq[j].reshape(MLA_Q_RANK, HEADS, MLA_NOPE + MLA_ROPE)
            hpad = jnp.zeros((MLA_Q_RANK, HEADS, HEAD_DIM - MLA_ROPE), F32)
            wq_cat = jnp.concatenate([wq, hpad], axis=-1).reshape(MLA_Q_RANK, HEADS * MLA_QK)
            wq_swap = jnp.concatenate([_swap_rope_halves(wq[..., MLA_NOPE:]), hpad], axis=-1)
            wq_swap = wq_swap.reshape(MLA_Q_RANK, HEADS * HEAD_DIM)
            cos, sin = _rope_tables()
            q, kv, kpe = _mla_up(z, b_q_norm[j], b_kv_norm[j], cos, sin, wq_cat.astype(BF16), wq_swap.astype(BF16),
                                 b_w_ukv[j].astype(BF16))
            att = _mla_attention(q, kv, kpe)
            w_o = b_w_o[j]
        else:
            uv = _mod_matmul(xs, norm_g[i, 0], mods, 0, c_w_in[j].astype(BF16), rows=T_ALL, tm=512, tn=512,
                             out_dtype=F32, bias=c_b_in[j], act="gelu", name="sgu_in")
            bs_full = jnp.repeat(c_bs[j].T, CHUNK, axis=1)
            att = _sgu(uv, c_ln_g[j], c_ln_b[j], c_ws[j].astype(BF16), bs_full)
            w_o = c_w_o[j]

        xs = _matmul_resid(att, w_o.astype(BF16), xs, mods, 2, rows=rows_out, tm=512, tn=512, name="mixer_out")
        u = _mod_matmul(xs, norm_g[i, 1], mods, 3, f_w_in[i].astype(BF16), rows=rows_out, tm=512, tn=512,
                        out_dtype=F32, name="ffn_in")
        act = _conv_gate(u, f_conv_w[i], f_conv_b[i], rows=rows_out)
        xs = _matmul_resid(act, f_w_out[i].astype(BF16), xs, mods, 5, rows=rows_out, tm=512, tn=512, name="ffn_out")

    return _final_norm(xs, final_g).reshape(BATCH, SEQ, D)
```

```python
import functools

import jax
import jax.numpy as jnp
import numpy as np
from jax import lax
from jax.experimental import pallas as pl
from jax.experimental.pallas import tpu as pltpu

F32 = jnp.float32
BF16 = jnp.bfloat16

D = 2048
BATCH = 4
SEQ = 2048
DEPTH = 4
GRID_W = 64
GRID_H = SEQ // GRID_W
CTX_LEN = 256
N_MIXERS = 3
EPS = 1e-6
ROPE_THETA = 10000.0
HEADS = 16
HEAD_DIM = 128
WIN_H = 8
WIN_W = 16
MLA_NOPE = 128
MLA_ROPE = 64
MLA_Q_RANK = 512
MLA_KV_RANK = 512
CHUNK = 128
SG_GROUPS = 16
D_FF = 5632
CONV_W = 3

T_LAT = BATCH * SEQ
T_CTX = BATCH * CTX_LEN
T_ALL = T_LAT + T_CTX
CTX_ROW_BLOCK0 = T_LAT // CTX_LEN
MASK_VALUE = -1e30
VMEM_LIMIT = 56 * 1024 * 1024
LANES = 128
BF16_ROWS = 16
PROLOGUE_ROWS = 256


def _params(*semantics):
    return pltpu.CompilerParams(dimension_semantics=semantics, vmem_limit_bytes=VMEM_LIMIT)


def _mod_row(tm):
    per_batch = SEQ // tm
    return lambda m: jnp.minimum(m // per_batch, BATCH)


def _resident(block_shape, index_map):
    return pl.BlockSpec(block_shape, index_map, pipeline_mode=pl.Buffered(1))


def _ada_kernel(s_ref, w_ref, b_ref, o_ref):
    s = s_ref[...]
    s = s * jax.nn.sigmoid(s)
    o_ref[...] = jnp.dot(s.astype(BF16), w_ref[...].astype(BF16), preferred_element_type=F32) + b_ref[...]


def _ada(s, ada_w, ada_b):
    tn = 1024
    n6 = 6 * D
    return pl.pallas_call(
        _ada_kernel,
        grid=(DEPTH, n6 // tn),
        in_specs=[
            pl.BlockSpec((8, D), lambda i, n: (0, 0)),
            pl.BlockSpec((None, D, tn), lambda i, n: (i, 0, n)),
            pl.BlockSpec((None, 1, tn), lambda i, n: (i, 0, n)),
        ],
        out_specs=pl.BlockSpec((None, 8, tn), lambda i, n: (i, 0, n)),
        out_shape=jax.ShapeDtypeStruct((DEPTH, 8, n6), F32),
        compiler_params=_params("parallel", "parallel"),
        name="ada",
    )(s, ada_w, ada_b.reshape(DEPTH, 1, n6))


def _gelu_exact(z):
    return 0.5 * z * (1.0 + lax.erf(z * np.float32(np.sqrt(0.5))))


def _modulate(xf, g, shift, scale):
    y = xf * lax.rsqrt(jnp.mean(xf * xf, axis=-1, keepdims=True) + EPS)
    return (y * g) * (1.0 + scale) + shift


def _modulate_rows(x_ref, h_ref, h_row0, n_rows, g, shift, scale):
    def body(c, carry):
        r = pl.multiple_of(c * PROLOGUE_ROWS, PROLOGUE_ROWS)
        h = _modulate(x_ref[pl.ds(r, PROLOGUE_ROWS), :], g, shift, scale)
        h_ref[pl.ds(h_row0 + r, PROLOGUE_ROWS), :] = h.astype(BF16)
        return carry
    lax.fori_loop(0, n_rows // PROLOGUE_ROWS, body, 0)


def _mod_mm_kernel(x_ref, g_ref, sh_ref, sc_ref, w_ref, *rest, has_bias, act, tm):
    if has_bias:
        b_ref, o_ref, h_ref = rest
    else:
        o_ref, h_ref = rest

    @pl.when(pl.program_id(1) == 0)
    def _():
        _modulate_rows(x_ref, h_ref, 0, tm, g_ref[...], sh_ref[...], sc_ref[...])

    acc = jnp.dot(h_ref[...], w_ref[...], preferred_element_type=F32)
    if has_bias:
        acc = acc + b_ref[...]
    if act == "gelu":
        acc = _gelu_exact(acc)
    o_ref[...] = acc.astype(o_ref.dtype)


def _mod_matmul(x, norm_g, g_row, mods, shift_col, w, w_layer, *, tm, tn, out_dtype, bias=None, act=None, name):
    n = w.shape[2]
    brow = _mod_row(tm)
    in_specs = [
        pl.BlockSpec((tm, D), lambda m, j: (m, 0)),
        pl.BlockSpec((None, 1, D), lambda m, j: (g_row, 0, 0)),
        pl.BlockSpec((None, 1, D), lambda m, j: (brow(m), 0, shift_col)),
        pl.BlockSpec((None, 1, D), lambda m, j: (brow(m), 0, shift_col + 1)),
        pl.BlockSpec((None, D, tn), lambda m, j: (w_layer, 0, j)),
    ]
    args = [x, norm_g, mods, mods, w]
    if bias is not None:
        in_specs.append(pl.BlockSpec((None, 1, tn), lambda m, j: (w_layer, 0, j)))
        args.append(bias.reshape(bias.shape[0], 1, n))
    return pl.pallas_call(
        functools.partial(_mod_mm_kernel, has_bias=bias is not None, act=act, tm=tm),
        grid=(T_ALL // tm, n // tn),
        in_specs=in_specs,
        out_specs=pl.BlockSpec((tm, tn), lambda m, j: (m, j)),
        out_shape=jax.ShapeDtypeStruct((T_ALL, n), out_dtype),
        scratch_shapes=[pltpu.VMEM((tm, D), BF16)],
        compiler_params=_params("parallel", "arbitrary"),
        name=name,
    )(*args)


MIXER_OUT_COLS = 512


def _mixer_out_kernel(a_ref, w_ref, r_ref, gate_ref, o_ref):
    a = a_ref[...]
    for c in range(D // MIXER_OUT_COLS):
        cols = slice(c * MIXER_OUT_COLS, (c + 1) * MIXER_OUT_COLS)
        acc = jnp.dot(a, w_ref[:, cols], preferred_element_type=F32)
        o_ref[:, cols] = r_ref[:, cols] + gate_ref[:, cols] * acc


def _mixer_out(a, w, w_layer, resid, mods, *, rows):
    tm = 512
    brow = _mod_row(tm)
    return pl.pallas_call(
        _mixer_out_kernel,
        grid=(rows // tm,),
        in_specs=[
            pl.BlockSpec((tm, D), lambda m: (m, 0)),
            _resident((None, D, D), lambda m: (w_layer, 0, 0)),
            pl.BlockSpec((tm, D), lambda m: (m, 0)),
            pl.BlockSpec((None, 1, D), lambda m: (brow(m), 0, 2)),
        ],
        out_specs=pl.BlockSpec((tm, D), lambda m: (m, 0)),
        out_shape=jax.ShapeDtypeStruct((rows, D), F32),
        compiler_params=_params("parallel"),
        name="mixer_out",
    )(a, w, resid, mods)


FFN_TM = 1024
FFN_TF = 256
FFN_HALO = BF16_ROWS
FFN_OUT_COLS = 512
FFN_ROW_CHUNK = 256
FFN_LOOKAHEAD = 2
FFN_SHIFT_PAD = 8


def _ffn_kernel(x_ref, xp_ref, xn_ref, g_ref, sh_ref, sc_ref, gate_ref, wa_ref, wg_ref, cwa_ref, cwg_ref,
                cba_ref, cbg_ref, wo_ref, o_ref, h_ref, ua_ref, ug_ref, ns_ref, ne_ref):
    tm, halo = FFN_TM, FFN_HALO
    f = pl.program_id(1)

    @pl.when(f == 0)
    def _():
        g, sh, sc = g_ref[...], sh_ref[...], sc_ref[...]
        h_ref[:halo, :] = _modulate(xp_ref[...], g, sh, sc).astype(BF16)
        h_ref[halo + tm:, :] = _modulate(xn_ref[...], g, sh, sc).astype(BF16)
        _modulate_rows(x_ref, h_ref, halo, tm, g, sh, sc)
        row = pl.program_id(0) * tm + lax.broadcasted_iota(jnp.int32, (tm, FFN_TF), 0)
        is_lat = row < T_LAT
        pos = jnp.where(is_lat, row & (SEQ - 1), row & (CTX_LEN - 1))
        last = jnp.where(is_lat, SEQ - 1, CTX_LEN - 1)
        ns_ref[...] = jnp.where(pos == 0, 0.0, 1.0).astype(F32)
        ne_ref[...] = jnp.where(pos == last, 0.0, 1.0).astype(F32)
        o_ref[...] = jnp.zeros_like(o_ref)

    ch, n_ch, pad = FFN_ROW_CHUNK, tm // FFN_ROW_CHUNK, FFN_SHIFT_PAD

    def hidden(r):
        lo = 0 if r == 0 else halo + r * ch
        hi = tm + 2 * halo if r == n_ch - 1 else halo + (r + 1) * ch
        h = h_ref[lo:hi, :]
        ua_ref[lo:hi, :] = jnp.dot(h, wa_ref[...], preferred_element_type=F32)
        ug_ref[lo:hi, :] = jnp.dot(h, wg_ref[...], preferred_element_type=F32)

    def gate_and_project(r):
        base = halo + r * ch
        rows = slice(r * ch, (r + 1) * ch)

        def conv(u_ref, cw_ref, cb_ref):
            u = u_ref[base - pad:base + ch + pad, :]
            up = pltpu.roll(u, 1, axis=0)[pad:pad + ch, :] * ns_ref[rows, :]
            dn = pltpu.roll(u, ch + 2 * pad - 1, axis=0)[pad:pad + ch, :] * ne_ref[rows, :]
            return up * cw_ref[0:1, :] + u[pad:pad + ch, :] * cw_ref[1:2, :] + dn * cw_ref[2:3, :] + cb_ref[...]

        a = conv(ua_ref, cwa_ref, cba_ref)
        gg = conv(ug_ref, cwg_ref, cbg_ref)
        act = (a * (gg * jax.nn.sigmoid(gg))).astype(BF16)
        for c in range(D // FFN_OUT_COLS):
            cols = slice(c * FFN_OUT_COLS, (c + 1) * FFN_OUT_COLS)
            o_ref[rows, cols] += jnp.dot(act, wo_ref[:, cols], preferred_element_type=F32)

    for r in range(min(FFN_LOOKAHEAD, n_ch)):
        hidden(r)
    for r in range(n_ch):
        if r + FFN_LOOKAHEAD < n_ch:
            hidden(r + FFN_LOOKAHEAD)
        gate_and_project(r)

    @pl.when(f == pl.num_programs(1) - 1)
    def _():
        gate = gate_ref[...]

        def body(c, carry):
            rows = pl.ds(pl.multiple_of(c * PROLOGUE_ROWS, PROLOGUE_ROWS), PROLOGUE_ROWS)
            o_ref[rows, :] = x_ref[rows, :] + gate * o_ref[rows, :]
            return carry
        lax.fori_loop(0, tm // PROLOGUE_ROWS, body, 0)


def _ffn(x, norm_g, g_row, mods, w_in, conv_w, conv_b, w_out, layer, *, rows):
    tm, tf, halo = FFN_TM, FFN_TF, FFN_HALO
    nf = D_FF // tf
    brow = _mod_row(tm)
    last_halo = rows // halo - 1
    cb = conv_b.reshape(DEPTH, 1, 2 * D_FF)
    return pl.pallas_call(
        _ffn_kernel,
        grid=(rows // tm, nf),
        in_specs=[
            pl.BlockSpec((tm, D), lambda m, f: (m, 0)),
            pl.BlockSpec((halo, D), lambda m, f: (jnp.maximum(m * (tm // halo) - 1, 0), 0)),
            pl.BlockSpec((halo, D), lambda m, f: (jnp.minimum((m + 1) * (tm // halo), last_halo), 0)),
            pl.BlockSpec((None, 1, D), lambda m, f: (g_row, 0, 0)),
            pl.BlockSpec((None, 1, D), lambda m, f: (brow(m), 0, 3)),
            pl.BlockSpec((None, 1, D), lambda m, f: (brow(m), 0, 4)),
            pl.BlockSpec((None, 1, D), lambda m, f: (brow(m), 0, 5)),
            pl.BlockSpec((None, D, tf), lambda m, f: (layer, 0, f)),
            pl.BlockSpec((None, D, tf), lambda m, f: (layer, 0, f + nf)),
            pl.BlockSpec((None, CONV_W, tf), lambda m, f: (layer, 0, f)),
            pl.BlockSpec((None, CONV_W, tf), lambda m, f: (layer, 0, f + nf)),
            pl.BlockSpec((None, 1, tf), lambda m, f: (layer, 0, f)),
            pl.BlockSpec((None, 1, tf), lambda m, f: (layer, 0, f + nf)),
            pl.BlockSpec((None, tf, D), lambda m, f: (layer, f, 0)),
        ],
        out_specs=pl.BlockSpec((tm, D), lambda m, f: (m, 0)),
        out_shape=jax.ShapeDtypeStruct((rows, D), F32),
        scratch_shapes=[
            pltpu.VMEM((tm + 2 * halo, D), BF16),
            pltpu.VMEM((tm + 2 * halo, tf), F32),
            pltpu.VMEM((tm + 2 * halo, tf), F32),
            pltpu.VMEM((tm, tf), F32),
            pltpu.VMEM((tm, tf), F32),
        ],
        compiler_params=_params("parallel", "arbitrary"),
        name="ffn",
    )(x, x, x, norm_g, mods, mods, mods, w_in, w_in, conv_w, conv_w, cb, cb, w_out)


NA_QROWS = 4
NA_QBLOCK = NA_QROWS * GRID_W
NA_SUB = 2
NA_BAND_ROWS = 12
NA_BAND = NA_BAND_ROWS * GRID_W


def _nt_dot(a, b):
    return lax.dot_general(a, b, (((1,), (1,)), ((), ())), preferred_element_type=F32)


def _na_lat_kernel(q_ref, k_ref, v_ref, kc_ref, vc_ref, bias_ref, o_ref):
    i = pl.program_id(2)
    scale = np.float32(HEAD_DIM ** -0.5)
    kc = kc_ref[...]
    vc = vc_ref[...]
    left = lax.broadcasted_iota(jnp.int32, (1, 2 * GRID_W), 1) < GRID_W

    for sb in range(NA_SUB):
        r0 = (i * NA_SUB + sb) * NA_QROWS
        u0 = jnp.clip(r0 - WIN_H // 2, 0, GRID_H - NA_BAND_ROWS)
        key0 = pl.multiple_of(u0 * GRID_W, NA_QROWS * GRID_W)
        q = q_ref[sb * NA_QBLOCK:(sb + 1) * NA_QBLOCK, :]
        kb = k_ref[pl.ds(key0, NA_BAND), :]
        vb = v_ref[pl.ds(key0, NA_BAND), :]
        bias_rows = []
        for j in range(NA_QROWS):
            r = r0 + j
            rs = jnp.clip(r - WIN_H // 2, 0, GRID_H - WIN_H)
            tiles = []
            for p in range(NA_BAND_ROWS // 2):
                kr = u0 + 2 * p
                pen_l = jnp.where((kr >= rs) & (kr < rs + WIN_H), 0.0, MASK_VALUE).astype(F32)
                pen_r = jnp.where((kr + 1 >= rs) & (kr + 1 < rs + WIN_H), 0.0, MASK_VALUE).astype(F32)
                pair = jnp.clip(kr - r + WIN_H, 0, 2 * WIN_H - 1)
                tiles.append(bias_ref[pair] + jnp.where(left, pen_l, pen_r))
            bias_rows.append(jnp.concatenate(tiles, axis=1))
        s = _nt_dot(q, kb) * scale + jnp.concatenate(bias_rows, axis=0)
        sc = _nt_dot(q, kc) * scale
        mx = jnp.maximum(jnp.max(s, axis=-1, keepdims=True), jnp.max(sc, axis=-1, keepdims=True))
        e = jnp.exp(s - mx)
        ec = jnp.exp(sc - mx)
        denom = jnp.sum(e, axis=-1, keepdims=True) + jnp.sum(ec, axis=-1, keepdims=True)
        o = (jnp.dot(e.astype(BF16), vb, preferred_element_type=F32)
             + jnp.dot(ec.astype(BF16), vc, preferred_element_type=F32))
        o_ref[sb * NA_QBLOCK:(sb + 1) * NA_QBLOCK, :] = (o / denom).astype(o_ref.dtype)


def _na_bias_table(rpb):
    w = np.arange(GRID_W)
    col0 = np.clip(w - WIN_W // 2, 0, GRID_W - WIN_W)
    kcol = np.arange(GRID_W)
    in_win = (kcol[None, :] >= col0[:, None]) & (kcol[None, :] < col0[:, None] + WIN_W)
    pad = GRID_W - WIN_W
    padded = jnp.pad(rpb, ((0, 0), (0, 0), (pad, pad)))
    base = pad + WIN_W - 1
    t = jnp.stack([padded[:, :, base - wq:base - wq + GRID_W] for wq in range(GRID_W)], axis=2)
    t = jnp.where(jnp.asarray(in_win), t, MASK_VALUE)
    zero = jnp.zeros_like(t[:, :1])
    t = jnp.concatenate([zero, t, zero], axis=1)
    return jnp.concatenate([t[:, :-1], t[:, 1:]], axis=-1)


def _na_lat_attention(qkv, bias_tab):
    blocks = SEQ // (NA_SUB * NA_QBLOCK)
    qb = NA_SUB * NA_QBLOCK
    return pl.pallas_call(
        _na_lat_kernel,
        grid=(BATCH, HEADS, blocks),
        in_specs=[
            pl.BlockSpec((qb, HEAD_DIM), lambda b, h, i: (b * blocks + i, h)),
            pl.BlockSpec((SEQ, HEAD_DIM), lambda b, h, i: (b, HEADS + h)),
            pl.BlockSpec((SEQ, HEAD_DIM), lambda b, h, i: (b, 2 * HEADS + h)),
            pl.BlockSpec((CTX_LEN, HEAD_DIM), lambda b, h, i: (CTX_ROW_BLOCK0 + b, HEADS + h)),
            pl.BlockSpec((CTX_LEN, HEAD_DIM), lambda b, h, i: (CTX_ROW_BLOCK0 + b, 2 * HEADS + h)),
            pl.BlockSpec((None, 2 * WIN_H, GRID_W, 2 * GRID_W), lambda b, h, i: (h, 0, 0, 0)),
        ],
        out_specs=pl.BlockSpec((qb, HEAD_DIM), lambda b, h, i: (b * blocks + i, h)),
        out_shape=jax.ShapeDtypeStruct((T_ALL, D), BF16),
        compiler_params=_params("parallel", "parallel", "arbitrary"),
        name="na_lat",
    )(qkv, qkv, qkv, qkv, qkv, bias_tab)


def _softmax_pv(s, v):
    e = jnp.exp(s - jnp.max(s, axis=-1, keepdims=True))
    o = jnp.dot(e.astype(BF16), v, preferred_element_type=F32)
    return o / jnp.sum(e, axis=-1, keepdims=True)


def _na_ctx_kernel(q_ref, k_ref, v_ref, att_ref, o_ref):
    del att_ref
    scale = np.float32(HEAD_DIM ** -0.5)
    for h in range(HEADS):
        cols = slice(h * HEAD_DIM, (h + 1) * HEAD_DIM)
        s = _nt_dot(q_ref[:, cols], k_ref[:, cols]) * scale
        o_ref[:, cols] = _softmax_pv(s, v_ref[:, cols]).astype(o_ref.dtype)


def _na_ctx_attention(qkv, att):
    return pl.pallas_call(
        _na_ctx_kernel,
        grid=(BATCH,),
        in_specs=[
            pl.BlockSpec((CTX_LEN, D), lambda b: (CTX_ROW_BLOCK0 + b, 0)),
            pl.BlockSpec((CTX_LEN, D), lambda b: (CTX_ROW_BLOCK0 + b, 1)),
            pl.BlockSpec((CTX_LEN, D), lambda b: (CTX_ROW_BLOCK0 + b, 2)),
            pl.BlockSpec(memory_space=pl.ANY),
        ],
        out_specs=pl.BlockSpec((CTX_LEN, D), lambda b: (CTX_ROW_BLOCK0 + b, 0)),
        out_shape=jax.ShapeDtypeStruct((T_ALL, D), BF16),
        input_output_aliases={3: 0},
        compiler_params=_params("parallel"),
        name="na_ctx",
    )(qkv, qkv, qkv, att)


MLA_QK = 2 * HEAD_DIM
MLA_QBLOCK = 256
MLA_SUB = 2
MLA_ZCOLS = MLA_Q_RANK + MLA_KV_RANK + 2 * HEAD_DIM
MLA_UP_TM = 512


def _rms(z, g):
    return (z * lax.rsqrt(jnp.mean(z * z, axis=-1, keepdims=True) + EPS)) * g


def _mla_up_kernel(z_ref, qn_ref, kvn_ref, cos_ref, sin_ref, wq_ref, wqs_ref, wkv_ref, q_ref, kv_ref, kpe_ref):
    cq = _rms(z_ref[:, :MLA_Q_RANK], qn_ref[...]).astype(BF16)
    ckv = _rms(z_ref[:, MLA_Q_RANK:MLA_Q_RANK + MLA_KV_RANK], kvn_ref[...]).astype(BF16)
    c0 = MLA_Q_RANK + MLA_KV_RANK
    cos = cos_ref[...]
    sin = sin_ref[...]
    kpe_ref[...] = (z_ref[:, c0:c0 + HEAD_DIM] * cos + z_ref[:, c0 + HEAD_DIM:] * sin).astype(BF16)
    for h in range(HEADS):
        q = jnp.dot(cq, wq_ref[h], preferred_element_type=F32)
        qs = jnp.dot(cq, wqs_ref[h], preferred_element_type=F32)
        q_ref[h, :, :HEAD_DIM] = q[:, :HEAD_DIM].astype(BF16)
        q_ref[h, :, HEAD_DIM:] = (q[:, HEAD_DIM:] * cos + qs * sin).astype(BF16)
        kv_ref[h] = jnp.dot(ckv, wkv_ref[h], preferred_element_type=F32).astype(BF16)


def _mla_up(z, q_norm, kv_norm, cos, sin, wq, wqs, wkv):
    tm = MLA_UP_TM
    return pl.pallas_call(
        _mla_up_kernel,
        grid=(T_ALL // tm,),
        in_specs=[
            pl.BlockSpec((tm, MLA_ZCOLS), lambda m: (m, 0)),
            pl.BlockSpec((1, MLA_Q_RANK), lambda m: (0, 0)),
            pl.BlockSpec((1, MLA_KV_RANK), lambda m: (0, 0)),
            pl.BlockSpec((tm, HEAD_DIM), lambda m: (m, 0)),
            pl.BlockSpec((tm, HEAD_DIM), lambda m: (m, 0)),
            _resident((HEADS, MLA_Q_RANK, MLA_QK), lambda m: (0, 0, 0)),
            _resident((HEADS, MLA_Q_RANK, HEAD_DIM), lambda m: (0, 0, 0)),
            _resident((HEADS, MLA_KV_RANK, 2 * HEAD_DIM), lambda m: (0, 0, 0)),
        ],
        out_specs=[
            pl.BlockSpec((HEADS, tm, MLA_QK), lambda m: (0, m, 0)),
            pl.BlockSpec((HEADS, tm, 2 * HEAD_DIM), lambda m: (0, m, 0)),
            pl.BlockSpec((tm, HEAD_DIM), lambda m: (m, 0)),
        ],
        out_shape=[
            jax.ShapeDtypeStruct((HEADS, T_ALL, MLA_QK), BF16),
            jax.ShapeDtypeStruct((HEADS, T_ALL, 2 * HEAD_DIM), BF16),
            jax.ShapeDtypeStruct((T_ALL, HEAD_DIM), BF16),
        ],
        compiler_params=_params("parallel"),
        name="mla_up",
    )(z, q_norm.reshape(1, -1), kv_norm.reshape(1, -1), cos, sin, wq, wqs, wkv)


MLA_SCALE = np.float32((MLA_NOPE + MLA_ROPE) ** -0.5)
MLA_SCALE_LOG2E = np.float32((MLA_NOPE + MLA_ROPE) ** -0.5 * np.log2(np.e))


def _mla_lat_kernel(q_ref, kn_ref, v_ref, kpe_ref, knc_ref, vc_ref, kpec_ref, o_ref, kcat_ref):
    @pl.when(pl.program_id(2) == 0)
    def _():
        kcat_ref[:SEQ, :HEAD_DIM] = kn_ref[...]
        kcat_ref[:SEQ, HEAD_DIM:] = kpe_ref[...]
        kcat_ref[SEQ:, :HEAD_DIM] = knc_ref[...]
        kcat_ref[SEQ:, HEAD_DIM:] = kpec_ref[...]

    for sb in range(MLA_SUB):
        rows = slice(sb * MLA_QBLOCK, (sb + 1) * MLA_QBLOCK)
        s = _nt_dot(q_ref[rows, :], kcat_ref[...])
        e = jnp.exp2((s - jnp.max(s, axis=-1, keepdims=True)) * MLA_SCALE_LOG2E)
        o = (jnp.dot(e[:, :SEQ].astype(BF16), v_ref[...], preferred_element_type=F32)
             + jnp.dot(e[:, SEQ:].astype(BF16), vc_ref[...], preferred_element_type=F32))
        o_ref[rows, :] = (o / jnp.sum(e, axis=-1, keepdims=True)).astype(o_ref.dtype)


def _mla_lat_attention(q, kv, kpe):
    qb = MLA_SUB * MLA_QBLOCK
    blocks = SEQ // qb
    ctx_q = CTX_ROW_BLOCK0
    return pl.pallas_call(
        _mla_lat_kernel,
        grid=(BATCH, HEADS, blocks),
        in_specs=[
            pl.BlockSpec((None, qb, MLA_QK), lambda b, h, i: (h, b * blocks + i, 0)),
            pl.BlockSpec((None, SEQ, HEAD_DIM), lambda b, h, i: (h, b, 0)),
            pl.BlockSpec((None, SEQ, HEAD_DIM), lambda b, h, i: (h, b, 1)),
            pl.BlockSpec((SEQ, HEAD_DIM), lambda b, h, i: (b, 0)),
            pl.BlockSpec((None, CTX_LEN, HEAD_DIM), lambda b, h, i: (h, ctx_q + b, 0)),
            pl.BlockSpec((None, CTX_LEN, HEAD_DIM), lambda b, h, i: (h, ctx_q + b, 1)),
            pl.BlockSpec((CTX_LEN, HEAD_DIM), lambda b, h, i: (ctx_q + b, 0)),
        ],
        out_specs=pl.BlockSpec((qb, HEAD_DIM), lambda b, h, i: (b * blocks + i, h)),
        out_shape=jax.ShapeDtypeStruct((T_ALL, D), BF16),
        scratch_shapes=[pltpu.VMEM((SEQ + CTX_LEN, MLA_QK), BF16)],
        compiler_params=_params("parallel", "parallel", "arbitrary"),
        name="mla_lat",
    )(q, kv, kv, kpe, kv, kv, kpe)


def _mla_ctx_kernel(q_ref, kv_ref, kpe_ref, att_ref, o_ref):
    del att_ref
    kpe = kpe_ref[...]
    for h in range(HEADS):
        kcat = jnp.concatenate([kv_ref[h, :, :HEAD_DIM], kpe], axis=1)
        s = _nt_dot(q_ref[h], kcat) * MLA_SCALE
        o_ref[:, h * HEAD_DIM:(h + 1) * HEAD_DIM] = _softmax_pv(s, kv_ref[h, :, HEAD_DIM:]).astype(o_ref.dtype)


def _mla_ctx_attention(q, kv, kpe, att):
    ctx_q = CTX_ROW_BLOCK0
    return pl.pallas_call(
        _mla_ctx_kernel,
        grid=(BATCH,),
        in_specs=[
            pl.BlockSpec((HEADS, CTX_LEN, MLA_QK), lambda b: (0, ctx_q + b, 0)),
            pl.BlockSpec((HEADS, CTX_LEN, 2 * HEAD_DIM), lambda b: (0, ctx_q + b, 0)),
            pl.BlockSpec((CTX_LEN, HEAD_DIM), lambda b: (ctx_q + b, 0)),
            pl.BlockSpec(memory_space=pl.ANY),
        ],
        out_specs=pl.BlockSpec((CTX_LEN, D), lambda b: (ctx_q + b, 0)),
        out_shape=jax.ShapeDtypeStruct((T_ALL, D), BF16),
        input_output_aliases={3: 0},
        compiler_params=_params("parallel"),
        name="mla_ctx",
    )(q, kv, kpe, att)


def _rope_tables():
    half = MLA_ROPE // 2
    freqs = ROPE_THETA ** (-jnp.arange(0, half, 2, dtype=F32) / half)
    t = jnp.arange(SEQ)
    rows = (t // GRID_W).astype(F32)[:, None] * freqs
    cols = (t % GRID_W).astype(F32)[:, None] * freqs
    cr, sr, cc, sn = jnp.cos(rows), jnp.sin(rows), jnp.cos(cols), jnp.sin(cols)
    pad = jnp.zeros((SEQ, HEAD_DIM - MLA_ROPE), F32)
    cos = jnp.concatenate([cr, cr, cc, cc, pad], axis=1)
    sin = jnp.concatenate([-sr, sr, -sn, sn, pad], axis=1)
    cos_c = jnp.concatenate([jnp.ones((T_CTX, MLA_ROPE), F32), jnp.zeros((T_CTX, HEAD_DIM - MLA_ROPE), F32)], axis=1)
    return (jnp.concatenate([jnp.tile(cos, (BATCH, 1)), cos_c], axis=0),
            jnp.concatenate([jnp.tile(sin, (BATCH, 1)), jnp.zeros((T_CTX, HEAD_DIM), F32)], axis=0))


def _swap_rope_halves(w):
    q = MLA_ROPE // 4
    return jnp.concatenate([w[..., q:2 * q], w[..., :q], w[..., 3 * q:], w[..., 2 * q:3 * q]], axis=-1)


def _sgu_kernel(u_ref, v_ref, g_ref, b_ref, ws_ref, bs_ref, o_ref):
    v = v_ref[...]
    mu = jnp.mean(v, axis=-1, keepdims=True)
    vc = v - mu
    var = jnp.mean(vc * vc, axis=-1, keepdims=True)
    vn = ((vc * lax.rsqrt(var + EPS)) * g_ref[...] + b_ref[...]).astype(BF16)
    for g in range(SG_GROUPS):
        cols = slice(g * CHUNK, (g + 1) * CHUNK)
        mix = jnp.dot(ws_ref[g], vn[:, cols], preferred_element_type=F32) + bs_ref[:, cols]
        o_ref[:, cols] = (u_ref[:, cols] * mix).astype(o_ref.dtype)


def _sgu(uv, ln_g, ln_b, ws, bs_full):
    rows = uv.shape[0]
    return pl.pallas_call(
        _sgu_kernel,
        grid=(rows // CHUNK,),
        in_specs=[
            pl.BlockSpec((CHUNK, D), lambda m: (m, 0)),
            pl.BlockSpec((CHUNK, D), lambda m: (m, 1)),
            pl.BlockSpec((1, D), lambda m: (0, 0)),
            pl.BlockSpec((1, D), lambda m: (0, 0)),
            pl.BlockSpec((SG_GROUPS, CHUNK, CHUNK), lambda m: (0, 0, 0)),
            pl.BlockSpec((CHUNK, D), lambda m: (0, 0)),
        ],
        out_specs=pl.BlockSpec((CHUNK, D), lambda m: (m, 0)),
        out_shape=jax.ShapeDtypeStruct((rows, D), BF16),
        compiler_params=_params("parallel"),
        name="sgu",
    )(uv, uv, ln_g.reshape(1, D), ln_b.reshape(1, D), ws, bs_full)


def _final_norm_kernel(x_ref, g_ref, o_ref):
    xf = x_ref[...]
    o_ref[...] = (xf * lax.rsqrt(jnp.mean(xf * xf, axis=-1, keepdims=True) + EPS)) * g_ref[...]


def _final_norm(x, g):
    tm = 512
    return pl.pallas_call(
        _final_norm_kernel,
        grid=(T_LAT // tm,),
        in_specs=[pl.BlockSpec((tm, D), lambda m: (m, 0)), pl.BlockSpec((1, D), lambda m: (0, 0))],
        out_specs=pl.BlockSpec((tm, D), lambda m: (m, 0)),
        out_shape=jax.ShapeDtypeStruct((T_LAT, D), F32),
        compiler_params=_params("parallel"),
        name="final_norm",
    )(x, g.reshape(1, D))


def kernel(x, c, ctx, c_ctx, ada_w, ada_b, norm_g, final_g, a_w_qkv, a_w_o, a_rpb, b_w_in, b_q_norm, b_kv_norm,
           b_w_uq, b_w_ukv, b_w_o, c_w_in, c_b_in, c_ln_g, c_ln_b, c_ws, c_bs, c_w_o, f_w_in, f_conv_w, f_conv_b,
           f_w_out):
    xs = jnp.concatenate([x.reshape(T_LAT, D), ctx.reshape(T_CTX, D)], axis=0)
    cond = jnp.concatenate([c, c_ctx[None, :], jnp.zeros((8 - BATCH - 1, D), F32)], axis=0)
    mods_all = _ada(cond, ada_w, ada_b).reshape(DEPTH, 8, 1, 6 * D)
    norm_rows = norm_g.reshape(DEPTH * 2, 1, D)

    a_w_qkv_b, a_w_o_b = a_w_qkv.astype(BF16), a_w_o.astype(BF16)
    b_w_o_b, c_w_in_b, c_w_o_b = b_w_o.astype(BF16), c_w_in.astype(BF16), c_w_o.astype(BF16)
    f_w_in_b, f_w_out_b = f_w_in.astype(BF16), f_w_out.astype(BF16)

    for i in range(DEPTH):
        kind, j = i % N_MIXERS, i // N_MIXERS
        last = i == DEPTH - 1
        mods = mods_all[i]
        rows_out = T_LAT if last else T_ALL

        if kind == 0:
            qkv = _mod_matmul(xs, norm_rows, 2 * i, mods, 0, a_w_qkv_b, j, tm=1024, tn=512, out_dtype=BF16,
                              name="na_qkv")
            att = _na_lat_attention(qkv, _na_bias_table(a_rpb[j]))
            if not last:
                att = _na_ctx_attention(qkv, att)
            w_o = a_w_o_b
        elif kind == 1:
            w_in = b_w_in[j]
            c0 = MLA_Q_RANK + MLA_KV_RANK
            zpad = jnp.zeros((D, HEAD_DIM - MLA_ROPE), F32)
            w_in_ext = jnp.concatenate([w_in, zpad, _swap_rope_halves(w_in[:, c0:]), zpad], axis=1)
            z = _mod_matmul(xs, norm_rows, 2 * i, mods, 0, w_in_ext.astype(BF16)[None], 0, tm=1024,
                            tn=MLA_ZCOLS // 2, out_dtype=F32, name="mla_in")
            wq = b_w_uq[j].reshape(MLA_Q_RANK, HEADS, MLA_NOPE + MLA_ROPE).transpose(1, 0, 2)
            hpad = jnp.zeros((HEADS, MLA_Q_RANK, HEAD_DIM - MLA_ROPE), F32)
            wq_cat = jnp.concatenate([wq, hpad], axis=-1)
            wq_swap = jnp.concatenate([_swap_rope_halves(wq[..., MLA_NOPE:]), hpad], axis=-1)
            wkv = b_w_ukv[j].reshape(MLA_KV_RANK, HEADS, 2 * HEAD_DIM).transpose(1, 0, 2)
            cos, sin = _rope_tables()
            q, kv, kpe = _mla_up(z, b_q_norm[j], b_kv_norm[j], cos, sin, wq_cat.astype(BF16), wq_swap.astype(BF16),
                                 wkv.astype(BF16))
            att = _mla_lat_attention(q, kv, kpe)
            att = _mla_ctx_attention(q, kv, kpe, att)
            w_o = b_w_o_b
        else:
            uv = _mod_matmul(xs, norm_rows, 2 * i, mods, 0, c_w_in_b, j, tm=1024, tn=512, out_dtype=F32,
                             bias=c_b_in, act="gelu", name="sgu_in")
            bs_full = jnp.repeat(c_bs[j].T, CHUNK, axis=1)
            att = _sgu(uv, c_ln_g[j], c_ln_b[j], c_ws[j].astype(BF16), bs_full)
            w_o = c_w_o_b

        xs = _mixer_out(att, w_o, j, xs, mods, rows=rows_out)
        xs = _ffn(xs, norm_rows, 2 * i + 1, mods, f_w_in_b, f_conv_w, f_conv_b, f_w_out_b, i, rows=rows_out)

    return _final_norm(xs, final_g).reshape(BATCH, SEQ, D)
```

```python
import functools

import jax
import jax.numpy as jnp
import numpy as np
from jax import lax
from jax.experimental import pallas as pl
from jax.experimental.pallas import tpu as pltpu

F32 = jnp.float32
BF16 = jnp.bfloat16

D = 2048
BATCH = 4
SEQ = 2048
DEPTH = 4
GRID_W = 64
GRID_H = SEQ // GRID_W
CTX_LEN = 256
N_MIXERS = 3
EPS = 1e-6
ROPE_THETA = 10000.0
HEADS = 16
HEAD_DIM = 128
WIN_H = 8
WIN_W = 16
MLA_NOPE = 128
MLA_ROPE = 64
MLA_Q_RANK = 512
MLA_KV_RANK = 512
CHUNK = 128
SG_GROUPS = 16
D_FF = 5632
CONV_W = 3

T_LAT = BATCH * SEQ
T_CTX = BATCH * CTX_LEN
T_ALL = T_LAT + T_CTX
CTX_ROW_BLOCK0 = T_LAT // CTX_LEN
MASK_VALUE = -1e30
VMEM_LIMIT = 56 * 1024 * 1024
LANES = 128
BF16_ROWS = 16
PROLOGUE_ROWS = 256
PROJ_TM = 1024
PROJ_TN = 512


def _params(*semantics):
    return pltpu.CompilerParams(dimension_semantics=semantics, vmem_limit_bytes=VMEM_LIMIT)


def _column_tiles(w, tn):
    layers, k, n = w.shape
    return w.reshape(layers, k, n // tn, tn).transpose(0, 2, 1, 3)


def _mod_row(tm):
    per_batch = SEQ // tm
    return lambda m: jnp.minimum(m // per_batch, BATCH)


def _resident(block_shape, index_map):
    return pl.BlockSpec(block_shape, index_map, pipeline_mode=pl.Buffered(1))


def _ada_kernel(s_ref, w_ref, b_ref, o_ref):
    s = s_ref[...]
    s = s * jax.nn.sigmoid(s)
    o_ref[...] = jnp.dot(s.astype(BF16), w_ref[...].astype(BF16), preferred_element_type=F32) + b_ref[...]


def _ada(s, ada_w, ada_b):
    tn = 1024
    n6 = 6 * D
    return pl.pallas_call(
        _ada_kernel,
        grid=(DEPTH, n6 // tn),
        in_specs=[
            pl.BlockSpec((8, D), lambda i, n: (0, 0)),
            pl.BlockSpec((None, D, tn), lambda i, n: (i, 0, n)),
            pl.BlockSpec((None, 1, tn), lambda i, n: (i, 0, n)),
        ],
        out_specs=pl.BlockSpec((None, 8, tn), lambda i, n: (i, 0, n)),
        out_shape=jax.ShapeDtypeStruct((DEPTH, 8, n6), F32),
        compiler_params=_params("parallel", "parallel"),
        name="ada",
    )(s, ada_w, ada_b.reshape(DEPTH, 1, n6))


def _gelu_exact(z):
    return 0.5 * z * (1.0 + lax.erf(z * np.float32(np.sqrt(0.5))))


def _modulate(xf, g, shift, scale):
    y = xf * lax.rsqrt(jnp.mean(xf * xf, axis=-1, keepdims=True) + EPS)
    return (y * g) * (1.0 + scale) + shift


def _modulate_rows(x_ref, h_ref, h_row0, n_rows, g, shift, scale):
    def body(c, carry):
        r = pl.multiple_of(c * PROLOGUE_ROWS, PROLOGUE_ROWS)
        h = _modulate(x_ref[pl.ds(r, PROLOGUE_ROWS), :], g, shift, scale)
        h_ref[pl.ds(h_row0 + r, PROLOGUE_ROWS), :] = h.astype(BF16)
        return carry
    lax.fori_loop(0, n_rows // PROLOGUE_ROWS, body, 0)


def _mod_mm_kernel(x_ref, g_ref, sh_ref, sc_ref, w_ref, *rest, has_bias, act, tm):
    if has_bias:
        b_ref, o_ref, h_ref = rest
    else:
        o_ref, h_ref = rest

    @pl.when(pl.program_id(1) == 0)
    def _():
        _modulate_rows(x_ref, h_ref, 0, tm, g_ref[...], sh_ref[...], sc_ref[...])

    acc = jnp.dot(h_ref[...], w_ref[...], preferred_element_type=F32)
    if has_bias:
        acc = acc + b_ref[...]
    if act == "gelu":
        acc = _gelu_exact(acc)
    o_ref[...] = acc.astype(o_ref.dtype)


def _mod_matmul(x, norm_g, g_row, mods, shift_col, w, w_layer, *, tm, out_dtype, bias=None, act=None, name):
    tn = w.shape[3]
    n = w.shape[1] * tn
    brow = _mod_row(tm)
    in_specs = [
        pl.BlockSpec((tm, D), lambda m, j: (m, 0)),
        pl.BlockSpec((None, 1, D), lambda m, j: (g_row, 0, 0)),
        pl.BlockSpec((None, 1, D), lambda m, j: (brow(m), 0, shift_col)),
        pl.BlockSpec((None, 1, D), lambda m, j: (brow(m), 0, shift_col + 1)),
        pl.BlockSpec((None, None, D, tn), lambda m, j: (w_layer, j, 0, 0)),
    ]
    args = [x, norm_g, mods, mods, w]
    if bias is not None:
        in_specs.append(pl.BlockSpec((None, 1, tn), lambda m, j: (w_layer, 0, j)))
        args.append(bias.reshape(bias.shape[0], 1, n))
    return pl.pallas_call(
        functools.partial(_mod_mm_kernel, has_bias=bias is not None, act=act, tm=tm),
        grid=(T_ALL // tm, n // tn),
        in_specs=in_specs,
        out_specs=pl.BlockSpec((tm, tn), lambda m, j: (m, j)),
        out_shape=jax.ShapeDtypeStruct((T_ALL, n), out_dtype),
        scratch_shapes=[pltpu.VMEM((tm, D), BF16)],
        compiler_params=_params("parallel", "arbitrary"),
        name=name,
    )(*args)


MIXER_OUT_COLS = 512


MIXER_OUT_TM = 512
MIXER_LAT_TILES = T_LAT // MIXER_OUT_TM


def _mixer_out_kernel(*refs, has_ctx):
    if has_ctx:
        a_ref, ac_ref, w_ref, r_ref, gate_ref, o_ref = refs
        a = jnp.where(pl.program_id(0) < MIXER_LAT_TILES, a_ref[...], ac_ref[...])
    else:
        a_ref, w_ref, r_ref, gate_ref, o_ref = refs
        a = a_ref[...]
    for c in range(D // MIXER_OUT_COLS):
        cols = slice(c * MIXER_OUT_COLS, (c + 1) * MIXER_OUT_COLS)
        acc = jnp.dot(a, w_ref[:, cols], preferred_element_type=F32)
        o_ref[:, cols] = r_ref[:, cols] + gate_ref[:, cols] * acc


def _mixer_out(a_lat, a_ctx, w, w_layer, resid, mods):
    tm = MIXER_OUT_TM
    brow = _mod_row(tm)
    has_ctx = a_ctx is not None
    rows = T_ALL if has_ctx else T_LAT
    in_specs = [pl.BlockSpec((tm, D), lambda m: (jnp.minimum(m, MIXER_LAT_TILES - 1), 0))]
    args = [a_lat]
    if has_ctx:
        ctx_tile0 = (a_ctx.shape[0] - T_CTX) // tm
        in_specs.append(pl.BlockSpec((tm, D), lambda m: (jnp.maximum(m - MIXER_LAT_TILES, 0) + ctx_tile0, 0)))
        args.append(a_ctx)
    in_specs += [
        _resident((None, D, D), lambda m: (w_layer, 0, 0)),
        pl.BlockSpec((tm, D), lambda m: (m, 0)),
        pl.BlockSpec((None, 1, D), lambda m: (brow(m), 0, 2)),
    ]
    return pl.pallas_call(
        functools.partial(_mixer_out_kernel, has_ctx=has_ctx),
        grid=(rows // tm,),
        in_specs=in_specs,
        out_specs=pl.BlockSpec((tm, D), lambda m: (m, 0)),
        out_shape=jax.ShapeDtypeStruct((rows, D), F32),
        compiler_params=_params("parallel"),
        name="mixer_out",
    )(*args, w, resid, mods)


FFN_TM = 1024
FFN_TF = 512
FFN_HALO = BF16_ROWS
FFN_OUT_COLS = 512
FFN_ROW_CHUNK = CTX_LEN
FFN_LOOKAHEAD = 2
FFN_ROW_SETS = 4
FFN_SLABS = FFN_TF // LANES


def _ffn_kernel(x_ref, xp_ref, xn_ref, g_ref, sh_ref, sc_ref, gate_ref, wa_ref, wg_ref, cwa_ref, cwg_ref,
                cba_ref, cbg_ref, wo_ref, o_ref, h_ref, ua_ref, ug_ref, act_ref):
    tm, halo = FFN_TM, FFN_HALO
    f = pl.program_id(1)

    @pl.when(f == 0)
    def _():
        g, sh, sc = g_ref[...], sh_ref[...], sc_ref[...]
        h_ref[:halo, :] = _modulate(xp_ref[...], g, sh, sc).astype(BF16)
        h_ref[halo + tm:, :] = _modulate(xn_ref[...], g, sh, sc).astype(BF16)
        _modulate_rows(x_ref, h_ref, halo, tm, g, sh, sc)
        o_ref[...] = jnp.zeros_like(o_ref)

    ch, n_ch, sets = FFN_ROW_CHUNK, tm // FFN_ROW_CHUNK, FFN_ROW_SETS
    set_rows = ch // sets
    sub = lax.broadcasted_iota(jnp.int32, (set_rows, LANES), 0)

    def hidden(r):
        lo = 0 if r == 0 else halo + r * ch
        hi = tm + 2 * halo if r == n_ch - 1 else halo + (r + 1) * ch
        h = h_ref[lo:hi, :]
        ua = jnp.dot(h, wa_ref[...], preferred_element_type=F32)
        ug = jnp.dot(h, wg_ref[...], preferred_element_type=F32)
        for s in range(FFN_SLABS):
            ua_ref[s, lo:hi, :] = ua[:, s * LANES:(s + 1) * LANES]
            ug_ref[s, lo:hi, :] = ug[:, s * LANES:(s + 1) * LANES]

    def gate_and_project(r):
        base = halo + r * ch
        rows = slice(r * ch, (r + 1) * ch)
        row0 = pl.program_id(0) * tm + r * ch
        seq_mask = jnp.where(row0 < T_LAT, SEQ - 1, CTX_LEN - 1)
        keep_first_up = jnp.where((row0 & seq_mask) == 0, 0.0, 1.0).astype(F32)
        keep_last_dn = jnp.where(((row0 + ch) & seq_mask) == 0, 0.0, 1.0).astype(F32)
        first_up = jnp.where(sub == 0, keep_first_up, 1.0)
        last_dn = jnp.where(sub == set_rows - 1, keep_last_dn, 1.0)

        for s in range(FFN_SLABS):
            lanes = slice(s * LANES, (s + 1) * LANES)
            for j in range(sets):
                def conv(u_ref, cw_ref, cb_ref):
                    up = u_ref[s, pl.ds(base + j - 1, set_rows, stride=sets), :]
                    cur = u_ref[s, pl.ds(base + j, set_rows, stride=sets), :]
                    dn = u_ref[s, pl.ds(base + j + 1, set_rows, stride=sets), :]
                    if j == 0:
                        up = up * first_up
                    if j == sets - 1:
                        dn = dn * last_dn
                    return (up * cw_ref[0:1, lanes] + cur * cw_ref[1:2, lanes] + dn * cw_ref[2:3, lanes]
                            + cb_ref[:, lanes])

                a = conv(ua_ref, cwa_ref, cba_ref)
                gg = conv(ug_ref, cwg_ref, cbg_ref)
                act_ref[s, pl.ds(r * ch + j, set_rows, stride=sets), :] = a * (gg * jax.nn.sigmoid(gg))

        act = jnp.concatenate([act_ref[s, rows, :] for s in range(FFN_SLABS)], axis=1).astype(BF16)
        for c in range(D // FFN_OUT_COLS):
            cols = slice(c * FFN_OUT_COLS, (c + 1) * FFN_OUT_COLS)
            o_ref[rows, cols] += jnp.dot(act, wo_ref[:, cols], preferred_element_type=F32)

    for r in range(min(FFN_LOOKAHEAD, n_ch)):
        hidden(r)
    for r in range(n_ch):
        if r + FFN_LOOKAHEAD < n_ch:
            hidden(r + FFN_LOOKAHEAD)
        gate_and_project(r)

    @pl.when(f == pl.num_programs(1) - 1)
    def _():
        gate = gate_ref[...]

        def body(c, carry):
            rows = pl.ds(pl.multiple_of(c * PROLOGUE_ROWS, PROLOGUE_ROWS), PROLOGUE_ROWS)
            o_ref[rows, :] = x_ref[rows, :] + gate * o_ref[rows, :]
            return carry
        lax.fori_loop(0, tm // PROLOGUE_ROWS, body, 0)


def _ffn(x, norm_g, g_row, mods, w_in, conv_w, conv_b, w_out, layer, *, rows):
    tm, tf, halo = FFN_TM, FFN_TF, FFN_HALO
    nf = D_FF // tf
    brow = _mod_row(tm)
    last_halo = rows // halo - 1
    cb = conv_b.reshape(DEPTH, 1, 2 * D_FF)
    return pl.pallas_call(
        _ffn_kernel,
        grid=(rows // tm, nf),
        in_specs=[
            pl.BlockSpec((tm, D), lambda m, f: (m, 0), pipeline_mode=pl.Buffered(1)),
            pl.BlockSpec((halo, D), lambda m, f: (jnp.maximum(m * (tm // halo) - 1, 0), 0)),
            pl.BlockSpec((halo, D), lambda m, f: (jnp.minimum((m + 1) * (tm // halo), last_halo), 0)),
            pl.BlockSpec((None, 1, D), lambda m, f: (g_row, 0, 0)),
            pl.BlockSpec((None, 1, D), lambda m, f: (brow(m), 0, 3)),
            pl.BlockSpec((None, 1, D), lambda m, f: (brow(m), 0, 4)),
            pl.BlockSpec((None, 1, D), lambda m, f: (brow(m), 0, 5)),
            pl.BlockSpec((None, None, D, tf), lambda m, f: (layer, f, 0, 0)),
            pl.BlockSpec((None, None, D, tf), lambda m, f: (layer, f + nf, 0, 0)),
            pl.BlockSpec((None, CONV_W, tf), lambda m, f: (layer, 0, f)),
            pl.BlockSpec((None, CONV_W, tf), lambda m, f: (layer, 0, f + nf)),
            pl.BlockSpec((None, 1, tf), lambda m, f: (layer, 0, f)),
            pl.BlockSpec((None, 1, tf), lambda m, f: (layer, 0, f + nf)),
            pl.BlockSpec((None, tf, D), lambda m, f: (layer, f, 0)),
        ],
        out_specs=pl.BlockSpec((tm, D), lambda m, f: (m, 0)),
        out_shape=jax.ShapeDtypeStruct((rows, D), F32),
        scratch_shapes=[
            pltpu.VMEM((tm + 2 * halo, D), BF16),
            pltpu.VMEM((FFN_SLABS, tm + 2 * halo, LANES), F32),
            pltpu.VMEM((FFN_SLABS, tm + 2 * halo, LANES), F32),
            pltpu.VMEM((FFN_SLABS, tm, LANES), F32),
        ],
        compiler_params=_params("parallel", "arbitrary"),
        name="ffn",
    )(x, x, x, norm_g, mods, mods, mods, w_in, w_in, conv_w, conv_w, cb, cb, w_out)


NA_QROWS = 4
NA_QBLOCK = NA_QROWS * GRID_W
NA_SUB = 4
NA_BAND_ROWS = 12
NA_BAND = NA_BAND_ROWS * GRID_W


def _nt_dot(a, b):
    return lax.dot_general(a, b, (((1,), (1,)), ((), ())), preferred_element_type=F32)


def _na_lat_kernel(q_ref, k_ref, v_ref, kc_ref, vc_ref, bias_ref, o_ref):
    i = pl.program_id(2)
    scale = np.float32(HEAD_DIM ** -0.5)
    kc = kc_ref[...]
    vc = vc_ref[...]
    left = lax.broadcasted_iota(jnp.int32, (1, 2 * GRID_W), 1) < GRID_W

    for sb in range(NA_SUB):
        r0 = (i * NA_SUB + sb) * NA_QROWS
        u0 = jnp.clip(r0 - WIN_H // 2, 0, GRID_H - NA_BAND_ROWS)
        key0 = pl.multiple_of(u0 * GRID_W, NA_QROWS * GRID_W)
        q = q_ref[sb * NA_QBLOCK:(sb + 1) * NA_QBLOCK, :]
        kb = k_ref[pl.ds(key0, NA_BAND), :]
        vb = v_ref[pl.ds(key0, NA_BAND), :]
        bias_rows = []
        for j in range(NA_QROWS):
            r = r0 + j
            rs = jnp.clip(r - WIN_H // 2, 0, GRID_H - WIN_H)
            tiles = []
            for p in range(NA_BAND_ROWS // 2):
                kr = u0 + 2 * p
                pen_l = jnp.where((kr >= rs) & (kr < rs + WIN_H), 0.0, MASK_VALUE).astype(F32)
                pen_r = jnp.where((kr + 1 >= rs) & (kr + 1 < rs + WIN_H), 0.0, MASK_VALUE).astype(F32)
                pair = jnp.clip(kr - r + WIN_H, 0, 2 * WIN_H - 1)
                tiles.append(bias_ref[pair] + jnp.where(left, pen_l, pen_r))
            bias_rows.append(jnp.concatenate(tiles, axis=1))
        s = _nt_dot(q, kb) * scale + jnp.concatenate(bias_rows, axis=0)
        sc = _nt_dot(q, kc) * scale
        mx = jnp.maximum(jnp.max(s, axis=-1, keepdims=True), jnp.max(sc, axis=-1, keepdims=True))
        e = jnp.exp(s - mx)
        ec = jnp.exp(sc - mx)
        denom = jnp.sum(e, axis=-1, keepdims=True) + jnp.sum(ec, axis=-1, keepdims=True)
        o = (jnp.dot(e.astype(BF16), vb, preferred_element_type=F32)
             + jnp.dot(ec.astype(BF16), vc, preferred_element_type=F32))
        o_ref[sb * NA_QBLOCK:(sb + 1) * NA_QBLOCK, :] = (o / denom).astype(o_ref.dtype)


def _na_bias_table(rpb):
    w = np.arange(GRID_W)
    col0 = np.clip(w - WIN_W // 2, 0, GRID_W - WIN_W)
    kcol = np.arange(GRID_W)
    in_win = (kcol[None, :] >= col0[:, None]) & (kcol[None, :] < col0[:, None] + WIN_W)
    pad = GRID_W - WIN_W
    padded = jnp.pad(rpb, ((0, 0), (0, 0), (pad, pad)))
    base = pad + WIN_W - 1
    t = jnp.stack([padded[:, :, base - wq:base - wq + GRID_W] for wq in range(GRID_W)], axis=2)
    t = jnp.where(jnp.asarray(in_win), t, MASK_VALUE)
    zero = jnp.zeros_like(t[:, :1])
    t = jnp.concatenate([zero, t, zero], axis=1)
    return jnp.concatenate([t[:, :-1], t[:, 1:]], axis=-1)


def _na_lat_attention(qkv, bias_tab):
    blocks = SEQ // (NA_SUB * NA_QBLOCK)
    qb = NA_SUB * NA_QBLOCK
    return pl.pallas_call(
        _na_lat_kernel,
        grid=(BATCH, HEADS, blocks),
        in_specs=[
            pl.BlockSpec((qb, HEAD_DIM), lambda b, h, i: (b * blocks + i, h)),
            pl.BlockSpec((SEQ, HEAD_DIM), lambda b, h, i: (b, HEADS + h)),
            pl.BlockSpec((SEQ, HEAD_DIM), lambda b, h, i: (b, 2 * HEADS + h)),
            pl.BlockSpec((CTX_LEN, HEAD_DIM), lambda b, h, i: (CTX_ROW_BLOCK0 + b, HEADS + h)),
            pl.BlockSpec((CTX_LEN, HEAD_DIM), lambda b, h, i: (CTX_ROW_BLOCK0 + b, 2 * HEADS + h)),
            pl.BlockSpec((None, 2 * WIN_H, GRID_W, 2 * GRID_W), lambda b, h, i: (h, 0, 0, 0)),
        ],
        out_specs=pl.BlockSpec((qb, HEAD_DIM), lambda b, h, i: (b * blocks + i, h)),
        out_shape=jax.ShapeDtypeStruct((T_LAT, D), BF16),
        compiler_params=_params("parallel", "parallel", "arbitrary"),
        name="na_lat",
    )(qkv, qkv, qkv, qkv, qkv, bias_tab)


def _softmax_pv(s, v):
    e = jnp.exp(s - jnp.max(s, axis=-1, keepdims=True))
    o = jnp.dot(e.astype(BF16), v, preferred_element_type=F32)
    return o / jnp.sum(e, axis=-1, keepdims=True)


def _na_ctx_kernel(q_ref, k_ref, v_ref, o_ref):
    scale = np.float32(HEAD_DIM ** -0.5)
    for h in range(HEADS):
        cols = slice(h * HEAD_DIM, (h + 1) * HEAD_DIM)
        s = _nt_dot(q_ref[:, cols], k_ref[:, cols]) * scale
        o_ref[:, cols] = _softmax_pv(s, v_ref[:, cols]).astype(o_ref.dtype)


def _na_ctx_attention(qkv):
    return pl.pallas_call(
        _na_ctx_kernel,
        grid=(BATCH,),
        in_specs=[
            pl.BlockSpec((CTX_LEN, D), lambda b: (CTX_ROW_BLOCK0 + b, 0)),
            pl.BlockSpec((CTX_LEN, D), lambda b: (CTX_ROW_BLOCK0 + b, 1)),
            pl.BlockSpec((CTX_LEN, D), lambda b: (CTX_ROW_BLOCK0 + b, 2)),
        ],
        out_specs=pl.BlockSpec((CTX_LEN, D), lambda b: (b, 0)),
        out_shape=jax.ShapeDtypeStruct((T_CTX, D), BF16),
        compiler_params=_params("parallel"),
        name="na_ctx",
    )(qkv, qkv, qkv)


MLA_QK = 2 * HEAD_DIM
MLA_QBLOCK = 256
MLA_SUB = 4
MLA_ZCOLS = MLA_Q_RANK + MLA_KV_RANK + 2 * HEAD_DIM
MLA_UP_TM = 512


def _rms(z, g):
    return (z * lax.rsqrt(jnp.mean(z * z, axis=-1, keepdims=True) + EPS)) * g


def _mla_up_kernel(z_ref, qn_ref, kvn_ref, cos_ref, sin_ref, wq_ref, wqs_ref, wkv_ref, q_ref, kv_ref, kpe_ref):
    cq = _rms(z_ref[:, :MLA_Q_RANK], qn_ref[...]).astype(BF16)
    ckv = _rms(z_ref[:, MLA_Q_RANK:MLA_Q_RANK + MLA_KV_RANK], kvn_ref[...]).astype(BF16)
    c0 = MLA_Q_RANK + MLA_KV_RANK
    cos = cos_ref[...]
    sin = sin_ref[...]
    kpe_ref[...] = (z_ref[:, c0:c0 + HEAD_DIM] * cos + z_ref[:, c0 + HEAD_DIM:] * sin).astype(BF16)
    for h in range(HEADS):
        q = jnp.dot(cq, wq_ref[h], preferred_element_type=F32)
        qs = jnp.dot(cq, wqs_ref[h], preferred_element_type=F32)
        q_ref[h, :, :HEAD_DIM] = q[:, :HEAD_DIM].astype(BF16)
        q_ref[h, :, HEAD_DIM:] = (q[:, HEAD_DIM:] * cos + qs * sin).astype(BF16)
        kv_ref[h] = jnp.dot(ckv, wkv_ref[h], preferred_element_type=F32).astype(BF16)


def _mla_up(z, q_norm, kv_norm, cos, sin, wq, wqs, wkv):
    tm = MLA_UP_TM
    return pl.pallas_call(
        _mla_up_kernel,
        grid=(T_ALL // tm,),
        in_specs=[
            pl.BlockSpec((tm, MLA_ZCOLS), lambda m: (m, 0)),
            pl.BlockSpec((1, MLA_Q_RANK), lambda m: (0, 0)),
            pl.BlockSpec((1, MLA_KV_RANK), lambda m: (0, 0)),
            pl.BlockSpec((tm, HEAD_DIM), lambda m: (m, 0)),
            pl.BlockSpec((tm, HEAD_DIM), lambda m: (m, 0)),
            _resident((HEADS, MLA_Q_RANK, MLA_QK), lambda m: (0, 0, 0)),
            _resident((HEADS, MLA_Q_RANK, HEAD_DIM), lambda m: (0, 0, 0)),
            _resident((HEADS, MLA_KV_RANK, 2 * HEAD_DIM), lambda m: (0, 0, 0)),
        ],
        out_specs=[
            pl.BlockSpec((HEADS, tm, MLA_QK), lambda m: (0, m, 0)),
            pl.BlockSpec((HEADS, tm, 2 * HEAD_DIM), lambda m: (0, m, 0)),
            pl.BlockSpec((tm, HEAD_DIM), lambda m: (m, 0)),
        ],
        out_shape=[
            jax.ShapeDtypeStruct((HEADS, T_ALL, MLA_QK), BF16),
            jax.ShapeDtypeStruct((HEADS, T_ALL, 2 * HEAD_DIM), BF16),
            jax.ShapeDtypeStruct((T_ALL, HEAD_DIM), BF16),
        ],
        compiler_params=_params("parallel"),
        name="mla_up",
    )(z, q_norm.reshape(1, -1), kv_norm.reshape(1, -1), cos, sin, wq, wqs, wkv)


MLA_SCALE = np.float32((MLA_NOPE + MLA_ROPE) ** -0.5)
MLA_SCALE_LOG2E = np.float32((MLA_NOPE + MLA_ROPE) ** -0.5 * np.log2(np.e))


def _mla_lat_kernel(q_ref, kn_ref, v_ref, kpe_ref, knc_ref, vc_ref, kpec_ref, o_ref, kcat_ref):
    @pl.when(pl.program_id(2) == 0)
    def _():
        kcat_ref[:SEQ, :HEAD_DIM] = kn_ref[...]
        kcat_ref[:SEQ, HEAD_DIM:] = kpe_ref[...]
        kcat_ref[SEQ:, :HEAD_DIM] = knc_ref[...]
        kcat_ref[SEQ:, HEAD_DIM:] = kpec_ref[...]

    for sb in range(MLA_SUB):
        rows = slice(sb * MLA_QBLOCK, (sb + 1) * MLA_QBLOCK)
        s = _nt_dot(q_ref[rows, :], kcat_ref[...])
        e = jnp.exp2((s - jnp.max(s, axis=-1, keepdims=True)) * MLA_SCALE_LOG2E)
        o = (jnp.dot(e[:, :SEQ].astype(BF16), v_ref[...], preferred_element_type=F32)
             + jnp.dot(e[:, SEQ:].astype(BF16), vc_ref[...], preferred_element_type=F32))
        o_ref[rows, :] = (o / jnp.sum(e, axis=-1, keepdims=True)).astype(o_ref.dtype)


def _mla_lat_attention(q, kv, kpe):
    qb = MLA_SUB * MLA_QBLOCK
    blocks = SEQ // qb
    ctx_q = CTX_ROW_BLOCK0
    return pl.pallas_call(
        _mla_lat_kernel,
        grid=(BATCH, HEADS, blocks),
        in_specs=[
            pl.BlockSpec((None, qb, MLA_QK), lambda b, h, i: (h, b * blocks + i, 0)),
            pl.BlockSpec((None, SEQ, HEAD_DIM), lambda b, h, i: (h, b, 0)),
            pl.BlockSpec((None, SEQ, HEAD_DIM), lambda b, h, i: (h, b, 1)),
            pl.BlockSpec((SEQ, HEAD_DIM), lambda b, h, i: (b, 0)),
            pl.BlockSpec((None, CTX_LEN, HEAD_DIM), lambda b, h, i: (h, ctx_q + b, 0)),
            pl.BlockSpec((None, CTX_LEN, HEAD_DIM), lambda b, h, i: (h, ctx_q + b, 1)),
            pl.BlockSpec((CTX_LEN, HEAD_DIM), lambda b, h, i: (ctx_q + b, 0)),
        ],
        out_specs=pl.BlockSpec((qb, HEAD_DIM), lambda b, h, i: (b * blocks + i, h)),
        out_shape=jax.ShapeDtypeStruct((T_LAT, D), BF16),
        scratch_shapes=[pltpu.VMEM((SEQ + CTX_LEN, MLA_QK), BF16)],
        compiler_params=_params("parallel", "parallel", "arbitrary"),
        name="mla_lat",
    )(q, kv, kv, kpe, kv, kv, kpe)


def _mla_ctx_kernel(q_ref, kv_ref, kpe_ref, o_ref):
    kpe = kpe_ref[...]
    for h in range(HEADS):
        kcat = jnp.concatenate([kv_ref[h, :, :HEAD_DIM], kpe], axis=1)
        s = _nt_dot(q_ref[h], kcat) * MLA_SCALE
        o_ref[:, h * HEAD_DIM:(h + 1) * HEAD_DIM] = _softmax_pv(s, kv_ref[h, :, HEAD_DIM:]).astype(o_ref.dtype)


def _mla_ctx_attention(q, kv, kpe):
    ctx_q = CTX_ROW_BLOCK0
    return pl.pallas_call(
        _mla_ctx_kernel,
        grid=(BATCH,),
        in_specs=[
            pl.BlockSpec((HEADS, CTX_LEN, MLA_QK), lambda b: (0, ctx_q + b, 0)),
            pl.BlockSpec((HEADS, CTX_LEN, 2 * HEAD_DIM), lambda b: (0, ctx_q + b, 0)),
            pl.BlockSpec((CTX_LEN, HEAD_DIM), lambda b: (ctx_q + b, 0)),
        ],
        out_specs=pl.BlockSpec((CTX_LEN, D), lambda b: (b, 0)),
        out_shape=jax.ShapeDtypeStruct((T_CTX, D), BF16),
        compiler_params=_params("parallel"),
        name="mla_ctx",
    )(q, kv, kpe)


def _rope_tables():
    half = MLA_ROPE // 2
    freqs = ROPE_THETA ** (-jnp.arange(0, half, 2, dtype=F32) / half)
    t = jnp.arange(SEQ)
    rows = (t // GRID_W).astype(F32)[:, None] * freqs
    cols = (t % GRID_W).astype(F32)[:, None] * freqs
    cr, sr, cc, sn = jnp.cos(rows), jnp.sin(rows), jnp.cos(cols), jnp.sin(cols)
    pad = jnp.zeros((SEQ, HEAD_DIM - MLA_ROPE), F32)
    cos = jnp.concatenate([cr, cr, cc, cc, pad], axis=1)
    sin = jnp.concatenate([-sr, sr, -sn, sn, pad], axis=1)
    cos_c = jnp.concatenate([jnp.ones((T_CTX, MLA_ROPE), F32), jnp.zeros((T_CTX, HEAD_DIM - MLA_ROPE), F32)], axis=1)
    return (jnp.concatenate([jnp.tile(cos, (BATCH, 1)), cos_c], axis=0),
            jnp.concatenate([jnp.tile(sin, (BATCH, 1)), jnp.zeros((T_CTX, HEAD_DIM), F32)], axis=0))


def _swap_rope_halves(w):
    q = MLA_ROPE // 4
    return jnp.concatenate([w[..., q:2 * q], w[..., :q], w[..., 3 * q:], w[..., 2 * q:3 * q]], axis=-1)


def _sgu_kernel(u_ref, v_ref, g_ref, b_ref, ws_ref, bs_ref, o_ref):
    v = v_ref[...]
    mu = jnp.mean(v, axis=-1, keepdims=True)
    vc = v - mu
    var = jnp.mean(vc * vc, axis=-1, keepdims=True)
    vn = ((vc * lax.rsqrt(var + EPS)) * g_ref[...] + b_ref[...]).astype(BF16)
    for g in range(SG_GROUPS):
        cols = slice(g * CHUNK, (g + 1) * CHUNK)
        mix = jnp.dot(ws_ref[g], vn[:, cols], preferred_element_type=F32) + bs_ref[:, cols]
        o_ref[:, cols] = (u_ref[:, cols] * mix).astype(o_ref.dtype)


def _sgu(uv, ln_g, ln_b, ws, bs_full):
    rows = uv.shape[0]
    return pl.pallas_call(
        _sgu_kernel,
        grid=(rows // CHUNK,),
        in_specs=[
            pl.BlockSpec((CHUNK, D), lambda m: (m, 0)),
            pl.BlockSpec((CHUNK, D), lambda m: (m, 1)),
            pl.BlockSpec((1, D), lambda m: (0, 0)),
            pl.BlockSpec((1, D), lambda m: (0, 0)),
            pl.BlockSpec((SG_GROUPS, CHUNK, CHUNK), lambda m: (0, 0, 0)),
            pl.BlockSpec((CHUNK, D), lambda m: (0, 0)),
        ],
        out_specs=pl.BlockSpec((CHUNK, D), lambda m: (m, 0)),
        out_shape=jax.ShapeDtypeStruct((rows, D), BF16),
        compiler_params=_params("parallel"),
        name="sgu",
    )(uv, uv, ln_g.reshape(1, D), ln_b.reshape(1, D), ws, bs_full)


def _final_norm_kernel(x_ref, g_ref, o_ref):
    xf = x_ref[...]
    o_ref[...] = (xf * lax.rsqrt(jnp.mean(xf * xf, axis=-1, keepdims=True) + EPS)) * g_ref[...]


def _final_norm(x, g):
    tm = 512
    return pl.pallas_call(
        _final_norm_kernel,
        grid=(T_LAT // tm,),
        in_specs=[pl.BlockSpec((tm, D), lambda m: (m, 0)), pl.BlockSpec((1, D), lambda m: (0, 0))],
        out_specs=pl.BlockSpec((tm, D), lambda m: (m, 0)),
        out_shape=jax.ShapeDtypeStruct((T_LAT, D), F32),
        compiler_params=_params("parallel"),
        name="final_norm",
    )(x, g.reshape(1, D))


def kernel(x, c, ctx, c_ctx, ada_w, ada_b, norm_g, final_g, a_w_qkv, a_w_o, a_rpb, b_w_in, b_q_norm, b_kv_norm,
           b_w_uq, b_w_ukv, b_w_o, c_w_in, c_b_in, c_ln_g, c_ln_b, c_ws, c_bs, c_w_o, f_w_in, f_conv_w, f_conv_b,
           f_w_out):
    xs = jnp.concatenate([x.reshape(T_LAT, D), ctx.reshape(T_CTX, D)], axis=0)
    cond = jnp.concatenate([c, c_ctx[None, :], jnp.zeros((8 - BATCH - 1, D), F32)], axis=0)
    mods_all = _ada(cond, ada_w, ada_b).reshape(DEPTH, 8, 1, 6 * D)
    norm_rows = norm_g.reshape(DEPTH * 2, 1, D)

    a_w_qkv_b, a_w_o_b = _column_tiles(a_w_qkv.astype(BF16), PROJ_TN), a_w_o.astype(BF16)
    b_w_o_b, c_w_in_b, c_w_o_b = b_w_o.astype(BF16), _column_tiles(c_w_in.astype(BF16), PROJ_TN), c_w_o.astype(BF16)
    f_w_in_b, f_w_out_b = _column_tiles(f_w_in.astype(BF16), FFN_TF), f_w_out.astype(BF16)

    for i in range(DEPTH):
        kind, j = i % N_MIXERS, i // N_MIXERS
        last = i == DEPTH - 1
        mods = mods_all[i]
        rows_out = T_LAT if last else T_ALL

        if kind == 0:
            qkv = _mod_matmul(xs, norm_rows, 2 * i, mods, 0, a_w_qkv_b, j, tm=PROJ_TM, out_dtype=BF16, name="na_qkv")
            att = _na_lat_attention(qkv, _na_bias_table(a_rpb[j]))
            att_ctx = None if last else _na_ctx_attention(qkv)
            w_o = a_w_o_b
        elif kind == 1:
            w_in = b_w_in[j]
            c0 = MLA_Q_RANK + MLA_KV_RANK
            zpad = jnp.zeros((D, HEAD_DIM - MLA_ROPE), F32)
            w_in_ext = jnp.concatenate([w_in, zpad, _swap_rope_halves(w_in[:, c0:]), zpad], axis=1)
            z = _mod_matmul(xs, norm_rows, 2 * i, mods, 0, _column_tiles(w_in_ext.astype(BF16)[None], MLA_ZCOLS // 2),
                            0, tm=PROJ_TM, out_dtype=F32, name="mla_in")
            wq = b_w_uq[j].reshape(MLA_Q_RANK, HEADS, MLA_NOPE + MLA_ROPE).transpose(1, 0, 2)
            hpad = jnp.zeros((HEADS, MLA_Q_RANK, HEAD_DIM - MLA_ROPE), F32)
            wq_cat = jnp.concatenate([wq, hpad], axis=-1)
            wq_swap = jnp.concatenate([_swap_rope_halves(wq[..., MLA_NOPE:]), hpad], axis=-1)
            wkv = b_w_ukv[j].reshape(MLA_KV_RANK, HEADS, 2 * HEAD_DIM).transpose(1, 0, 2)
            cos, sin = _rope_tables()
            q, kv, kpe = _mla_up(z, b_q_norm[j], b_kv_norm[j], cos, sin, wq_cat.astype(BF16), wq_swap.astype(BF16),
                                 wkv.astype(BF16))
            att = _mla_lat_attention(q, kv, kpe)
            att_ctx = _mla_ctx_attention(q, kv, kpe)
            w_o = b_w_o_b
        else:
            uv = _mod_matmul(xs, norm_rows, 2 * i, mods, 0, c_w_in_b, j, tm=PROJ_TM, out_dtype=F32, bias=c_b_in,
                             act="gelu", name="sgu_in")
            bs_full = jnp.repeat(c_bs[j].T, CHUNK, axis=1)
            att = _sgu(uv, c_ln_g[j], c_ln_b[j], c_ws[j].astype(BF16), bs_full)
            att_ctx = None if last else att
            w_o = c_w_o_b

        xs = _mixer_out(att, att_ctx, w_o, j, xs, mods)
        xs = _ffn(xs, norm_rows, 2 * i + 1, mods, f_w_in_b, f_conv_w, f_conv_b, f_w_out_b, i, rows=rows_out)

    return _final_norm(xs, final_g).reshape(BATCH, SEQ, D)
```

```python
import functools

import jax
import jax.numpy as jnp
import numpy as np
from jax import lax
from jax.experimental import pallas as pl
from jax.experimental.pallas import tpu as pltpu

F32 = jnp.float32
BF16 = jnp.bfloat16

D = 2048
BATCH = 4
SEQ = 2048
DEPTH = 4
GRID_W = 64
GRID_H = SEQ // GRID_W
CTX_LEN = 256
N_MIXERS = 3
EPS = 1e-6
ROPE_THETA = 10000.0
HEADS = 16
HEAD_DIM = 128
WIN_H = 8
WIN_W = 16
MLA_NOPE = 128
MLA_ROPE = 64
MLA_Q_RANK = 512
MLA_KV_RANK = 512
CHUNK = 128
SG_GROUPS = 16
D_FF = 5632
CONV_W = 3

T_LAT = BATCH * SEQ
T_CTX = BATCH * CTX_LEN
T_ALL = T_LAT + T_CTX
CTX_ROW_BLOCK0 = T_LAT // CTX_LEN
MASK_VALUE = -1e30
VMEM_LIMIT = 56 * 1024 * 1024
LANES = 128
BF16_ROWS = 16
PROLOGUE_ROWS = 128
PROJ_TM = 1024
PROJ_TN = 1024


def _params(*semantics):
    return pltpu.CompilerParams(dimension_semantics=semantics, vmem_limit_bytes=VMEM_LIMIT)


def _mod_row(tm):
    per_batch = SEQ // tm
    return lambda m: jnp.minimum(m // per_batch, BATCH)


def _resident(block_shape, index_map):
    return pl.BlockSpec(block_shape, index_map, pipeline_mode=pl.Buffered(1))


def _ada_kernel(s_ref, w_ref, b_ref, o_ref):
    s = s_ref[...]
    s = s * jax.nn.sigmoid(s)
    o_ref[...] = jnp.dot(s.astype(BF16), w_ref[...].astype(BF16), preferred_element_type=F32) + b_ref[...]


def _ada(s, ada_w, ada_b):
    tn = 1024
    n6 = 6 * D
    return pl.pallas_call(
        _ada_kernel,
        grid=(DEPTH, n6 // tn),
        in_specs=[
            pl.BlockSpec((8, D), lambda i, n: (0, 0)),
            pl.BlockSpec((None, D, tn), lambda i, n: (i, 0, n)),
            pl.BlockSpec((None, 1, tn), lambda i, n: (i, 0, n)),
        ],
        out_specs=pl.BlockSpec((None, 8, tn), lambda i, n: (i, 0, n)),
        out_shape=jax.ShapeDtypeStruct((DEPTH, 8, n6), F32),
        compiler_params=_params("parallel", "parallel"),
        name="ada",
    )(s, ada_w, ada_b.reshape(DEPTH, 1, n6))


def _gelu_exact(z):
    return 0.5 * z * (1.0 + lax.erf(z * np.float32(np.sqrt(0.5))))


def _modulate(xf, g, shift, scale):
    y = xf * lax.rsqrt(jnp.mean(xf * xf, axis=-1, keepdims=True) + EPS)
    return y * (g * (1.0 + scale)) + shift


def _modulate_rows(x_ref, h_ref, h_row0, n_rows, g, shift, scale):
    def body(c, carry):
        r = pl.multiple_of(c * PROLOGUE_ROWS, PROLOGUE_ROWS)
        h = _modulate(x_ref[pl.ds(r, PROLOGUE_ROWS), :], g, shift, scale)
        h_ref[pl.ds(h_row0 + r, PROLOGUE_ROWS), :] = h.astype(BF16)
        return carry
    lax.fori_loop(0, n_rows // PROLOGUE_ROWS, body, 0)


def _mod_mm_kernel(x_ref, g_ref, sh_ref, sc_ref, w_ref, *rest, has_bias, act, tm):
    if has_bias:
        b_ref, o_ref, h_ref = rest
    else:
        o_ref, h_ref = rest

    @pl.when(pl.program_id(1) == 0)
    def _():
        _modulate_rows(x_ref, h_ref, 0, tm, g_ref[...], sh_ref[...], sc_ref[...])

    acc = jnp.dot(h_ref[...], w_ref[...], preferred_element_type=F32)
    if has_bias:
        acc = acc + b_ref[...]
    if act == "gelu":
        acc = _gelu_exact(acc)
    o_ref[...] = acc.astype(o_ref.dtype)


def _mod_matmul(x, norm_g, g_row, mods, shift_col, w, w_layer, *, tm, tn, out_dtype, bias=None, act=None, name):
    n = w.shape[2]
    brow = _mod_row(tm)
    in_specs = [
        pl.BlockSpec((tm, D), lambda m, j: (m, 0)),
        pl.BlockSpec((None, 1, D), lambda m, j: (g_row, 0, 0)),
        pl.BlockSpec((None, 1, D), lambda m, j: (brow(m), 0, shift_col)),
        pl.BlockSpec((None, 1, D), lambda m, j: (brow(m), 0, shift_col + 1)),
        pl.BlockSpec((None, D, tn), lambda m, j: (w_layer, 0, j)),
    ]
    args = [x, norm_g, mods, mods, w]
    if bias is not None:
        in_specs.append(pl.BlockSpec((None, 1, tn), lambda m, j: (w_layer, 0, j)))
        args.append(bias.reshape(bias.shape[0], 1, n))
    return pl.pallas_call(
        functools.partial(_mod_mm_kernel, has_bias=bias is not None, act=act, tm=tm),
        grid=(T_ALL // tm, n // tn),
        in_specs=in_specs,
        out_specs=pl.BlockSpec((tm, tn), lambda m, j: (m, j)),
        out_shape=jax.ShapeDtypeStruct((T_ALL, n), out_dtype),
        scratch_shapes=[pltpu.VMEM((tm, D), BF16)],
        compiler_params=_params("parallel", "arbitrary"),
        name=name,
    )(*args)


MIXER_OUT_COLS = 512


MIXER_OUT_TM = 512
MIXER_LAT_TILES = T_LAT // MIXER_OUT_TM


def _mixer_out_kernel(*refs, has_ctx):
    if has_ctx:
        a_ref, ac_ref, w_ref, r_ref, gate_ref, o_ref = refs
        a = jnp.where(pl.program_id(0) < MIXER_LAT_TILES, a_ref[...], ac_ref[...])
    else:
        a_ref, w_ref, r_ref, gate_ref, o_ref = refs
        a = a_ref[...]
    for c in range(D // MIXER_OUT_COLS):
        cols = slice(c * MIXER_OUT_COLS, (c + 1) * MIXER_OUT_COLS)
        acc = jnp.dot(a, w_ref[:, cols], preferred_element_type=F32)
        o_ref[:, cols] = r_ref[:, cols] + gate_ref[:, cols] * acc


def _mixer_out(a_lat, a_ctx, w, w_layer, resid, mods):
    tm = MIXER_OUT_TM
    brow = _mod_row(tm)
    has_ctx = a_ctx is not None
    rows = T_ALL if has_ctx else T_LAT
    in_specs = [pl.BlockSpec((tm, D), lambda m: (jnp.minimum(m, MIXER_LAT_TILES - 1), 0))]
    args = [a_lat]
    if has_ctx:
        ctx_tile0 = (a_ctx.shape[0] - T_CTX) // tm
        in_specs.append(pl.BlockSpec((tm, D), lambda m: (jnp.maximum(m - MIXER_LAT_TILES, 0) + ctx_tile0, 0)))
        args.append(a_ctx)
    in_specs += [
        _resident((None, D, D), lambda m: (w_layer, 0, 0)),
        pl.BlockSpec((tm, D), lambda m: (m, 0)),
        pl.BlockSpec((None, 1, D), lambda m: (brow(m), 0, 2)),
    ]
    return pl.pallas_call(
        functools.partial(_mixer_out_kernel, has_ctx=has_ctx),
        grid=(rows // tm,),
        in_specs=in_specs,
        out_specs=pl.BlockSpec((tm, D), lambda m: (m, 0)),
        out_shape=jax.ShapeDtypeStruct((rows, D), F32),
        compiler_params=_params("parallel"),
        name="mixer_out",
    )(*args, w, resid, mods)


FFN_TM = 1024
FFN_TF = 512
FFN_HALO = BF16_ROWS
FFN_OUT_COLS = 512
FFN_ROW_CHUNK = CTX_LEN
FFN_LOOKAHEAD = 2
FFN_ROW_SETS = 4
FFN_SLABS = FFN_TF // LANES


def _ffn_kernel(x_ref, xp_ref, xn_ref, g_ref, sh_ref, sc_ref, gate_ref, wa_ref, wg_ref, cwa_ref, cwg_ref,
                cba_ref, cbg_ref, wo_ref, fg_ref, o_ref, h_ref, ua_ref, ug_ref, act_ref, *, final_norm):
    tm, halo = FFN_TM, FFN_HALO
    f = pl.program_id(1)

    @pl.when(f == 0)
    def _():
        g, sh, sc = g_ref[...], sh_ref[...], sc_ref[...]
        h_ref[:halo, :] = _modulate(xp_ref[...], g, sh, sc).astype(BF16)
        h_ref[halo + tm:, :] = _modulate(xn_ref[...], g, sh, sc).astype(BF16)
        _modulate_rows(x_ref, h_ref, halo, tm, g, sh, sc)
        o_ref[...] = jnp.zeros_like(o_ref)

    ch, n_ch, sets = FFN_ROW_CHUNK, tm // FFN_ROW_CHUNK, FFN_ROW_SETS
    set_rows = ch // sets
    sub = lax.broadcasted_iota(jnp.int32, (set_rows, LANES), 0)

    def hidden(r):
        lo = 0 if r == 0 else halo + r * ch
        hi = tm + 2 * halo if r == n_ch - 1 else halo + (r + 1) * ch
        h = h_ref[lo:hi, :]
        ua = jnp.dot(h, wa_ref[...], preferred_element_type=F32)
        ug = jnp.dot(h, wg_ref[...], preferred_element_type=F32)
        for s in range(FFN_SLABS):
            ua_ref[s, lo:hi, :] = ua[:, s * LANES:(s + 1) * LANES]
            ug_ref[s, lo:hi, :] = ug[:, s * LANES:(s + 1) * LANES]

    def gate_and_project(r):
        base = halo + r * ch
        rows = slice(r * ch, (r + 1) * ch)
        row0 = pl.program_id(0) * tm + r * ch
        seq_mask = jnp.where(row0 < T_LAT, SEQ - 1, CTX_LEN - 1)
        keep_first_up = jnp.where((row0 & seq_mask) == 0, 0.0, 1.0).astype(F32)
        keep_last_dn = jnp.where(((row0 + ch) & seq_mask) == 0, 0.0, 1.0).astype(F32)
        first_up = jnp.where(sub == 0, keep_first_up, 1.0)
        last_dn = jnp.where(sub == set_rows - 1, keep_last_dn, 1.0)

        for s in range(FFN_SLABS):
            lanes = slice(s * LANES, (s + 1) * LANES)
            for j in range(sets):
                def conv(u_ref, cw_ref, cb_ref):
                    up = u_ref[s, pl.ds(base + j - 1, set_rows, stride=sets), :]
                    cur = u_ref[s, pl.ds(base + j, set_rows, stride=sets), :]
                    dn = u_ref[s, pl.ds(base + j + 1, set_rows, stride=sets), :]
                    if j == 0:
                        up = up * first_up
                    if j == sets - 1:
                        dn = dn * last_dn
                    return (up * cw_ref[0:1, lanes] + cur * cw_ref[1:2, lanes] + dn * cw_ref[2:3, lanes]
                            + cb_ref[:, lanes])

                a = conv(ua_ref, cwa_ref, cba_ref)
                gg = conv(ug_ref, cwg_ref, cbg_ref)
                act_ref[s, pl.ds(r * ch + j, set_rows, stride=sets), :] = a * (gg * jax.nn.sigmoid(gg))

        act = jnp.concatenate([act_ref[s, rows, :] for s in range(FFN_SLABS)], axis=1).astype(BF16)
        for c in range(D // FFN_OUT_COLS):
            cols = slice(c * FFN_OUT_COLS, (c + 1) * FFN_OUT_COLS)
            o_ref[rows, cols] += jnp.dot(act, wo_ref[:, cols], preferred_element_type=F32)

    for r in range(min(FFN_LOOKAHEAD, n_ch)):
        hidden(r)
    for r in range(n_ch):
        if r + FFN_LOOKAHEAD < n_ch:
            hidden(r + FFN_LOOKAHEAD)
        gate_and_project(r)

    @pl.when(f == pl.num_programs(1) - 1)
    def _():
        gate = gate_ref[...]

        def body(c, carry):
            rows = pl.ds(pl.multiple_of(c * PROLOGUE_ROWS, PROLOGUE_ROWS), PROLOGUE_ROWS)
            y = x_ref[rows, :] + gate * o_ref[rows, :]
            if final_norm:
                y = (y * lax.rsqrt(jnp.mean(y * y, axis=-1, keepdims=True) + EPS)) * fg_ref[...]
            o_ref[rows, :] = y
            return carry
        lax.fori_loop(0, tm // PROLOGUE_ROWS, body, 0)


def _ffn(x, norm_g, g_row, mods, w_in, conv_w, conv_b, w_out, layer, final_g, *, rows, final_norm):
    tm, tf, halo = FFN_TM, FFN_TF, FFN_HALO
    nf = D_FF // tf
    brow = _mod_row(tm)
    last_halo = rows // halo - 1
    cb = conv_b.reshape(DEPTH, 1, 2 * D_FF)
    return pl.pallas_call(
        functools.partial(_ffn_kernel, final_norm=final_norm),
        grid=(rows // tm, nf),
        in_specs=[
            pl.BlockSpec((tm, D), lambda m, f: (m, 0), pipeline_mode=pl.Buffered(1)),
            pl.BlockSpec((halo, D), lambda m, f: (jnp.maximum(m * (tm // halo) - 1, 0), 0)),
            pl.BlockSpec((halo, D), lambda m, f: (jnp.minimum((m + 1) * (tm // halo), last_halo), 0)),
            pl.BlockSpec((None, 1, D), lambda m, f: (g_row, 0, 0)),
            pl.BlockSpec((None, 1, D), lambda m, f: (brow(m), 0, 3)),
            pl.BlockSpec((None, 1, D), lambda m, f: (brow(m), 0, 4)),
            pl.BlockSpec((None, 1, D), lambda m, f: (brow(m), 0, 5)),
            pl.BlockSpec((None, D, tf), lambda m, f: (layer, 0, f)),
            pl.BlockSpec((None, D, tf), lambda m, f: (layer, 0, f + nf)),
            pl.BlockSpec((None, CONV_W, tf), lambda m, f: (layer, 0, f)),
            pl.BlockSpec((None, CONV_W, tf), lambda m, f: (layer, 0, f + nf)),
            pl.BlockSpec((None, 1, tf), lambda m, f: (layer, 0, f)),
            pl.BlockSpec((None, 1, tf), lambda m, f: (layer, 0, f + nf)),
            pl.BlockSpec((None, tf, D), lambda m, f: (layer, f, 0)),
            pl.BlockSpec((1, D), lambda m, f: (0, 0)),
        ],
        out_specs=pl.BlockSpec((tm, D), lambda m, f: (m, 0)),
        out_shape=jax.ShapeDtypeStruct((rows, D), F32),
        scratch_shapes=[
            pltpu.VMEM((tm + 2 * halo, D), BF16),
            pltpu.VMEM((FFN_SLABS, tm + 2 * halo, LANES), F32),
            pltpu.VMEM((FFN_SLABS, tm + 2 * halo, LANES), F32),
            pltpu.VMEM((FFN_SLABS, tm, LANES), F32),
        ],
        compiler_params=_params("parallel", "arbitrary"),
        name="ffn",
    )(x, x, x, norm_g, mods, mods, mods, w_in, w_in, conv_w, conv_w, cb, cb, w_out, final_g.reshape(1, D))


NA_QROWS = 4
NA_QBLOCK = NA_QROWS * GRID_W
NA_SUB = 4
NA_BAND_ROWS = 12
NA_BAND = NA_BAND_ROWS * GRID_W


def _nt_dot(a, b):
    return lax.dot_general(a, b, (((1,), (1,)), ((), ())), preferred_element_type=F32)


NA_QBLOCKS = GRID_H // NA_QROWS
NA_SCALE = HEAD_DIM ** -0.5
NA_SCALE_LOG2E = np.float32(NA_SCALE * np.log2(np.e))
NA_KINDS = ((lambda j: 0, WIN_H - 1), (lambda j: j, WIN_H // 2 - 1), (lambda j: NA_BAND_ROWS - WIN_H, -1))


def _na_band_start(blk):
    return jnp.clip(blk * NA_QROWS - WIN_H // 2, 0, GRID_H - NA_BAND_ROWS)


def _na_lat_kernel(q_ref, k_ref, v_ref, kc_ref, vc_ref, bias_ref, o_ref):
    i = pl.program_id(2)
    kc = kc_ref[...]
    vc = vc_ref[...]

    def key_start(sb):
        return pl.multiple_of(_na_band_start(i * NA_SUB + sb) * GRID_W, NA_QROWS * GRID_W)

    def logits(sb):
        blk = i * NA_SUB + sb
        kind = jnp.where(blk == 0, 0, jnp.where(blk == NA_QBLOCKS - 1, 2, 1))
        q = q_ref[sb * NA_QBLOCK:(sb + 1) * NA_QBLOCK, :]
        return _nt_dot(q, k_ref[pl.ds(key_start(sb), NA_BAND), :]) + bias_ref[kind], _nt_dot(q, kc)

    nxt = logits(0)
    for sb in range(NA_SUB):
        t, tc = nxt
        if sb + 1 < NA_SUB:
            nxt = logits(sb + 1)
        mx = jnp.maximum(jnp.max(t, axis=-1, keepdims=True), jnp.max(tc, axis=-1, keepdims=True))
        e = jnp.exp2((t - mx) * NA_SCALE_LOG2E)
        ec = jnp.exp2((tc - mx) * NA_SCALE_LOG2E)
        denom = jnp.sum(e, axis=-1, keepdims=True) + jnp.sum(ec, axis=-1, keepdims=True)
        o = (jnp.dot(e.astype(BF16), v_ref[pl.ds(key_start(sb), NA_BAND), :], preferred_element_type=F32)
             + jnp.dot(ec.astype(BF16), vc, preferred_element_type=F32))
        o_ref[sb * NA_QBLOCK:(sb + 1) * NA_QBLOCK, :] = (o / denom).astype(o_ref.dtype)


def _na_bias_table(rpb):
    w = np.arange(GRID_W)
    col0 = np.clip(w - WIN_W // 2, 0, GRID_W - WIN_W)
    kcol = np.arange(GRID_W)
    in_win = (kcol[None, :] >= col0[:, None]) & (kcol[None, :] < col0[:, None] + WIN_W)
    pad = GRID_W - WIN_W
    padded = jnp.pad(rpb, ((0, 0), (0, 0), (pad, pad)))
    base = pad + WIN_W - 1
    t = jnp.stack([padded[:, :, base - wq:base - wq + GRID_W] for wq in range(GRID_W)], axis=2)
    t = jnp.where(jnp.asarray(in_win), t, MASK_VALUE)
    masked = jnp.full_like(t[:, 0], MASK_VALUE)
    kinds = []
    for first_valid, offset0 in NA_KINDS:
        rows = []
        for j in range(NA_QROWS):
            lo = first_valid(j)
            rows.append(jnp.concatenate(
                [t[:, c - j + offset0] if lo <= c < lo + WIN_H else masked for c in range(NA_BAND_ROWS)], axis=-1))
        kinds.append(jnp.concatenate(rows, axis=1))
    return jnp.stack(kinds, axis=1) * np.float32(1.0 / NA_SCALE)


def _na_lat_attention(qkv, bias_tab):
    blocks = SEQ // (NA_SUB * NA_QBLOCK)
    qb = NA_SUB * NA_QBLOCK
    return pl.pallas_call(
        _na_lat_kernel,
        grid=(HEADS, BATCH, blocks),
        in_specs=[
            pl.BlockSpec((qb, HEAD_DIM), lambda h, b, i: (b * blocks + i, h)),
            pl.BlockSpec((SEQ, HEAD_DIM), lambda h, b, i: (b, HEADS + h)),
            pl.BlockSpec((SEQ, HEAD_DIM), lambda h, b, i: (b, 2 * HEADS + h)),
            pl.BlockSpec((CTX_LEN, HEAD_DIM), lambda h, b, i: (CTX_ROW_BLOCK0 + b, HEADS + h)),
            pl.BlockSpec((CTX_LEN, HEAD_DIM), lambda h, b, i: (CTX_ROW_BLOCK0 + b, 2 * HEADS + h)),
            pl.BlockSpec((None, len(NA_KINDS), NA_QBLOCK, NA_BAND), lambda h, b, i: (h, 0, 0, 0)),
        ],
        out_specs=pl.BlockSpec((qb, HEAD_DIM), lambda h, b, i: (b * blocks + i, h)),
        out_shape=jax.ShapeDtypeStruct((T_LAT, D), BF16),
        compiler_params=_params("parallel", "parallel", "arbitrary"),
        name="na_lat",
    )(qkv, qkv, qkv, qkv, qkv, bias_tab)


def _softmax_pv(s, v):
    e = jnp.exp(s - jnp.max(s, axis=-1, keepdims=True))
    o = jnp.dot(e.astype(BF16), v, preferred_element_type=F32)
    return o / jnp.sum(e, axis=-1, keepdims=True)


def _na_ctx_kernel(q_ref, k_ref, v_ref, o_ref):
    scale = np.float32(HEAD_DIM ** -0.5)
    for h in range(HEADS):
        cols = slice(h * HEAD_DIM, (h + 1) * HEAD_DIM)
        s = _nt_dot(q_ref[:, cols], k_ref[:, cols]) * scale
        o_ref[:, cols] = _softmax_pv(s, v_ref[:, cols]).astype(o_ref.dtype)


def _na_ctx_attention(qkv):
    return pl.pallas_call(
        _na_ctx_kernel,
        grid=(BATCH,),
        in_specs=[
            pl.BlockSpec((CTX_LEN, D), lambda b: (CTX_ROW_BLOCK0 + b, 0)),
            pl.BlockSpec((CTX_LEN, D), lambda b: (CTX_ROW_BLOCK0 + b, 1)),
            pl.BlockSpec((CTX_LEN, D), lambda b: (CTX_ROW_BLOCK0 + b, 2)),
        ],
        out_specs=pl.BlockSpec((CTX_LEN, D), lambda b: (b, 0)),
        out_shape=jax.ShapeDtypeStruct((T_CTX, D), BF16),
        compiler_params=_params("parallel"),
        name="na_ctx",
    )(qkv, qkv, qkv)


MLA_QK = 2 * HEAD_DIM
MLA_QBLOCK = 256
MLA_SUB = 4
MLA_ZCOLS = MLA_Q_RANK + MLA_KV_RANK + 2 * HEAD_DIM
MLA_UP_TM = 512


def _rms(z, g):
    return (z * lax.rsqrt(jnp.mean(z * z, axis=-1, keepdims=True) + EPS)) * g


def _mla_up_kernel(z_ref, qn_ref, kvn_ref, cos_ref, sin_ref, wq_ref, wqs_ref, wkv_ref, q_ref, kv_ref, kpe_ref):
    cq = _rms(z_ref[:, :MLA_Q_RANK], qn_ref[...]).astype(BF16)
    ckv = _rms(z_ref[:, MLA_Q_RANK:MLA_Q_RANK + MLA_KV_RANK], kvn_ref[...]).astype(BF16)
    c0 = MLA_Q_RANK + MLA_KV_RANK
    cos = cos_ref[...]
    sin = sin_ref[...]
    kpe_ref[...] = (z_ref[:, c0:c0 + HEAD_DIM] * cos + z_ref[:, c0 + HEAD_DIM:] * sin).astype(BF16)
    for h in range(HEADS):
        q = jnp.dot(cq, wq_ref[h], preferred_element_type=F32)
        qs = jnp.dot(cq, wqs_ref[h], preferred_element_type=F32)
        q_ref[h, :, :HEAD_DIM] = q[:, :HEAD_DIM].astype(BF16)
        q_ref[h, :, HEAD_DIM:] = (q[:, HEAD_DIM:] * cos + qs * sin).astype(BF16)
        kv_ref[h] = jnp.dot(ckv, wkv_ref[h], preferred_element_type=F32).astype(BF16)


def _mla_up(z, q_norm, kv_norm, cos, sin, wq, wqs, wkv):
    tm = MLA_UP_TM
    return pl.pallas_call(
        _mla_up_kernel,
        grid=(T_ALL // tm,),
        in_specs=[
            pl.BlockSpec((tm, MLA_ZCOLS), lambda m: (m, 0)),
            pl.BlockSpec((1, MLA_Q_RANK), lambda m: (0, 0)),
            pl.BlockSpec((1, MLA_KV_RANK), lambda m: (0, 0)),
            pl.BlockSpec((tm, HEAD_DIM), lambda m: (m, 0)),
            pl.BlockSpec((tm, HEAD_DIM), lambda m: (m, 0)),
            _resident((HEADS, MLA_Q_RANK, MLA_QK), lambda m: (0, 0, 0)),
            _resident((HEADS, MLA_Q_RANK, HEAD_DIM), lambda m: (0, 0, 0)),
            _resident((HEADS, MLA_KV_RANK, 2 * HEAD_DIM), lambda m: (0, 0, 0)),
        ],
        out_specs=[
            pl.BlockSpec((HEADS, tm, MLA_QK), lambda m: (0, m, 0)),
            pl.BlockSpec((HEADS, tm, 2 * HEAD_DIM), lambda m: (0, m, 0)),
            pl.BlockSpec((tm, HEAD_DIM), lambda m: (m, 0)),
        ],
        out_shape=[
            jax.ShapeDtypeStruct((HEADS, T_ALL, MLA_QK), BF16),
            jax.ShapeDtypeStruct((HEADS, T_ALL, 2 * HEAD_DIM), BF16),
            jax.ShapeDtypeStruct((T_ALL, HEAD_DIM), BF16),
        ],
        compiler_params=_params("parallel"),
        name="mla_up",
    )(z, q_norm.reshape(1, -1), kv_norm.reshape(1, -1), cos, sin, wq, wqs, wkv)


MLA_SCALE = np.float32((MLA_NOPE + MLA_ROPE) ** -0.5)
MLA_SCALE_LOG2E = np.float32((MLA_NOPE + MLA_ROPE) ** -0.5 * np.log2(np.e))


def _mla_lat_kernel(q_ref, kn_ref, v_ref, kpe_ref, knc_ref, vc_ref, kpec_ref, o_ref, kcat_ref):
    @pl.when(pl.program_id(2) == 0)
    def _():
        kcat_ref[:SEQ, :HEAD_DIM] = kn_ref[...]
        kcat_ref[:SEQ, HEAD_DIM:] = kpe_ref[...]
        kcat_ref[SEQ:, :HEAD_DIM] = knc_ref[...]
        kcat_ref[SEQ:, HEAD_DIM:] = kpec_ref[...]

    def scores(sb):
        return _nt_dot(q_ref[sb * MLA_QBLOCK:(sb + 1) * MLA_QBLOCK, :], kcat_ref[...])

    nxt = scores(0)
    for sb in range(MLA_SUB):
        rows = slice(sb * MLA_QBLOCK, (sb + 1) * MLA_QBLOCK)
        s = nxt
        if sb + 1 < MLA_SUB:
            nxt = scores(sb + 1)
        e = jnp.exp2((s - jnp.max(s, axis=-1, keepdims=True)) * MLA_SCALE_LOG2E)
        o = (jnp.dot(e[:, :SEQ].astype(BF16), v_ref[...], preferred_element_type=F32)
             + jnp.dot(e[:, SEQ:].astype(BF16), vc_ref[...], preferred_element_type=F32))
        o_ref[rows, :] = (o / jnp.sum(e, axis=-1, keepdims=True)).astype(o_ref.dtype)


def _mla_lat_attention(q, kv, kpe):
    qb = MLA_SUB * MLA_QBLOCK
    blocks = SEQ // qb
    ctx_q = CTX_ROW_BLOCK0
    return pl.pallas_call(
        _mla_lat_kernel,
        grid=(BATCH, HEADS, blocks),
        in_specs=[
            pl.BlockSpec((None, qb, MLA_QK), lambda b, h, i: (h, b * blocks + i, 0)),
            pl.BlockSpec((None, SEQ, HEAD_DIM), lambda b, h, i: (h, b, 0)),
            pl.BlockSpec((None, SEQ, HEAD_DIM), lambda b, h, i: (h, b, 1)),
            pl.BlockSpec((SEQ, HEAD_DIM), lambda b, h, i: (b, 0)),
            pl.BlockSpec((None, CTX_LEN, HEAD_DIM), lambda b, h, i: (h, ctx_q + b, 0)),
            pl.BlockSpec((None, CTX_LEN, HEAD_DIM), lambda b, h, i: (h, ctx_q + b, 1)),
            pl.BlockSpec((CTX_LEN, HEAD_DIM), lambda b, h, i: (ctx_q + b, 0)),
        ],
        out_specs=pl.BlockSpec((qb, HEAD_DIM), lambda b, h, i: (b * blocks + i, h)),
        out_shape=jax.ShapeDtypeStruct((T_LAT, D), BF16),
        scratch_shapes=[pltpu.VMEM((SEQ + CTX_LEN, MLA_QK), BF16)],
        compiler_params=_params("parallel", "parallel", "arbitrary"),
        name="mla_lat",
    )(q, kv, kv, kpe, kv, kv, kpe)


def _mla_ctx_kernel(q_ref, kv_ref, kpe_ref, o_ref):
    kpe = kpe_ref[...]
    for h in range(HEADS):
        kcat = jnp.concatenate([kv_ref[h, :, :HEAD_DIM], kpe], axis=1)
        s = _nt_dot(q_ref[h], kcat) * MLA_SCALE
        o_ref[:, h * HEAD_DIM:(h + 1) * HEAD_DIM] = _softmax_pv(s, kv_ref[h, :, HEAD_DIM:]).astype(o_ref.dtype)


def _mla_ctx_attention(q, kv, kpe):
    ctx_q = CTX_ROW_BLOCK0
    return pl.pallas_call(
        _mla_ctx_kernel,
        grid=(BATCH,),
        in_specs=[
            pl.BlockSpec((HEADS, CTX_LEN, MLA_QK), lambda b: (0, ctx_q + b, 0)),
            pl.BlockSpec((HEADS, CTX_LEN, 2 * HEAD_DIM), lambda b: (0, ctx_q + b, 0)),
            pl.BlockSpec((CTX_LEN, HEAD_DIM), lambda b: (ctx_q + b, 0)),
        ],
        out_specs=pl.BlockSpec((CTX_LEN, D), lambda b: (b, 0)),
        out_shape=jax.ShapeDtypeStruct((T_CTX, D), BF16),
        compiler_params=_params("parallel"),
        name="mla_ctx",
    )(q, kv, kpe)


def _rope_tables():
    half = MLA_ROPE // 2
    freqs = ROPE_THETA ** (-jnp.arange(0, half, 2, dtype=F32) / half)
    t = jnp.arange(SEQ)
    rows = (t // GRID_W).astype(F32)[:, None] * freqs
    cols = (t % GRID_W).astype(F32)[:, None] * freqs
    cr, sr, cc, sn = jnp.cos(rows), jnp.sin(rows), jnp.cos(cols), jnp.sin(cols)
    pad = jnp.zeros((SEQ, HEAD_DIM - MLA_ROPE), F32)
    cos = jnp.concatenate([cr, cr, cc, cc, pad], axis=1)
    sin = jnp.concatenate([-sr, sr, -sn, sn, pad], axis=1)
    cos_c = jnp.concatenate([jnp.ones((T_CTX, MLA_ROPE), F32), jnp.zeros((T_CTX, HEAD_DIM - MLA_ROPE), F32)], axis=1)
    return (jnp.concatenate([jnp.tile(cos, (BATCH, 1)), cos_c], axis=0),
            jnp.concatenate([jnp.tile(sin, (BATCH, 1)), jnp.zeros((T_CTX, HEAD_DIM), F32)], axis=0))


def _swap_rope_halves(w):
    q = MLA_ROPE // 4
    return jnp.concatenate([w[..., q:2 * q], w[..., :q], w[..., 3 * q:], w[..., 2 * q:3 * q]], axis=-1)


def _sgu_kernel(u_ref, v_ref, g_ref, b_ref, ws_ref, bs_ref, o_ref):
    v = v_ref[...]
    mu = jnp.mean(v, axis=-1, keepdims=True)
    vc = v - mu
    var = jnp.mean(vc * vc, axis=-1, keepdims=True)
    vn = ((vc * lax.rsqrt(var + EPS)) * g_ref[...] + b_ref[...]).astype(BF16)
    for g in range(SG_GROUPS):
        cols = slice(g * CHUNK, (g + 1) * CHUNK)
        mix = jnp.dot(ws_ref[g], vn[:, cols], preferred_element_type=F32) + bs_ref[:, cols]
        o_ref[:, cols] = (u_ref[:, cols] * mix).astype(o_ref.dtype)


def _sgu(uv, ln_g, ln_b, ws, bs_full):
    rows = uv.shape[0]
    return pl.pallas_call(
        _sgu_kernel,
        grid=(rows // CHUNK,),
        in_specs=[
            pl.BlockSpec((CHUNK, D), lambda m: (m, 0)),
            pl.BlockSpec((CHUNK, D), lambda m: (m, 1)),
            pl.BlockSpec((1, D), lambda m: (0, 0)),
            pl.BlockSpec((1, D), lambda m: (0, 0)),
            pl.BlockSpec((SG_GROUPS, CHUNK, CHUNK), lambda m: (0, 0, 0)),
            pl.BlockSpec((CHUNK, D), lambda m: (0, 0)),
        ],
        out_specs=pl.BlockSpec((CHUNK, D), lambda m: (m, 0)),
        out_shape=jax.ShapeDtypeStruct((rows, D), BF16),
        compiler_params=_params("parallel"),
        name="sgu",
    )(uv, uv, ln_g.reshape(1, D), ln_b.reshape(1, D), ws, bs_full)


def kernel(x, c, ctx, c_ctx, ada_w, ada_b, norm_g, final_g, a_w_qkv, a_w_o, a_rpb, b_w_in, b_q_norm, b_kv_norm,
           b_w_uq, b_w_ukv, b_w_o, c_w_in, c_b_in, c_ln_g, c_ln_b, c_ws, c_bs, c_w_o, f_w_in, f_conv_w, f_conv_b,
           f_w_out):
    xs = jnp.concatenate([x.reshape(T_LAT, D), ctx.reshape(T_CTX, D)], axis=0)
    cond = jnp.concatenate([c, c_ctx[None, :], jnp.zeros((8 - BATCH - 1, D), F32)], axis=0)
    mods_all = _ada(cond, ada_w, ada_b).reshape(DEPTH, 8, 1, 6 * D)
    norm_rows = norm_g.reshape(DEPTH * 2, 1, D)

    a_w_qkv_b, a_w_o_b = a_w_qkv.astype(BF16), a_w_o.astype(BF16)
    b_w_o_b, c_w_in_b, c_w_o_b = b_w_o.astype(BF16), c_w_in.astype(BF16), c_w_o.astype(BF16)
    f_w_in_b, f_w_out_b = f_w_in.astype(BF16), f_w_out.astype(BF16)

    for i in range(DEPTH):
        kind, j = i % N_MIXERS, i // N_MIXERS
        last = i == DEPTH - 1
        mods = mods_all[i]
        rows_out = T_LAT if last else T_ALL

        if kind == 0:
            qkv = _mod_matmul(xs, norm_rows, 2 * i, mods, 0, a_w_qkv_b, j, tm=PROJ_TM, tn=PROJ_TN, out_dtype=BF16,
                              name="na_qkv")
            att = _na_lat_attention(qkv, _na_bias_table(a_rpb[j]))
            att_ctx = None if last else _na_ctx_attention(qkv)
            w_o = a_w_o_b
        elif kind == 1:
            w_in = b_w_in[j]
            c0 = MLA_Q_RANK + MLA_KV_RANK
            zpad = jnp.zeros((D, HEAD_DIM - MLA_ROPE), F32)
            w_in_ext = jnp.concatenate([w_in, zpad, _swap_rope_halves(w_in[:, c0:]), zpad], axis=1)
            z = _mod_matmul(xs, norm_rows, 2 * i, mods, 0, w_in_ext.astype(BF16)[None], 0, tm=PROJ_TM,
                            tn=MLA_ZCOLS // 2, out_dtype=F32, name="mla_in")
            wq = b_w_uq[j].reshape(MLA_Q_RANK, HEADS, MLA_NOPE + MLA_ROPE).transpose(1, 0, 2)
            hpad = jnp.zeros((HEADS, MLA_Q_RANK, HEAD_DIM - MLA_ROPE), F32)
            wq_cat = jnp.concatenate([wq, hpad], axis=-1)
            wq_swap = jnp.concatenate([_swap_rope_halves(wq[..., MLA_NOPE:]), hpad], axis=-1)
            wkv = b_w_ukv[j].reshape(MLA_KV_RANK, HEADS, 2 * HEAD_DIM).transpose(1, 0, 2)
            cos, sin = _rope_tables()
            q, kv, kpe = _mla_up(z, b_q_norm[j], b_kv_norm[j], cos, sin, wq_cat.astype(BF16), wq_swap.astype(BF16),
                                 wkv.astype(BF16))
            att = _mla_lat_attention(q, kv, kpe)
            att_ctx = _mla_ctx_attention(q, kv, kpe)
            w_o = b_w_o_b
        else:
            uv = _mod_matmul(xs, norm_rows, 2 * i, mods, 0, c_w_in_b, j, tm=PROJ_TM, tn=PROJ_TN, out_dtype=F32,
                             bias=c_b_in, act="gelu", name="sgu_in")
            bs_full = jnp.repeat(c_bs[j].T, CHUNK, axis=1)
            att = _sgu(uv, c_ln_g[j], c_ln_b[j], c_ws[j].astype(BF16), bs_full)
            att_ctx = None if last else att
            w_o = c_w_o_b

        xs = _mixer_out(att, att_ctx, w_o, j, xs, mods)
        xs = _ffn(xs, norm_rows, 2 * i + 1, mods, f_w_in_b, f_conv_w, f_conv_b, f_w_out_b, i, final_g, rows=rows_out,
                  final_norm=last)

    return xs.reshape(BATCH, SEQ, D)
```

```python
import functools

import jax
import jax.numpy as jnp
import numpy as np
from jax import lax
from jax.experimental import pallas as pl
from jax.experimental.pallas import tpu as pltpu

F32 = jnp.float32
BF16 = jnp.bfloat16

D = 2048
BATCH = 4
SEQ = 2048
DEPTH = 4
GRID_W = 64
GRID_H = SEQ // GRID_W
CTX_LEN = 256
N_MIXERS = 3
EPS = 1e-6
ROPE_THETA = 10000.0
HEADS = 16
HEAD_DIM = 128
WIN_H = 8
WIN_W = 16
MLA_NOPE = 128
MLA_ROPE = 64
MLA_Q_RANK = 512
MLA_KV_RANK = 512
CHUNK = 128
SG_GROUPS = 16
D_FF = 5632
CONV_W = 3

T_LAT = BATCH * SEQ
T_CTX = BATCH * CTX_LEN
T_ALL = T_LAT + T_CTX
CTX_ROW_BLOCK0 = T_LAT // CTX_LEN
MASK_VALUE = -1e30
VMEM_LIMIT = 56 * 1024 * 1024
LANES = 128
BF16_ROWS = 16
PROLOGUE_ROWS = 128
PROJ_TM = 1024
PROJ_TN = 1024


def _params(*semantics):
    return pltpu.CompilerParams(dimension_semantics=semantics, vmem_limit_bytes=VMEM_LIMIT)


def _mod_row(tm):
    per_batch = SEQ // tm
    return lambda m: jnp.minimum(m // per_batch, BATCH)


def _resident(block_shape, index_map):
    return pl.BlockSpec(block_shape, index_map, pipeline_mode=pl.Buffered(1))


def _ada_kernel(s_ref, w_ref, b_ref, o_ref):
    s = s_ref[...]
    s = s * jax.nn.sigmoid(s)
    o_ref[...] = jnp.dot(s.astype(BF16), w_ref[...].astype(BF16), preferred_element_type=F32) + b_ref[...]


def _ada(s, ada_w, ada_b):
    tn = 1024
    n6 = 6 * D
    return pl.pallas_call(
        _ada_kernel,
        grid=(DEPTH, n6 // tn),
        in_specs=[
            pl.BlockSpec((8, D), lambda i, n: (0, 0)),
            pl.BlockSpec((None, D, tn), lambda i, n: (i, 0, n)),
            pl.BlockSpec((None, 1, tn), lambda i, n: (i, 0, n)),
        ],
        out_specs=pl.BlockSpec((None, 8, tn), lambda i, n: (i, 0, n)),
        out_shape=jax.ShapeDtypeStruct((DEPTH, 8, n6), F32),
        compiler_params=_params("parallel", "parallel"),
        name="ada",
    )(s, ada_w, ada_b.reshape(DEPTH, 1, n6))


def _gelu_exact(z):
    return 0.5 * z * (1.0 + lax.erf(z * np.float32(np.sqrt(0.5))))


def _modulate(xf, g, shift, scale):
    y = xf * lax.rsqrt(jnp.mean(xf * xf, axis=-1, keepdims=True) + EPS)
    return y * (g * (1.0 + scale)) + shift


def _modulate_rows(x_ref, h_ref, h_row0, n_rows, g, shift, scale):
    def body(c, carry):
        r = pl.multiple_of(c * PROLOGUE_ROWS, PROLOGUE_ROWS)
        h = _modulate(x_ref[pl.ds(r, PROLOGUE_ROWS), :], g, shift, scale)
        h_ref[pl.ds(h_row0 + r, PROLOGUE_ROWS), :] = h.astype(BF16)
        return carry
    lax.fori_loop(0, n_rows // PROLOGUE_ROWS, body, 0)


def _mod_mm_kernel(x_ref, g_ref, sh_ref, sc_ref, w_ref, *rest, has_bias, act, tm):
    if has_bias:
        b_ref, o_ref, h_ref = rest
    else:
        o_ref, h_ref = rest

    @pl.when(pl.program_id(1) == 0)
    def _():
        _modulate_rows(x_ref, h_ref, 0, tm, g_ref[...], sh_ref[...], sc_ref[...])

    acc = jnp.dot(h_ref[...], w_ref[...], preferred_element_type=F32)
    if has_bias:
        acc = acc + b_ref[...]
    if act == "gelu":
        acc = _gelu_exact(acc)
    o_ref[...] = acc.astype(o_ref.dtype)


def _mod_matmul(x, norm_g, g_row, mods, shift_col, w, w_layer, *, tm, tn, out_dtype, bias=None, act=None, name):
    n = w.shape[2]
    brow = _mod_row(tm)
    in_specs = [
        pl.BlockSpec((tm, D), lambda m, j: (m, 0)),
        pl.BlockSpec((None, 1, D), lambda m, j: (g_row, 0, 0)),
        pl.BlockSpec((None, 1, D), lambda m, j: (brow(m), 0, shift_col)),
        pl.BlockSpec((None, 1, D), lambda m, j: (brow(m), 0, shift_col + 1)),
        pl.BlockSpec((None, D, tn), lambda m, j: (w_layer, 0, j)),
    ]
    args = [x, norm_g, mods, mods, w]
    if bias is not None:
        in_specs.append(pl.BlockSpec((None, 1, tn), lambda m, j: (w_layer, 0, j)))
        args.append(bias.reshape(bias.shape[0], 1, n))
    return pl.pallas_call(
        functools.partial(_mod_mm_kernel, has_bias=bias is not None, act=act, tm=tm),
        grid=(T_ALL // tm, n // tn),
        in_specs=in_specs,
        out_specs=pl.BlockSpec((tm, tn), lambda m, j: (m, j)),
        out_shape=jax.ShapeDtypeStruct((T_ALL, n), out_dtype),
        scratch_shapes=[pltpu.VMEM((tm, D), BF16)],
        compiler_params=_params("parallel", "arbitrary"),
        name=name,
    )(*args)


MIXER_OUT_COLS = 512


MIXER_OUT_TM = 512
MIXER_LAT_TILES = T_LAT // MIXER_OUT_TM


def _mixer_out_kernel(*refs, has_ctx):
    if has_ctx:
        a_ref, ac_ref, w_ref, r_ref, gate_ref, o_ref = refs
        a = jnp.where(pl.program_id(0) < MIXER_LAT_TILES, a_ref[...], ac_ref[...])
    else:
        a_ref, w_ref, r_ref, gate_ref, o_ref = refs
        a = a_ref[...]
    for c in range(D // MIXER_OUT_COLS):
        cols = slice(c * MIXER_OUT_COLS, (c + 1) * MIXER_OUT_COLS)
        acc = jnp.dot(a, w_ref[:, cols], preferred_element_type=F32)
        o_ref[:, cols] = r_ref[:, cols] + gate_ref[:, cols] * acc


def _mixer_out(a_lat, a_ctx, w, w_layer, resid, mods):
    tm = MIXER_OUT_TM
    brow = _mod_row(tm)
    has_ctx = a_ctx is not None
    rows = T_ALL if has_ctx else T_LAT
    in_specs = [pl.BlockSpec((tm, D), lambda m: (jnp.minimum(m, MIXER_LAT_TILES - 1), 0))]
    args = [a_lat]
    if has_ctx:
        ctx_tile0 = (a_ctx.shape[0] - T_CTX) // tm
        in_specs.append(pl.BlockSpec((tm, D), lambda m: (jnp.maximum(m - MIXER_LAT_TILES, 0) + ctx_tile0, 0)))
        args.append(a_ctx)
    in_specs += [
        _resident((None, D, D), lambda m: (w_layer, 0, 0)),
        pl.BlockSpec((tm, D), lambda m: (m, 0)),
        pl.BlockSpec((None, 1, D), lambda m: (brow(m), 0, 2)),
    ]
    return pl.pallas_call(
        functools.partial(_mixer_out_kernel, has_ctx=has_ctx),
        grid=(rows // tm,),
        in_specs=in_specs,
        out_specs=pl.BlockSpec((tm, D), lambda m: (m, 0)),
        out_shape=jax.ShapeDtypeStruct((rows, D), F32),
        compiler_params=_params("parallel"),
        name="mixer_out",
    )(*args, w, resid, mods)


FFN_TM = 1024
FFN_TF = 512
FFN_HALO = BF16_ROWS
FFN_OUT_COLS = 512
FFN_ROW_CHUNK = CTX_LEN
FFN_LOOKAHEAD = 4
FFN_ROW_SETS = 4
FFN_SLABS = FFN_TF // LANES


def _ffn_kernel(x_ref, xp_ref, xn_ref, g_ref, sh_ref, sc_ref, gate_ref, wa_ref, wg_ref, cwa_ref, cwg_ref,
                cba_ref, cbg_ref, wo_ref, fg_ref, o_ref, h_ref, ua_ref, ug_ref, act_ref, *, final_norm):
    tm, halo = FFN_TM, FFN_HALO
    f = pl.program_id(1)

    @pl.when(f == 0)
    def _():
        g, sh, sc = g_ref[...], sh_ref[...], sc_ref[...]
        h_ref[:halo, :] = _modulate(xp_ref[...], g, sh, sc).astype(BF16)
        h_ref[halo + tm:, :] = _modulate(xn_ref[...], g, sh, sc).astype(BF16)
        _modulate_rows(x_ref, h_ref, halo, tm, g, sh, sc)
        o_ref[...] = jnp.zeros_like(o_ref)

    ch, n_ch, sets = FFN_ROW_CHUNK, tm // FFN_ROW_CHUNK, FFN_ROW_SETS
    set_rows = ch // sets
    sub = lax.broadcasted_iota(jnp.int32, (set_rows, LANES), 0)

    def hidden(r):
        lo = 0 if r == 0 else halo + r * ch
        hi = tm + 2 * halo if r == n_ch - 1 else halo + (r + 1) * ch
        h = h_ref[lo:hi, :]
        ua = jnp.dot(h, wa_ref[...], preferred_element_type=F32)
        ug = jnp.dot(h, wg_ref[...], preferred_element_type=F32)
        for s in range(FFN_SLABS):
            ua_ref[s, lo:hi, :] = ua[:, s * LANES:(s + 1) * LANES]
            ug_ref[s, lo:hi, :] = ug[:, s * LANES:(s + 1) * LANES]

    def gate_and_project(r):
        base = halo + r * ch
        rows = slice(r * ch, (r + 1) * ch)
        row0 = pl.program_id(0) * tm + r * ch
        seq_mask = jnp.where(row0 < T_LAT, SEQ - 1, CTX_LEN - 1)
        keep_first_up = jnp.where((row0 & seq_mask) == 0, 0.0, 1.0).astype(F32)
        keep_last_dn = jnp.where(((row0 + ch) & seq_mask) == 0, 0.0, 1.0).astype(F32)
        first_up = jnp.where(sub == 0, keep_first_up, 1.0)
        last_dn = jnp.where(sub == set_rows - 1, keep_last_dn, 1.0)

        for s in range(FFN_SLABS):
            lanes = slice(s * LANES, (s + 1) * LANES)
            for j in range(sets):
                def conv(u_ref, cw_ref, cb_ref):
                    up = u_ref[s, pl.ds(base + j - 1, set_rows, stride=sets), :]
                    cur = u_ref[s, pl.ds(base + j, set_rows, stride=sets), :]
                    dn = u_ref[s, pl.ds(base + j + 1, set_rows, stride=sets), :]
                    if j == 0:
                        up = up * first_up
                    if j == sets - 1:
                        dn = dn * last_dn
                    return (up * cw_ref[0:1, lanes] + cur * cw_ref[1:2, lanes] + dn * cw_ref[2:3, lanes]
                            + cb_ref[:, lanes])

                a = conv(ua_ref, cwa_ref, cba_ref)
                gg = conv(ug_ref, cwg_ref, cbg_ref)
                act_ref[s, pl.ds(r * ch + j, set_rows, stride=sets), :] = a * (gg * jax.nn.sigmoid(gg))

        act = jnp.concatenate([act_ref[s, rows, :] for s in range(FFN_SLABS)], axis=1).astype(BF16)
        for c in range(D // FFN_OUT_COLS):
            cols = slice(c * FFN_OUT_COLS, (c + 1) * FFN_OUT_COLS)
            o_ref[rows, cols] += jnp.dot(act, wo_ref[:, cols], preferred_element_type=F32)

    for r in range(min(FFN_LOOKAHEAD, n_ch)):
        hidden(r)
    for r in range(n_ch):
        if r + FFN_LOOKAHEAD < n_ch:
            hidden(r + FFN_LOOKAHEAD)
        gate_and_project(r)

    @pl.when(f == pl.num_programs(1) - 1)
    def _():
        gate = gate_ref[...]

        def body(c, carry):
            rows = pl.ds(pl.multiple_of(c * PROLOGUE_ROWS, PROLOGUE_ROWS), PROLOGUE_ROWS)
            y = x_ref[rows, :] + gate * o_ref[rows, :]
            if final_norm:
                y = (y * lax.rsqrt(jnp.mean(y * y, axis=-1, keepdims=True) + EPS)) * fg_ref[...]
            o_ref[rows, :] = y
            return carry
        lax.fori_loop(0, tm // PROLOGUE_ROWS, body, 0)


def _ffn(x, norm_g, g_row, mods, w_in, conv_w, conv_b, w_out, layer, final_g, *, rows, final_norm):
    tm, tf, halo = FFN_TM, FFN_TF, FFN_HALO
    nf = D_FF // tf
    brow = _mod_row(tm)
    last_halo = rows // halo - 1
    cb = conv_b.reshape(DEPTH, 1, 2 * D_FF)
    return pl.pallas_call(
        functools.partial(_ffn_kernel, final_norm=final_norm),
        grid=(rows // tm, nf),
        in_specs=[
            pl.BlockSpec((tm, D), lambda m, f: (m, 0), pipeline_mode=pl.Buffered(1)),
            pl.BlockSpec((halo, D), lambda m, f: (jnp.maximum(m * (tm // halo) - 1, 0), 0)),
            pl.BlockSpec((halo, D), lambda m, f: (jnp.minimum((m + 1) * (tm // halo), last_halo), 0)),
            pl.BlockSpec((None, 1, D), lambda m, f: (g_row, 0, 0)),
            pl.BlockSpec((None, 1, D), lambda m, f: (brow(m), 0, 3)),
            pl.BlockSpec((None, 1, D), lambda m, f: (brow(m), 0, 4)),
            pl.BlockSpec((None, 1, D), lambda m, f: (brow(m), 0, 5)),
            pl.BlockSpec((None, D, tf), lambda m, f: (layer, 0, f)),
            pl.BlockSpec((None, D, tf), lambda m, f: (layer, 0, f + nf)),
            pl.BlockSpec((None, CONV_W, tf), lambda m, f: (layer, 0, f)),
            pl.BlockSpec((None, CONV_W, tf), lambda m, f: (layer, 0, f + nf)),
            pl.BlockSpec((None, 1, tf), lambda m, f: (layer, 0, f)),
            pl.BlockSpec((None, 1, tf), lambda m, f: (layer, 0, f + nf)),
            pl.BlockSpec((None, tf, D), lambda m, f: (layer, f, 0)),
            pl.BlockSpec((1, D), lambda m, f: (0, 0)),
        ],
        out_specs=pl.BlockSpec((tm, D), lambda m, f: (m, 0)),
        out_shape=jax.ShapeDtypeStruct((rows, D), F32),
        scratch_shapes=[
            pltpu.VMEM((tm + 2 * halo, D), BF16),
            pltpu.VMEM((FFN_SLABS, tm + 2 * halo, LANES), F32),
            pltpu.VMEM((FFN_SLABS, tm + 2 * halo, LANES), F32),
            pltpu.VMEM((FFN_SLABS, tm, LANES), F32),
        ],
        compiler_params=_params("parallel", "arbitrary"),
        name="ffn",
    )(x, x, x, norm_g, mods, mods, mods, w_in, w_in, conv_w, conv_w, cb, cb, w_out, final_g.reshape(1, D))


NA_QROWS = 4
NA_QBLOCK = NA_QROWS * GRID_W
NA_SUB = 4
NA_BAND_ROWS = 12
NA_BAND = NA_BAND_ROWS * GRID_W


ATTN_LOOKAHEAD = 2


def _nt_dot(a, b):
    return lax.dot_general(a, b, (((1,), (1,)), ((), ())), preferred_element_type=F32)


NA_QBLOCKS = GRID_H // NA_QROWS
NA_SCALE = HEAD_DIM ** -0.5
NA_SCALE_LOG2E = np.float32(NA_SCALE * np.log2(np.e))
NA_KINDS = ((lambda j: 0, WIN_H - 1), (lambda j: j, WIN_H // 2 - 1), (lambda j: NA_BAND_ROWS - WIN_H, -1))


def _na_band_start(blk):
    return jnp.clip(blk * NA_QROWS - WIN_H // 2, 0, GRID_H - NA_BAND_ROWS)


def _na_lat_kernel(q_ref, k_ref, v_ref, kc_ref, vc_ref, bias_ref, o_ref):
    i = pl.program_id(2)
    kc = kc_ref[...]
    vc = vc_ref[...]

    def key_start(sb):
        return pl.multiple_of(_na_band_start(i * NA_SUB + sb) * GRID_W, NA_QROWS * GRID_W)

    def logits(sb):
        blk = i * NA_SUB + sb
        kind = jnp.where(blk == 0, 0, jnp.where(blk == NA_QBLOCKS - 1, 2, 1))
        q = q_ref[sb * NA_QBLOCK:(sb + 1) * NA_QBLOCK, :]
        return _nt_dot(q, k_ref[pl.ds(key_start(sb), NA_BAND), :]) + bias_ref[kind], _nt_dot(q, kc)

    ahead = [logits(sb) for sb in range(min(ATTN_LOOKAHEAD, NA_SUB))]
    for sb in range(NA_SUB):
        t, tc = ahead.pop(0)
        if sb + ATTN_LOOKAHEAD < NA_SUB:
            ahead.append(logits(sb + ATTN_LOOKAHEAD))
        mx = jnp.maximum(jnp.max(t, axis=-1, keepdims=True), jnp.max(tc, axis=-1, keepdims=True))
        e = jnp.exp2((t - mx) * NA_SCALE_LOG2E)
        ec = jnp.exp2((tc - mx) * NA_SCALE_LOG2E)
        denom = jnp.sum(e, axis=-1, keepdims=True) + jnp.sum(ec, axis=-1, keepdims=True)
        o = (jnp.dot(e.astype(BF16), v_ref[pl.ds(key_start(sb), NA_BAND), :], preferred_element_type=F32)
             + jnp.dot(ec.astype(BF16), vc, preferred_element_type=F32))
        o_ref[sb * NA_QBLOCK:(sb + 1) * NA_QBLOCK, :] = (o / denom).astype(o_ref.dtype)


def _na_bias_table(rpb):
    n = rpb.shape[0]
    w = np.arange(GRID_W)[:, None]
    k = np.arange(GRID_W)[None, :]
    col0 = np.clip(w - WIN_W // 2, 0, GRID_W - WIN_W)
    in_win = (k >= col0) & (k < col0 + WIN_W)
    col_pick = (np.arange(2 * WIN_W - 1)[:, None, None] == (k - w + WIN_W - 1)[None]) & in_win[None]
    t = jnp.einsum("nrx,xwk->nwrk", rpb, col_pick.astype(np.float32), precision=lax.Precision.HIGHEST)
    inv_scale = np.float32(1.0 / NA_SCALE)
    t = jnp.where(jnp.asarray(in_win)[None, :, None, :], t, MASK_VALUE) * inv_scale
    rows = []
    for first_valid, offset0 in NA_KINDS:
        for j in range(NA_QROWS):
            lo = first_valid(j)
            r0 = lo - j + offset0
            valid = t[:, :, r0:r0 + WIN_H, :].reshape(n, GRID_W, WIN_H * GRID_W)
            rows.append(jnp.pad(valid, ((0, 0), (0, 0), (lo * GRID_W, (NA_BAND_ROWS - WIN_H - lo) * GRID_W)),
                                constant_values=MASK_VALUE * inv_scale))
    return jnp.stack(rows, axis=1).reshape(n, len(NA_KINDS), NA_QBLOCK, NA_BAND)


def _na_lat_attention(qkv, bias_tab, layer):
    blocks = SEQ // (NA_SUB * NA_QBLOCK)
    qb = NA_SUB * NA_QBLOCK
    return pl.pallas_call(
        _na_lat_kernel,
        grid=(HEADS, BATCH, blocks),
        in_specs=[
            pl.BlockSpec((qb, HEAD_DIM), lambda h, b, i: (b * blocks + i, h)),
            pl.BlockSpec((SEQ, HEAD_DIM), lambda h, b, i: (b, HEADS + h)),
            pl.BlockSpec((SEQ, HEAD_DIM), lambda h, b, i: (b, 2 * HEADS + h)),
            pl.BlockSpec((CTX_LEN, HEAD_DIM), lambda h, b, i: (CTX_ROW_BLOCK0 + b, HEADS + h)),
            pl.BlockSpec((CTX_LEN, HEAD_DIM), lambda h, b, i: (CTX_ROW_BLOCK0 + b, 2 * HEADS + h)),
            pl.BlockSpec((None, len(NA_KINDS), NA_QBLOCK, NA_BAND), lambda h, b, i: (layer * HEADS + h, 0, 0, 0)),
        ],
        out_specs=pl.BlockSpec((qb, HEAD_DIM), lambda h, b, i: (b * blocks + i, h)),
        out_shape=jax.ShapeDtypeStruct((T_LAT, D), BF16),
        compiler_params=_params("parallel", "parallel", "arbitrary"),
        name="na_lat",
    )(qkv, qkv, qkv, qkv, qkv, bias_tab)


def _softmax_pv(s, v):
    e = jnp.exp(s - jnp.max(s, axis=-1, keepdims=True))
    o = jnp.dot(e.astype(BF16), v, preferred_element_type=F32)
    return o / jnp.sum(e, axis=-1, keepdims=True)


def _na_ctx_kernel(q_ref, k_ref, v_ref, o_ref):
    scale = np.float32(HEAD_DIM ** -0.5)
    for h in range(HEADS):
        cols = slice(h * HEAD_DIM, (h + 1) * HEAD_DIM)
        s = _nt_dot(q_ref[:, cols], k_ref[:, cols]) * scale
        o_ref[:, cols] = _softmax_pv(s, v_ref[:, cols]).astype(o_ref.dtype)


def _na_ctx_attention(qkv):
    return pl.pallas_call(
        _na_ctx_kernel,
        grid=(BATCH,),
        in_specs=[
            pl.BlockSpec((CTX_LEN, D), lambda b: (CTX_ROW_BLOCK0 + b, 0)),
            pl.BlockSpec((CTX_LEN, D), lambda b: (CTX_ROW_BLOCK0 + b, 1)),
            pl.BlockSpec((CTX_LEN, D), lambda b: (CTX_ROW_BLOCK0 + b, 2)),
        ],
        out_specs=pl.BlockSpec((CTX_LEN, D), lambda b: (b, 0)),
        out_shape=jax.ShapeDtypeStruct((T_CTX, D), BF16),
        compiler_params=_params("parallel"),
        name="na_ctx",
    )(qkv, qkv, qkv)


MLA_QK = 2 * HEAD_DIM
MLA_QBLOCK = 256
MLA_LOOKAHEAD = 1
MLA_SUB = 4
MLA_ZCOLS = MLA_Q_RANK + MLA_KV_RANK + 2 * HEAD_DIM
MLA_UP_TM = 512


def _rms(z, g):
    return (z * lax.rsqrt(jnp.mean(z * z, axis=-1, keepdims=True) + EPS)) * g


def _mla_up_kernel(z_ref, qn_ref, kvn_ref, cos_ref, sin_ref, wq_ref, wqs_ref, wkv_ref, q_ref, kv_ref, kpe_ref):
    cq = _rms(z_ref[:, :MLA_Q_RANK], qn_ref[...]).astype(BF16)
    ckv = _rms(z_ref[:, MLA_Q_RANK:MLA_Q_RANK + MLA_KV_RANK], kvn_ref[...]).astype(BF16)
    c0 = MLA_Q_RANK + MLA_KV_RANK
    cos = cos_ref[...]
    sin = sin_ref[...]
    kpe_ref[...] = (z_ref[:, c0:c0 + HEAD_DIM] * cos + z_ref[:, c0 + HEAD_DIM:] * sin).astype(BF16)
    for h in range(HEADS):
        q = jnp.dot(cq, wq_ref[h], preferred_element_type=F32)
        qs = jnp.dot(cq, wqs_ref[h], preferred_element_type=F32)
        q_ref[h, :, :HEAD_DIM] = q[:, :HEAD_DIM].astype(BF16)
        q_ref[h, :, HEAD_DIM:] = (q[:, HEAD_DIM:] * cos + qs * sin).astype(BF16)
        kv_ref[h] = jnp.dot(ckv, wkv_ref[h], preferred_element_type=F32).astype(BF16)


def _mla_up(z, q_norm, kv_norm, cos, sin, wq, wqs, wkv):
    tm = MLA_UP_TM
    return pl.pallas_call(
        _mla_up_kernel,
        grid=(T_ALL // tm,),
        in_specs=[
            pl.BlockSpec((tm, MLA_ZCOLS), lambda m: (m, 0)),
            pl.BlockSpec((1, MLA_Q_RANK), lambda m: (0, 0)),
            pl.BlockSpec((1, MLA_KV_RANK), lambda m: (0, 0)),
            pl.BlockSpec((tm, HEAD_DIM), lambda m: (m, 0)),
            pl.BlockSpec((tm, HEAD_DIM), lambda m: (m, 0)),
            _resident((HEADS, MLA_Q_RANK, MLA_QK), lambda m: (0, 0, 0)),
            _resident((HEADS, MLA_Q_RANK, HEAD_DIM), lambda m: (0, 0, 0)),
            _resident((HEADS, MLA_KV_RANK, 2 * HEAD_DIM), lambda m: (0, 0, 0)),
        ],
        out_specs=[
            pl.BlockSpec((HEADS, tm, MLA_QK), lambda m: (0, m, 0)),
            pl.BlockSpec((HEADS, tm, 2 * HEAD_DIM), lambda m: (0, m, 0)),
            pl.BlockSpec((tm, HEAD_DIM), lambda m: (m, 0)),
        ],
        out_shape=[
            jax.ShapeDtypeStruct((HEADS, T_ALL, MLA_QK), BF16),
            jax.ShapeDtypeStruct((HEADS, T_ALL, 2 * HEAD_DIM), BF16),
            jax.ShapeDtypeStruct((T_ALL, HEAD_DIM), BF16),
        ],
        compiler_params=_params("parallel"),
        name="mla_up",
    )(z, q_norm.reshape(1, -1), kv_norm.reshape(1, -1), cos, sin, wq, wqs, wkv)


MLA_SCALE = np.float32((MLA_NOPE + MLA_ROPE) ** -0.5)
MLA_SCALE_LOG2E = np.float32((MLA_NOPE + MLA_ROPE) ** -0.5 * np.log2(np.e))


def _mla_lat_kernel(q_ref, kn_ref, v_ref, kpe_ref, knc_ref, vc_ref, kpec_ref, o_ref, kcat_ref):
    @pl.when(pl.program_id(2) == 0)
    def _():
        kcat_ref[:SEQ, :HEAD_DIM] = kn_ref[...]
        kcat_ref[:SEQ, HEAD_DIM:] = kpe_ref[...]
        kcat_ref[SEQ:, :HEAD_DIM] = knc_ref[...]
        kcat_ref[SEQ:, HEAD_DIM:] = kpec_ref[...]

    def scores(sb):
        return _nt_dot(q_ref[sb * MLA_QBLOCK:(sb + 1) * MLA_QBLOCK, :], kcat_ref[...])

    ahead = [scores(sb) for sb in range(min(MLA_LOOKAHEAD, MLA_SUB))]
    for sb in range(MLA_SUB):
        rows = slice(sb * MLA_QBLOCK, (sb + 1) * MLA_QBLOCK)
        s = ahead.pop(0)
        if sb + MLA_LOOKAHEAD < MLA_SUB:
            ahead.append(scores(sb + MLA_LOOKAHEAD))
        e = jnp.exp2((s - jnp.max(s, axis=-1, keepdims=True)) * MLA_SCALE_LOG2E)
        o = (jnp.dot(e[:, :SEQ].astype(BF16), v_ref[...], preferred_element_type=F32)
             + jnp.dot(e[:, SEQ:].astype(BF16), vc_ref[...], preferred_element_type=F32))
        o_ref[rows, :] = (o / jnp.sum(e, axis=-1, keepdims=True)).astype(o_ref.dtype)


def _mla_lat_attention(q, kv, kpe):
    qb = MLA_SUB * MLA_QBLOCK
    blocks = SEQ // qb
    ctx_q = CTX_ROW_BLOCK0
    return pl.pallas_call(
        _mla_lat_kernel,
        grid=(BATCH, HEADS, blocks),
        in_specs=[
            pl.BlockSpec((None, qb, MLA_QK), lambda b, h, i: (h, b * blocks + i, 0)),
            pl.BlockSpec((None, SEQ, HEAD_DIM), lambda b, h, i: (h, b, 0)),
            pl.BlockSpec((None, SEQ, HEAD_DIM), lambda b, h, i: (h, b, 1)),
            pl.BlockSpec((SEQ, HEAD_DIM), lambda b, h, i: (b, 0)),
            pl.BlockSpec((None, CTX_LEN, HEAD_DIM), lambda b, h, i: (h, ctx_q + b, 0)),
            pl.BlockSpec((None, CTX_LEN, HEAD_DIM), lambda b, h, i: (h, ctx_q + b, 1)),
            pl.BlockSpec((CTX_LEN, HEAD_DIM), lambda b, h, i: (ctx_q + b, 0)),
        ],
        out_specs=pl.BlockSpec((qb, HEAD_DIM), lambda b, h, i: (b * blocks + i, h)),
        out_shape=jax.ShapeDtypeStruct((T_LAT, D), BF16),
        scratch_shapes=[pltpu.VMEM((SEQ + CTX_LEN, MLA_QK), BF16)],
        compiler_params=_params("parallel", "parallel", "arbitrary"),
        name="mla_lat",
    )(q, kv, kv, kpe, kv, kv, kpe)


def _mla_ctx_kernel(q_ref, kv_ref, kpe_ref, o_ref):
    kpe = kpe_ref[...]
    for h in range(HEADS):
        kcat = jnp.concatenate([kv_ref[h, :, :HEAD_DIM], kpe], axis=1)
        s = _nt_dot(q_ref[h], kcat) * MLA_SCALE
        o_ref[:, h * HEAD_DIM:(h + 1) * HEAD_DIM] = _softmax_pv(s, kv_ref[h, :, HEAD_DIM:]).astype(o_ref.dtype)


def _mla_ctx_attention(q, kv, kpe):
    ctx_q = CTX_ROW_BLOCK0
    return pl.pallas_call(
        _mla_ctx_kernel,
        grid=(BATCH,),
        in_specs=[
            pl.BlockSpec((HEADS, CTX_LEN, MLA_QK), lambda b: (0, ctx_q + b, 0)),
            pl.BlockSpec((HEADS, CTX_LEN, 2 * HEAD_DIM), lambda b: (0, ctx_q + b, 0)),
            pl.BlockSpec((CTX_LEN, HEAD_DIM), lambda b: (ctx_q + b, 0)),
        ],
        out_specs=pl.BlockSpec((CTX_LEN, D), lambda b: (b, 0)),
        out_shape=jax.ShapeDtypeStruct((T_CTX, D), BF16),
        compiler_params=_params("parallel"),
        name="mla_ctx",
    )(q, kv, kpe)


def _rope_tables():
    half = MLA_ROPE // 2
    freqs = ROPE_THETA ** (-jnp.arange(0, half, 2, dtype=F32) / half)
    t = jnp.arange(SEQ)
    rows = (t // GRID_W).astype(F32)[:, None] * freqs
    cols = (t % GRID_W).astype(F32)[:, None] * freqs
    cr, sr, cc, sn = jnp.cos(rows), jnp.sin(rows), jnp.cos(cols), jnp.sin(cols)
    pad = jnp.zeros((SEQ, HEAD_DIM - MLA_ROPE), F32)
    cos = jnp.concatenate([cr, cr, cc, cc, pad], axis=1)
    sin = jnp.concatenate([-sr, sr, -sn, sn, pad], axis=1)
    cos_c = jnp.concatenate([jnp.ones((T_CTX, MLA_ROPE), F32), jnp.zeros((T_CTX, HEAD_DIM - MLA_ROPE), F32)], axis=1)
    return (jnp.concatenate([jnp.tile(cos, (BATCH, 1)), cos_c], axis=0),
            jnp.concatenate([jnp.tile(sin, (BATCH, 1)), jnp.zeros((T_CTX, HEAD_DIM), F32)], axis=0))


def _swap_rope_halves(w):
    q = MLA_ROPE // 4
    return jnp.concatenate([w[..., q:2 * q], w[..., :q], w[..., 3 * q:], w[..., 2 * q:3 * q]], axis=-1)


def _sgu_kernel(u_ref, v_ref, g_ref, b_ref, ws_ref, bs_ref, o_ref):
    v = v_ref[...]
    mu = jnp.mean(v, axis=-1, keepdims=True)
    vc = v - mu
    var = jnp.mean(vc * vc, axis=-1, keepdims=True)
    vn = ((vc * lax.rsqrt(var + EPS)) * g_ref[...] + b_ref[...]).astype(BF16)
    for g in range(SG_GROUPS):
        cols = slice(g * CHUNK, (g + 1) * CHUNK)
        mix = jnp.dot(ws_ref[g], vn[:, cols], preferred_element_type=F32) + bs_ref[:, cols]
        o_ref[:, cols] = (u_ref[:, cols] * mix).astype(o_ref.dtype)


def _sgu(uv, ln_g, ln_b, ws, bs_full):
    rows = uv.shape[0]
    return pl.pallas_call(
        _sgu_kernel,
        grid=(rows // CHUNK,),
        in_specs=[
            pl.BlockSpec((CHUNK, D), lambda m: (m, 0)),
            pl.BlockSpec((CHUNK, D), lambda m: (m, 1)),
            pl.BlockSpec((1, D), lambda m: (0, 0)),
            pl.BlockSpec((1, D), lambda m: (0, 0)),
            pl.BlockSpec((SG_GROUPS, CHUNK, CHUNK), lambda m: (0, 0, 0)),
            pl.BlockSpec((CHUNK, D), lambda m: (0, 0)),
        ],
        out_specs=pl.BlockSpec((CHUNK, D), lambda m: (m, 0)),
        out_shape=jax.ShapeDtypeStruct((rows, D), BF16),
        compiler_params=_params("parallel"),
        name="sgu",
    )(uv, uv, ln_g.reshape(1, D), ln_b.reshape(1, D), ws, bs_full)


def kernel(x, c, ctx, c_ctx, ada_w, ada_b, norm_g, final_g, a_w_qkv, a_w_o, a_rpb, b_w_in, b_q_norm, b_kv_norm,
           b_w_uq, b_w_ukv, b_w_o, c_w_in, c_b_in, c_ln_g, c_ln_b, c_ws, c_bs, c_w_o, f_w_in, f_conv_w, f_conv_b,
           f_w_out):
    xs = jnp.concatenate([x.reshape(T_LAT, D), ctx.reshape(T_CTX, D)], axis=0)
    cond = jnp.concatenate([c, c_ctx[None, :], jnp.zeros((8 - BATCH - 1, D), F32)], axis=0)
    mods_all = _ada(cond, ada_w, ada_b).reshape(DEPTH, 8, 1, 6 * D)
    norm_rows = norm_g.reshape(DEPTH * 2, 1, D)
    na_bias = _na_bias_table(a_rpb.reshape(-1, 2 * WIN_H - 1, 2 * WIN_W - 1))

    a_w_qkv_b, a_w_o_b = a_w_qkv.astype(BF16), a_w_o.astype(BF16)
    b_w_o_b, c_w_in_b, c_w_o_b = b_w_o.astype(BF16), c_w_in.astype(BF16), c_w_o.astype(BF16)
    f_w_in_b, f_w_out_b = f_w_in.astype(BF16), f_w_out.astype(BF16)

    for i in range(DEPTH):
        kind, j = i % N_MIXERS, i // N_MIXERS
        last = i == DEPTH - 1
        mods = mods_all[i]
        rows_out = T_LAT if last else T_ALL

        if kind == 0:
            qkv = _mod_matmul(xs, norm_rows, 2 * i, mods, 0, a_w_qkv_b, j, tm=PROJ_TM, tn=PROJ_TN, out_dtype=BF16,
                              name="na_qkv")
            att = _na_lat_attention(qkv, na_bias, j)
            att_ctx = None if last else _na_ctx_attention(qkv)
            w_o = a_w_o_b
        elif kind == 1:
            w_in = b_w_in[j]
            c0 = MLA_Q_RANK + MLA_KV_RANK
            zpad = jnp.zeros((D, HEAD_DIM - MLA_ROPE), F32)
            w_in_ext = jnp.concatenate([w_in, zpad, _swap_rope_halves(w_in[:, c0:]), zpad], axis=1)
            z = _mod_matmul(xs, norm_rows, 2 * i, mods, 0, w_in_ext.astype(BF16)[None], 0, tm=PROJ_TM,
                            tn=MLA_ZCOLS // 2, out_dtype=F32, name="mla_in")
            wq = b_w_uq[j].reshape(MLA_Q_RANK, HEADS, MLA_NOPE + MLA_ROPE).transpose(1, 0, 2)
            hpad = jnp.zeros((HEADS, MLA_Q_RANK, HEAD_DIM - MLA_ROPE), F32)
            wq_cat = jnp.concatenate([wq, hpad], axis=-1)
            wq_swap = jnp.concatenate([_swap_rope_halves(wq[..., MLA_NOPE:]), hpad], axis=-1)
            wkv = b_w_ukv[j].reshape(MLA_KV_RANK, HEADS, 2 * HEAD_DIM).transpose(1, 0, 2)
            cos, sin = _rope_tables()
            q, kv, kpe = _mla_up(z, b_q_norm[j], b_kv_norm[j], cos, sin, wq_cat.astype(BF16), wq_swap.astype(BF16),
                                 wkv.astype(BF16))
            att = _mla_lat_attention(q, kv, kpe)
            att_ctx = _mla_ctx_attention(q, kv, kpe)
            w_o = b_w_o_b
        else:
            uv = _mod_matmul(xs, norm_rows, 2 * i, mods, 0, c_w_in_b, j, tm=PROJ_TM, tn=PROJ_TN, out_dtype=F32,
                             bias=c_b_in, act="gelu", name="sgu_in")
            bs_full = jnp.repeat(c_bs[j].T, CHUNK, axis=1)
            att = _sgu(uv, c_ln_g[j], c_ln_b[j], c_ws[j].astype(BF16), bs_full)
            att_ctx = None if last else att
            w_o = c_w_o_b

        xs = _mixer_out(att, att_ctx, w_o, j, xs, mods)
        xs = _ffn(xs, norm_rows, 2 * i + 1, mods, f_w_in_b, f_conv_w, f_conv_b, f_w_out_b, i, final_g, rows=rows_out,
                  final_norm=last)

    return xs.reshape(BATCH, SEQ, D)
```

```python
import functools

import jax
import jax.numpy as jnp
import numpy as np
from jax import lax
from jax.experimental import pallas as pl
from jax.experimental.pallas import tpu as pltpu

F32 = jnp.float32
BF16 = jnp.bfloat16

D = 2048
BATCH = 4
SEQ = 2048
DEPTH = 4
GRID_W = 64
GRID_H = SEQ // GRID_W
CTX_LEN = 256
N_MIXERS = 3
EPS = 1e-6
ROPE_THETA = 10000.0
HEADS = 16
HEAD_DIM = 128
WIN_H = 8
WIN_W = 16
MLA_NOPE = 128
MLA_ROPE = 64
MLA_Q_RANK = 512
MLA_KV_RANK = 512
CHUNK = 128
SG_GROUPS = 16
D_FF = 5632
CONV_W = 3

T_LAT = BATCH * SEQ
T_CTX = BATCH * CTX_LEN
T_ALL = T_LAT + T_CTX
CTX_ROW_BLOCK0 = T_LAT // CTX_LEN
MASK_VALUE = -1e30
VMEM_LIMIT = 56 * 1024 * 1024
LANES = 128
BF16_ROWS = 16
PROLOGUE_ROWS = 128
PROJ_TM = 1024
PROJ_TN = 1024


def _params(*semantics):
    return pltpu.CompilerParams(dimension_semantics=semantics, vmem_limit_bytes=VMEM_LIMIT)


def _mod_row(tm):
    per_batch = SEQ // tm
    return lambda m: jnp.minimum(m // per_batch, BATCH)


def _resident(block_shape, index_map):
    return pl.BlockSpec(block_shape, index_map, pipeline_mode=pl.Buffered(1))


def _ada_kernel(s_ref, w_ref, b_ref, o_ref):
    s = s_ref[...]
    s = s * jax.nn.sigmoid(s)
    o_ref[...] = jnp.dot(s.astype(BF16), w_ref[...].astype(BF16), preferred_element_type=F32) + b_ref[...]


def _ada(s, ada_w, ada_b):
    tn = 1024
    n6 = 6 * D
    return pl.pallas_call(
        _ada_kernel,
        grid=(DEPTH, n6 // tn),
        in_specs=[
            pl.BlockSpec((8, D), lambda i, n: (0, 0)),
            pl.BlockSpec((None, D, tn), lambda i, n: (i, 0, n)),
            pl.BlockSpec((None, 1, tn), lambda i, n: (i, 0, n)),
        ],
        out_specs=pl.BlockSpec((None, 8, tn), lambda i, n: (i, 0, n)),
        out_shape=jax.ShapeDtypeStruct((DEPTH, 8, n6), F32),
        compiler_params=_params("parallel", "parallel"),
        name="ada",
    )(s, ada_w, ada_b.reshape(DEPTH, 1, n6))


def _gelu_exact(z):
    return 0.5 * z * (1.0 + lax.erf(z * np.float32(np.sqrt(0.5))))


def _modulate(xf, g, shift, scale):
    y = xf * lax.rsqrt(jnp.mean(xf * xf, axis=-1, keepdims=True) + EPS)
    return y * (g * (1.0 + scale)) + shift


def _modulate_rows(x_ref, h_ref, h_row0, n_rows, g, shift, scale):
    def body(c, carry):
        r = pl.multiple_of(c * PROLOGUE_ROWS, PROLOGUE_ROWS)
        h = _modulate(x_ref[pl.ds(r, PROLOGUE_ROWS), :], g, shift, scale)
        h_ref[pl.ds(h_row0 + r, PROLOGUE_ROWS), :] = h.astype(BF16)
        return carry
    lax.fori_loop(0, n_rows // PROLOGUE_ROWS, body, 0)


def _mod_mm_kernel(x_ref, g_ref, sh_ref, sc_ref, w_ref, *rest, has_bias, act, tm):
    if has_bias:
        b_ref, o_ref, h_ref = rest
    else:
        o_ref, h_ref = rest

    @pl.when(pl.program_id(1) == 0)
    def _():
        _modulate_rows(x_ref, h_ref, 0, tm, g_ref[...], sh_ref[...], sc_ref[...])

    acc = jnp.dot(h_ref[...], w_ref[...], preferred_element_type=F32)
    if has_bias:
        acc = acc + b_ref[...]
    if act == "gelu":
        acc = _gelu_exact(acc)
    o_ref[...] = acc.astype(o_ref.dtype)


def _mod_matmul(x, norm_g, g_row, mods, shift_col, w, w_layer, *, tm, tn, out_dtype, bias=None, act=None, name):
    n = w.shape[2]
    brow = _mod_row(tm)
    in_specs = [
        pl.BlockSpec((tm, D), lambda m, j: (m, 0)),
        pl.BlockSpec((None, 1, D), lambda m, j: (g_row, 0, 0)),
        pl.BlockSpec((None, 1, D), lambda m, j: (brow(m), 0, shift_col)),
        pl.BlockSpec((None, 1, D), lambda m, j: (brow(m), 0, shift_col + 1)),
        pl.BlockSpec((None, D, tn), lambda m, j: (w_layer, 0, j)),
    ]
    args = [x, norm_g, mods, mods, w]
    if bias is not None:
        in_specs.append(pl.BlockSpec((None, 1, tn), lambda m, j: (w_layer, 0, j)))
        args.append(bias.reshape(bias.shape[0], 1, n))
    return pl.pallas_call(
        functools.partial(_mod_mm_kernel, has_bias=bias is not None, act=act, tm=tm),
        grid=(T_ALL // tm, n // tn),
        in_specs=in_specs,
        out_specs=pl.BlockSpec((tm, tn), lambda m, j: (m, j)),
        out_shape=jax.ShapeDtypeStruct((T_ALL, n), out_dtype),
        scratch_shapes=[pltpu.VMEM((tm, D), BF16)],
        compiler_params=_params("parallel", "arbitrary"),
        name=name,
    )(*args)


MIXER_OUT_COLS = 512


MIXER_OUT_TM = 512
MIXER_LAT_TILES = T_LAT // MIXER_OUT_TM


def _mixer_out_kernel(*refs, has_ctx):
    if has_ctx:
        a_ref, ac_ref, w_ref, r_ref, gate_ref, o_ref = refs
        a = jnp.where(pl.program_id(0) < MIXER_LAT_TILES, a_ref[...], ac_ref[...])
    else:
        a_ref, w_ref, r_ref, gate_ref, o_ref = refs
        a = a_ref[...]
    for c in range(D // MIXER_OUT_COLS):
        cols = slice(c * MIXER_OUT_COLS, (c + 1) * MIXER_OUT_COLS)
        acc = jnp.dot(a, w_ref[:, cols], preferred_element_type=F32)
        o_ref[:, cols] = r_ref[:, cols] + gate_ref[:, cols] * acc


def _mixer_out(a_lat, a_ctx, w, w_layer, resid, mods):
    tm = MIXER_OUT_TM
    brow = _mod_row(tm)
    has_ctx = a_ctx is not None
    rows = T_ALL if has_ctx else T_LAT
    in_specs = [pl.BlockSpec((tm, D), lambda m: (jnp.minimum(m, MIXER_LAT_TILES - 1), 0))]
    args = [a_lat]
    if has_ctx:
        ctx_tile0 = (a_ctx.shape[0] - T_CTX) // tm
        in_specs.append(pl.BlockSpec((tm, D), lambda m: (jnp.maximum(m - MIXER_LAT_TILES, 0) + ctx_tile0, 0)))
        args.append(a_ctx)
    in_specs += [
        _resident((None, D, D), lambda m: (w_layer, 0, 0)),
        pl.BlockSpec((tm, D), lambda m: (m, 0)),
        pl.BlockSpec((None, 1, D), lambda m: (brow(m), 0, 2)),
    ]
    return pl.pallas_call(
        functools.partial(_mixer_out_kernel, has_ctx=has_ctx),
        grid=(rows // tm,),
        in_specs=in_specs,
        out_specs=pl.BlockSpec((tm, D), lambda m: (m, 0)),
        out_shape=jax.ShapeDtypeStruct((rows, D), F32),
        compiler_params=_params("parallel"),
        name="mixer_out",
    )(*args, w, resid, mods)


FFN_TM = 1024
FFN_TF = 512
FFN_HALO = BF16_ROWS
FFN_OUT_COLS = 512
FFN_ROW_CHUNK = CTX_LEN
FFN_LOOKAHEAD = 4
FFN_ROW_SETS = 4
FFN_SLABS = FFN_TF // LANES


def _ffn_kernel(x_ref, xp_ref, xn_ref, g_ref, sh_ref, sc_ref, gate_ref, wa_ref, wg_ref, cwa_ref, cwg_ref,
                cba_ref, cbg_ref, wo_ref, fg_ref, o_ref, h_ref, ua_ref, ug_ref, act_ref, *, final_norm):
    tm, halo = FFN_TM, FFN_HALO
    f = pl.program_id(1)

    @pl.when(f == 0)
    def _():
        g, sh, sc = g_ref[...], sh_ref[...], sc_ref[...]
        h_ref[:halo, :] = _modulate(xp_ref[...], g, sh, sc).astype(BF16)
        h_ref[halo + tm:, :] = _modulate(xn_ref[...], g, sh, sc).astype(BF16)
        _modulate_rows(x_ref, h_ref, halo, tm, g, sh, sc)
        o_ref[...] = jnp.zeros_like(o_ref)

    ch, n_ch, sets = FFN_ROW_CHUNK, tm // FFN_ROW_CHUNK, FFN_ROW_SETS
    set_rows = ch // sets
    sub = lax.broadcasted_iota(jnp.int32, (set_rows, LANES), 0)

    def hidden(r):
        lo = 0 if r == 0 else halo + r * ch
        hi = tm + 2 * halo if r == n_ch - 1 else halo + (r + 1) * ch
        h = h_ref[lo:hi, :]
        ua = jnp.dot(h, wa_ref[...], preferred_element_type=F32)
        ug = jnp.dot(h, wg_ref[...], preferred_element_type=F32)
        for s in range(FFN_SLABS):
            ua_ref[s, lo:hi, :] = ua[:, s * LANES:(s + 1) * LANES]
            ug_ref[s, lo:hi, :] = ug[:, s * LANES:(s + 1) * LANES]

    def gate_and_project(r):
        base = halo + r * ch
        rows = slice(r * ch, (r + 1) * ch)
        row0 = pl.program_id(0) * tm + r * ch
        seq_mask = jnp.where(row0 < T_LAT, SEQ - 1, CTX_LEN - 1)
        keep_first_up = jnp.where((row0 & seq_mask) == 0, 0.0, 1.0).astype(F32)
        keep_last_dn = jnp.where(((row0 + ch) & seq_mask) == 0, 0.0, 1.0).astype(F32)
        first_up = jnp.where(sub == 0, keep_first_up, 1.0)
        last_dn = jnp.where(sub == set_rows - 1, keep_last_dn, 1.0)

        for s in range(FFN_SLABS):
            lanes = slice(s * LANES, (s + 1) * LANES)
            for j in range(sets):
                def conv(u_ref, cw_ref, cb_ref):
                    up = u_ref[s, pl.ds(base + j - 1, set_rows, stride=sets), :]
                    cur = u_ref[s, pl.ds(base + j, set_rows, stride=sets), :]
                    dn = u_ref[s, pl.ds(base + j + 1, set_rows, stride=sets), :]
                    if j == 0:
                        up = up * first_up
                    if j == sets - 1:
                        dn = dn * last_dn
                    return (up * cw_ref[0:1, lanes] + cur * cw_ref[1:2, lanes] + dn * cw_ref[2:3, lanes]
                            + cb_ref[:, lanes])

                a = conv(ua_ref, cwa_ref, cba_ref)
                gg = conv(ug_ref, cwg_ref, cbg_ref)
                act_ref[s, pl.ds(r * ch + j, set_rows, stride=sets), :] = a * (gg * jax.nn.sigmoid(gg))

        act = jnp.concatenate([act_ref[s, rows, :] for s in range(FFN_SLABS)], axis=1).astype(BF16)
        for c in range(D // FFN_OUT_COLS):
            cols = slice(c * FFN_OUT_COLS, (c + 1) * FFN_OUT_COLS)
            o_ref[rows, cols] += jnp.dot(act, wo_ref[:, cols], preferred_element_type=F32)

    for r in range(min(FFN_LOOKAHEAD, n_ch)):
        hidden(r)
    for r in range(n_ch):
        if r + FFN_LOOKAHEAD < n_ch:
            hidden(r + FFN_LOOKAHEAD)
        gate_and_project(r)

    @pl.when(f == pl.num_programs(1) - 1)
    def _():
        gate = gate_ref[...]

        def body(c, carry):
            rows = pl.ds(pl.multiple_of(c * PROLOGUE_ROWS, PROLOGUE_ROWS), PROLOGUE_ROWS)
            y = x_ref[rows, :] + gate * o_ref[rows, :]
            if final_norm:
                y = (y * lax.rsqrt(jnp.mean(y * y, axis=-1, keepdims=True) + EPS)) * fg_ref[...]
            o_ref[rows, :] = y
            return carry
        lax.fori_loop(0, tm // PROLOGUE_ROWS, body, 0)


def _ffn(x, norm_g, g_row, mods, w_in, conv_w, conv_b, w_out, layer, final_g, *, rows, final_norm):
    tm, tf, halo = FFN_TM, FFN_TF, FFN_HALO
    nf = D_FF // tf
    brow = _mod_row(tm)
    last_halo = rows // halo - 1
    cb = conv_b.reshape(DEPTH, 1, 2 * D_FF)
    return pl.pallas_call(
        functools.partial(_ffn_kernel, final_norm=final_norm),
        grid=(rows // tm, nf),
        in_specs=[
            pl.BlockSpec((tm, D), lambda m, f: (m, 0), pipeline_mode=pl.Buffered(1)),
            pl.BlockSpec((halo, D), lambda m, f: (jnp.maximum(m * (tm // halo) - 1, 0), 0)),
            pl.BlockSpec((halo, D), lambda m, f: (jnp.minimum((m + 1) * (tm // halo), last_halo), 0)),
            pl.BlockSpec((None, 1, D), lambda m, f: (g_row, 0, 0)),
            pl.BlockSpec((None, 1, D), lambda m, f: (brow(m), 0, 3)),
            pl.BlockSpec((None, 1, D), lambda m, f: (brow(m), 0, 4)),
            pl.BlockSpec((None, 1, D), lambda m, f: (brow(m), 0, 5)),
            pl.BlockSpec((None, D, tf), lambda m, f: (layer, 0, f)),
            pl.BlockSpec((None, D, tf), lambda m, f: (layer, 0, f + nf)),
            pl.BlockSpec((None, CONV_W, tf), lambda m, f: (layer, 0, f)),
            pl.BlockSpec((None, CONV_W, tf), lambda m, f: (layer, 0, f + nf)),
            pl.BlockSpec((None, 1, tf), lambda m, f: (layer, 0, f)),
            pl.BlockSpec((None, 1, tf), lambda m, f: (layer, 0, f + nf)),
            pl.BlockSpec((None, tf, D), lambda m, f: (layer, f, 0)),
            pl.BlockSpec((1, D), lambda m, f: (0, 0)),
        ],
        out_specs=pl.BlockSpec((tm, D), lambda m, f: (m, 0)),
        out_shape=jax.ShapeDtypeStruct((rows, D), F32),
        scratch_shapes=[
            pltpu.VMEM((tm + 2 * halo, D), BF16),
            pltpu.VMEM((FFN_SLABS, tm + 2 * halo, LANES), F32),
            pltpu.VMEM((FFN_SLABS, tm + 2 * halo, LANES), F32),
            pltpu.VMEM((FFN_SLABS, tm, LANES), F32),
        ],
        compiler_params=_params("parallel", "arbitrary"),
        name="ffn",
    )(x, x, x, norm_g, mods, mods, mods, w_in, w_in, conv_w, conv_w, cb, cb, w_out, final_g.reshape(1, D))


NA_QROWS = 4
NA_QBLOCK = NA_QROWS * GRID_W
NA_SUB = 4
NA_BAND_ROWS = 12
NA_BAND = NA_BAND_ROWS * GRID_W


ATTN_LOOKAHEAD = 2


def _nt_dot(a, b):
    return lax.dot_general(a, b, (((1,), (1,)), ((), ())), preferred_element_type=F32)


NA_QBLOCKS = GRID_H // NA_QROWS
NA_SCALE = HEAD_DIM ** -0.5
NA_SCALE_LOG2E = np.float32(NA_SCALE * np.log2(np.e))
NA_KINDS = ((lambda j: 0, WIN_H - 1), (lambda j: j, WIN_H // 2 - 1), (lambda j: NA_BAND_ROWS - WIN_H, -1))


def _na_band_start(blk):
    return jnp.clip(blk * NA_QROWS - WIN_H // 2, 0, GRID_H - NA_BAND_ROWS)


def _na_lat_kernel(q_ref, k_ref, v_ref, kc_ref, vc_ref, bias_ref, o_ref):
    i = pl.program_id(2)
    kc = kc_ref[...]
    vc = vc_ref[...]

    def key_start(sb):
        return pl.multiple_of(_na_band_start(i * NA_SUB + sb) * GRID_W, NA_QROWS * GRID_W)

    def logits(sb):
        blk = i * NA_SUB + sb
        kind = jnp.where(blk == 0, 0, jnp.where(blk == NA_QBLOCKS - 1, 2, 1))
        q = q_ref[sb * NA_QBLOCK:(sb + 1) * NA_QBLOCK, :]
        return _nt_dot(q, k_ref[pl.ds(key_start(sb), NA_BAND), :]) + bias_ref[kind], _nt_dot(q, kc)

    ahead = [logits(sb) for sb in range(min(ATTN_LOOKAHEAD, NA_SUB))]
    for sb in range(NA_SUB):
        t, tc = ahead.pop(0)
        if sb + ATTN_LOOKAHEAD < NA_SUB:
            ahead.append(logits(sb + ATTN_LOOKAHEAD))
        mx = jnp.maximum(jnp.max(t, axis=-1, keepdims=True), jnp.max(tc, axis=-1, keepdims=True))
        e = jnp.exp2((t - mx) * NA_SCALE_LOG2E)
        ec = jnp.exp2((tc - mx) * NA_SCALE_LOG2E)
        denom = jnp.sum(e, axis=-1, keepdims=True) + jnp.sum(ec, axis=-1, keepdims=True)
        o = (jnp.dot(e.astype(BF16), v_ref[pl.ds(key_start(sb), NA_BAND), :], preferred_element_type=F32)
             + jnp.dot(ec.astype(BF16), vc, preferred_element_type=F32))
        o_ref[sb * NA_QBLOCK:(sb + 1) * NA_QBLOCK, :] = (o / denom).astype(o_ref.dtype)


def _na_bias_table(rpb):
    n = rpb.shape[0]
    w = np.arange(GRID_W)[:, None]
    k = np.arange(GRID_W)[None, :]
    col0 = np.clip(w - WIN_W // 2, 0, GRID_W - WIN_W)
    in_win = (k >= col0) & (k < col0 + WIN_W)
    col_pick = (np.arange(2 * WIN_W - 1)[:, None, None] == (k - w + WIN_W - 1)[None]) & in_win[None]
    t = jnp.einsum("nrx,xwk->nrwk", rpb, col_pick.astype(np.float32), precision=lax.Precision.HIGHEST)
    t = jnp.where(jnp.asarray(in_win), t, MASK_VALUE) * np.float32(1.0 / NA_SCALE)
    return pl.pallas_call(
        _na_bias_blocks_kernel,
        grid=(n,),
        in_specs=[pl.BlockSpec((None, 2 * WIN_H - 1, GRID_W, GRID_W), lambda i: (i, 0, 0, 0))],
        out_specs=pl.BlockSpec((None, len(NA_KINDS), NA_QBLOCK, NA_BAND), lambda i: (i, 0, 0, 0)),
        out_shape=jax.ShapeDtypeStruct((n, len(NA_KINDS), NA_QBLOCK, NA_BAND), F32),
        compiler_params=_params("parallel"),
        name="na_bias_blocks",
    )(t)


def _na_bias_blocks_kernel(t_ref, o_ref):
    masked = jnp.full((GRID_W, GRID_W), MASK_VALUE / NA_SCALE, F32)
    for kind, (first_valid, offset0) in enumerate(NA_KINDS):
        for j in range(NA_QROWS):
            lo = first_valid(j)
            for c in range(NA_BAND_ROWS):
                tile = t_ref[c - j + offset0] if lo <= c < lo + WIN_H else masked
                o_ref[kind, j * GRID_W:(j + 1) * GRID_W, c * GRID_W:(c + 1) * GRID_W] = tile


def _na_lat_attention(qkv, bias_tab, layer):
    blocks = SEQ // (NA_SUB * NA_QBLOCK)
    qb = NA_SUB * NA_QBLOCK
    return pl.pallas_call(
        _na_lat_kernel,
        grid=(HEADS, BATCH, blocks),
        in_specs=[
            pl.BlockSpec((qb, HEAD_DIM), lambda h, b, i: (b * blocks + i, h)),
            pl.BlockSpec((SEQ, HEAD_DIM), lambda h, b, i: (b, HEADS + h)),
            pl.BlockSpec((SEQ, HEAD_DIM), lambda h, b, i: (b, 2 * HEADS + h)),
            pl.BlockSpec((CTX_LEN, HEAD_DIM), lambda h, b, i: (CTX_ROW_BLOCK0 + b, HEADS + h)),
            pl.BlockSpec((CTX_LEN, HEAD_DIM), lambda h, b, i: (CTX_ROW_BLOCK0 + b, 2 * HEADS + h)),
            pl.BlockSpec((None, len(NA_KINDS), NA_QBLOCK, NA_BAND), lambda h, b, i: (layer * HEADS + h, 0, 0, 0)),
        ],
        out_specs=pl.BlockSpec((qb, HEAD_DIM), lambda h, b, i: (b * blocks + i, h)),
        out_shape=jax.ShapeDtypeStruct((T_LAT, D), BF16),
        compiler_params=_params("parallel", "parallel", "arbitrary"),
        name="na_lat",
    )(qkv, qkv, qkv, qkv, qkv, bias_tab)


def _softmax_pv(s, v):
    e = jnp.exp(s - jnp.max(s, axis=-1, keepdims=True))
    o = jnp.dot(e.astype(BF16), v, preferred_element_type=F32)
    return o / jnp.sum(e, axis=-1, keepdims=True)


def _na_ctx_kernel(q_ref, k_ref, v_ref, o_ref):
    scale = np.float32(HEAD_DIM ** -0.5)
    for h in range(HEADS):
        cols = slice(h * HEAD_DIM, (h + 1) * HEAD_DIM)
        s = _nt_dot(q_ref[:, cols], k_ref[:, cols]) * scale
        o_ref[:, cols] = _softmax_pv(s, v_ref[:, cols]).astype(o_ref.dtype)


def _na_ctx_attention(qkv):
    return pl.pallas_call(
        _na_ctx_kernel,
        grid=(BATCH,),
        in_specs=[
            pl.BlockSpec((CTX_LEN, D), lambda b: (CTX_ROW_BLOCK0 + b, 0)),
            pl.BlockSpec((CTX_LEN, D), lambda b: (CTX_ROW_BLOCK0 + b, 1)),
            pl.BlockSpec((CTX_LEN, D), lambda b: (CTX_ROW_BLOCK0 + b, 2)),
        ],
        out_specs=pl.BlockSpec((CTX_LEN, D), lambda b: (b, 0)),
        out_shape=jax.ShapeDtypeStruct((T_CTX, D), BF16),
        compiler_params=_params("parallel"),
        name="na_ctx",
    )(qkv, qkv, qkv)


MLA_QK = 2 * HEAD_DIM
MLA_QBLOCK = 256
MLA_LOOKAHEAD = 1
MLA_SUB = 4
MLA_ZCOLS = MLA_Q_RANK + MLA_KV_RANK + 2 * HEAD_DIM
MLA_UP_TM = 512


def _rms(z, g):
    return (z * lax.rsqrt(jnp.mean(z * z, axis=-1, keepdims=True) + EPS)) * g


def _mla_up_kernel(z_ref, qn_ref, kvn_ref, cos_ref, sin_ref, wq_ref, wqs_ref, wkv_ref, q_ref, kv_ref, kpe_ref):
    cq = _rms(z_ref[:, :MLA_Q_RANK], qn_ref[...]).astype(BF16)
    ckv = _rms(z_ref[:, MLA_Q_RANK:MLA_Q_RANK + MLA_KV_RANK], kvn_ref[...]).astype(BF16)
    c0 = MLA_Q_RANK + MLA_KV_RANK
    cos = cos_ref[...]
    sin = sin_ref[...]
    kpe_ref[...] = (z_ref[:, c0:c0 + HEAD_DIM] * cos + z_ref[:, c0 + HEAD_DIM:] * sin).astype(BF16)
    for h in range(HEADS):
        q = jnp.dot(cq, wq_ref[h], preferred_element_type=F32)
        qs = jnp.dot(cq, wqs_ref[h], preferred_element_type=F32)
        q_ref[h, :, :HEAD_DIM] = q[:, :HEAD_DIM].astype(BF16)
        q_ref[h, :, HEAD_DIM:] = (q[:, HEAD_DIM:] * cos + qs * sin).astype(BF16)
        kv_ref[h] = jnp.dot(ckv, wkv_ref[h], preferred_element_type=F32).astype(BF16)


def _mla_up(z, q_norm, kv_norm, cos, sin, wq, wqs, wkv):
    tm = MLA_UP_TM
    return pl.pallas_call(
        _mla_up_kernel,
        grid=(T_ALL // tm,),
        in_specs=[
            pl.BlockSpec((tm, MLA_ZCOLS), lambda m: (m, 0)),
            pl.BlockSpec((1, MLA_Q_RANK), lambda m: (0, 0)),
            pl.BlockSpec((1, MLA_KV_RANK), lambda m: (0, 0)),
            pl.BlockSpec((tm, HEAD_DIM), lambda m: (m, 0)),
            pl.BlockSpec((tm, HEAD_DIM), lambda m: (m, 0)),
            _resident((HEADS, MLA_Q_RANK, MLA_QK), lambda m: (0, 0, 0)),
            _resident((HEADS, MLA_Q_RANK, HEAD_DIM), lambda m: (0, 0, 0)),
            _resident((HEADS, MLA_KV_RANK, 2 * HEAD_DIM), lambda m: (0, 0, 0)),
        ],
        out_specs=[
            pl.BlockSpec((HEADS, tm, MLA_QK), lambda m: (0, m, 0)),
            pl.BlockSpec((HEADS, tm, 2 * HEAD_DIM), lambda m: (0, m, 0)),
            pl.BlockSpec((tm, HEAD_DIM), lambda m: (m, 0)),
        ],
        out_shape=[
            jax.ShapeDtypeStruct((HEADS, T_ALL, MLA_QK), BF16),
            jax.ShapeDtypeStruct((HEADS, T_ALL, 2 * HEAD_DIM), BF16),
            jax.ShapeDtypeStruct((T_ALL, HEAD_DIM), BF16),
        ],
        compiler_params=_params("parallel"),
        name="mla_up",
    )(z, q_norm.reshape(1, -1), kv_norm.reshape(1, -1), cos, sin, wq, wqs, wkv)


MLA_SCALE = np.float32((MLA_NOPE + MLA_ROPE) ** -0.5)
MLA_SCALE_LOG2E = np.float32((MLA_NOPE + MLA_ROPE) ** -0.5 * np.log2(np.e))


def _mla_lat_kernel(q_ref, kn_ref, v_ref, kpe_ref, knc_ref, vc_ref, kpec_ref, o_ref, kcat_ref):
    @pl.when(pl.program_id(2) == 0)
    def _():
        kcat_ref[:SEQ, :HEAD_DIM] = kn_ref[...]
        kcat_ref[:SEQ, HEAD_DIM:] = kpe_ref[...]
        kcat_ref[SEQ:, :HEAD_DIM] = knc_ref[...]
        kcat_ref[SEQ:, HEAD_DIM:] = kpec_ref[...]

    def scores(sb):
        return _nt_dot(q_ref[sb * MLA_QBLOCK:(sb + 1) * MLA_QBLOCK, :], kcat_ref[...])

    ahead = [scores(sb) for sb in range(min(MLA_LOOKAHEAD, MLA_SUB))]
    for sb in range(MLA_SUB):
        rows = slice(sb * MLA_QBLOCK, (sb + 1) * MLA_QBLOCK)
        s = ahead.pop(0)
        if sb + MLA_LOOKAHEAD < MLA_SUB:
            ahead.append(scores(sb + MLA_LOOKAHEAD))
        e = jnp.exp2((s - jnp.max(s, axis=-1, keepdims=True)) * MLA_SCALE_LOG2E)
        o = (jnp.dot(e[:, :SEQ].astype(BF16), v_ref[...], preferred_element_type=F32)
             + jnp.dot(e[:, SEQ:].astype(BF16), vc_ref[...], preferred_element_type=F32))
        o_ref[rows, :] = (o / jnp.sum(e, axis=-1, keepdims=True)).astype(o_ref.dtype)


def _mla_lat_attention(q, kv, kpe):
    qb = MLA_SUB * MLA_QBLOCK
    blocks = SEQ // qb
    ctx_q = CTX_ROW_BLOCK0
    return pl.pallas_call(
        _mla_lat_kernel,
        grid=(BATCH, HEADS, blocks),
        in_specs=[
            pl.BlockSpec((None, qb, MLA_QK), lambda b, h, i: (h, b * blocks + i, 0)),
            pl.BlockSpec((None, SEQ, HEAD_DIM), lambda b, h, i: (h, b, 0)),
            pl.BlockSpec((None, SEQ, HEAD_DIM), lambda b, h, i: (h, b, 1)),
            pl.BlockSpec((SEQ, HEAD_DIM), lambda b, h, i: (b, 0)),
            pl.BlockSpec((None, CTX_LEN, HEAD_DIM), lambda b, h, i: (h, ctx_q + b, 0)),
            pl.BlockSpec((None, CTX_LEN, HEAD_DIM), lambda b, h, i: (h, ctx_q + b, 1)),
            pl.BlockSpec((CTX_LEN, HEAD_DIM), lambda b, h, i: (ctx_q + b, 0)),
        ],
        out_specs=pl.BlockSpec((qb, HEAD_DIM), lambda b, h, i: (b * blocks + i, h)),
        out_shape=jax.ShapeDtypeStruct((T_LAT, D), BF16),
        scratch_shapes=[pltpu.VMEM((SEQ + CTX_LEN, MLA_QK), BF16)],
        compiler_params=_params("parallel", "parallel", "arbitrary"),
        name="mla_lat",
    )(q, kv, kv, kpe, kv, kv, kpe)


def _mla_ctx_kernel(q_ref, kv_ref, kpe_ref, o_ref):
    kpe = kpe_ref[...]
    for h in range(HEADS):
        kcat = jnp.concatenate([kv_ref[h, :, :HEAD_DIM], kpe], axis=1)
        s = _nt_dot(q_ref[h], kcat) * MLA_SCALE
        o_ref[:, h * HEAD_DIM:(h + 1) * HEAD_DIM] = _softmax_pv(s, kv_ref[h, :, HEAD_DIM:]).astype(o_ref.dtype)


def _mla_ctx_attention(q, kv, kpe):
    ctx_q = CTX_ROW_BLOCK0
    return pl.pallas_call(
        _mla_ctx_kernel,
        grid=(BATCH,),
        in_specs=[
            pl.BlockSpec((HEADS, CTX_LEN, MLA_QK), lambda b: (0, ctx_q + b, 0)),
            pl.BlockSpec((HEADS, CTX_LEN, 2 * HEAD_DIM), lambda b: (0, ctx_q + b, 0)),
            pl.BlockSpec((CTX_LEN, HEAD_DIM), lambda b: (ctx_q + b, 0)),
        ],
        out_specs=pl.BlockSpec((CTX_LEN, D), lambda b: (b, 0)),
        out_shape=jax.ShapeDtypeStruct((T_CTX, D), BF16),
        compiler_params=_params("parallel"),
        name="mla_ctx",
    )(q, kv, kpe)


def _rope_tables():
    half = MLA_ROPE // 2
    freqs = ROPE_THETA ** (-jnp.arange(0, half, 2, dtype=F32) / half)
    t = jnp.arange(SEQ)
    rows = (t // GRID_W).astype(F32)[:, None] * freqs
    cols = (t % GRID_W).astype(F32)[:, None] * freqs
    cr, sr, cc, sn = jnp.cos(rows), jnp.sin(rows), jnp.cos(cols), jnp.sin(cols)
    pad = jnp.zeros((SEQ, HEAD_DIM - MLA_ROPE), F32)
    cos = jnp.concatenate([cr, cr, cc, cc, pad], axis=1)
    sin = jnp.concatenate([-sr, sr, -sn, sn, pad], axis=1)
    cos_c = jnp.concatenate([jnp.ones((T_CTX, MLA_ROPE), F32), jnp.zeros((T_CTX, HEAD_DIM - MLA_ROPE), F32)], axis=1)
    return (jnp.concatenate([jnp.tile(cos, (BATCH, 1)), cos_c], axis=0),
            jnp.concatenate([jnp.tile(sin, (BATCH, 1)), jnp.zeros((T_CTX, HEAD_DIM), F32)], axis=0))


def _swap_rope_halves(w):
    q = MLA_ROPE // 4
    return jnp.concatenate([w[..., q:2 * q], w[..., :q], w[..., 3 * q:], w[..., 2 * q:3 * q]], axis=-1)


def _sgu_kernel(u_ref, v_ref, g_ref, b_ref, ws_ref, bs_ref, o_ref):
    v = v_ref[...]
    mu = jnp.mean(v, axis=-1, keepdims=True)
    vc = v - mu
    var = jnp.mean(vc * vc, axis=-1, keepdims=True)
    vn = ((vc * lax.rsqrt(var + EPS)) * g_ref[...] + b_ref[...]).astype(BF16)
    for g in range(SG_GROUPS):
        cols = slice(g * CHUNK, (g + 1) * CHUNK)
        mix = jnp.dot(ws_ref[g], vn[:, cols], preferred_element_type=F32) + bs_ref[:, cols]
        o_ref[:, cols] = (u_ref[:, cols] * mix).astype(o_ref.dtype)


def _sgu(uv, ln_g, ln_b, ws, bs_full):
    rows = uv.shape[0]
    return pl.pallas_call(
        _sgu_kernel,
        grid=(rows // CHUNK,),
        in_specs=[
            pl.BlockSpec((CHUNK, D), lambda m: (m, 0)),
            pl.BlockSpec((CHUNK, D), lambda m: (m, 1)),
            pl.BlockSpec((1, D), lambda m: (0, 0)),
            pl.BlockSpec((1, D), lambda m: (0, 0)),
            pl.BlockSpec((SG_GROUPS, CHUNK, CHUNK), lambda m: (0, 0, 0)),
            pl.BlockSpec((CHUNK, D), lambda m: (0, 0)),
        ],
        out_specs=pl.BlockSpec((CHUNK, D), lambda m: (m, 0)),
        out_shape=jax.ShapeDtypeStruct((rows, D), BF16),
        compiler_params=_params("parallel"),
        name="sgu",
    )(uv, uv, ln_g.reshape(1, D), ln_b.reshape(1, D), ws, bs_full)


def kernel(x, c, ctx, c_ctx, ada_w, ada_b, norm_g, final_g, a_w_qkv, a_w_o, a_rpb, b_w_in, b_q_norm, b_kv_norm,
           b_w_uq, b_w_ukv, b_w_o, c_w_in, c_b_in, c_ln_g, c_ln_b, c_ws, c_bs, c_w_o, f_w_in, f_conv_w, f_conv_b,
           f_w_out):
    xs = jnp.concatenate([x.reshape(T_LAT, D), ctx.reshape(T_CTX, D)], axis=0)
    cond = jnp.concatenate([c, c_ctx[None, :], jnp.zeros((8 - BATCH - 1, D), F32)], axis=0)
    mods_all = _ada(cond, ada_w, ada_b).reshape(DEPTH, 8, 1, 6 * D)
    norm_rows = norm_g.reshape(DEPTH * 2, 1, D)
    na_bias = _na_bias_table(a_rpb.reshape(-1, 2 * WIN_H - 1, 2 * WIN_W - 1))

    a_w_qkv_b, a_w_o_b = a_w_qkv.astype(BF16), a_w_o.astype(BF16)
    b_w_o_b, c_w_in_b, c_w_o_b = b_w_o.astype(BF16), c_w_in.astype(BF16), c_w_o.astype(BF16)
    f_w_in_b, f_w_out_b = f_w_in.astype(BF16), f_w_out.astype(BF16)

    for i in range(DEPTH):
        kind, j = i % N_MIXERS, i // N_MIXERS
        last = i == DEPTH - 1
        mods = mods_all[i]
        rows_out = T_LAT if last else T_ALL

        if kind == 0:
            qkv = _mod_matmul(xs, norm_rows, 2 * i, mods, 0, a_w_qkv_b, j, tm=PROJ_TM, tn=PROJ_TN, out_dtype=BF16,
                              name="na_qkv")
            att = _na_lat_attention(qkv, na_bias, j)
            att_ctx = None if last else _na_ctx_attention(qkv)
            w_o = a_w_o_b
        elif kind == 1:
            w_in = b_w_in[j]
            c0 = MLA_Q_RANK + MLA_KV_RANK
            zpad = jnp.zeros((D, HEAD_DIM - MLA_ROPE), F32)
            w_in_ext = jnp.concatenate([w_in, zpad, _swap_rope_halves(w_in[:, c0:]), zpad], axis=1)
            z = _mod_matmul(xs, norm_rows, 2 * i, mods, 0, w_in_ext.astype(BF16)[None], 0, tm=PROJ_TM,
                            tn=MLA_ZCOLS // 2, out_dtype=F32, name="mla_in")
            wq = b_w_uq[j].reshape(MLA_Q_RANK, HEADS, MLA_NOPE + MLA_ROPE).transpose(1, 0, 2)
            hpad = jnp.zeros((HEADS, MLA_Q_RANK, HEAD_DIM - MLA_ROPE), F32)
            wq_cat = jnp.concatenate([wq, hpad], axis=-1)
            wq_swap = jnp.concatenate([_swap_rope_halves(wq[..., MLA_NOPE:]), hpad], axis=-1)
            wkv = b_w_ukv[j].reshape(MLA_KV_RANK, HEADS, 2 * HEAD_DIM).transpose(1, 0, 2)
            cos, sin = _rope_tables()
            q, kv, kpe = _mla_up(z, b_q_norm[j], b_kv_norm[j], cos, sin, wq_cat.astype(BF16), wq_swap.astype(BF16),
                                 wkv.astype(BF16))
            att = _mla_lat_attention(q, kv, kpe)
            att_ctx = _mla_ctx_attention(q, kv, kpe)
            w_o = b_w_o_b
        else:
            uv = _mod_matmul(xs, norm_rows, 2 * i, mods, 0, c_w_in_b, j, tm=PROJ_TM, tn=PROJ_TN, out_dtype=F32,
                             bias=c_b_in, act="gelu", name="sgu_in")
            bs_full = jnp.repeat(c_bs[j].T, CHUNK, axis=1)
            att = _sgu(uv, c_ln_g[j], c_ln_b[j], c_ws[j].astype(BF16), bs_full)
            att_ctx = None if last else att
            w_o = c_w_o_b

        xs = _mixer_out(att, att_ctx, w_o, j, xs, mods)
        xs = _ffn(xs, norm_rows, 2 * i + 1, mods, f_w_in_b, f_conv_w, f_conv_b, f_w_out_b, i, final_g, rows=rows_out,
                  final_norm=last)

    return xs.reshape(BATCH, SEQ, D)
```

```python
import functools

import jax
import jax.numpy as jnp
import numpy as np
from jax import lax
from jax.experimental import pallas as pl
from jax.experimental.pallas import tpu as pltpu

F32 = jnp.float32
BF16 = jnp.bfloat16

D = 2048
BATCH = 4
SEQ = 2048
DEPTH = 4
GRID_W = 64
GRID_H = SEQ // GRID_W
CTX_LEN = 256
N_MIXERS = 3
EPS = 1e-6
ROPE_THETA = 10000.0
HEADS = 16
HEAD_DIM = 128
WIN_H = 8
WIN_W = 16
MLA_NOPE = 128
MLA_ROPE = 64
MLA_Q_RANK = 512
MLA_KV_RANK = 512
CHUNK = 128
SG_GROUPS = 16
D_FF = 5632
CONV_W = 3

T_LAT = BATCH * SEQ
T_CTX = BATCH * CTX_LEN
T_ALL = T_LAT + T_CTX
CTX_ROW_BLOCK0 = T_LAT // CTX_LEN
MASK_VALUE = -1e30
VMEM_LIMIT = 56 * 1024 * 1024
LANES = 128
BF16_ROWS = 16
PROLOGUE_ROWS = 128
PROJ_TM = 1024
PROJ_TN = 1024


def _params(*semantics):
    return pltpu.CompilerParams(dimension_semantics=semantics, vmem_limit_bytes=VMEM_LIMIT)


def _mod_row(tm):
    per_batch = SEQ // tm
    return lambda m: jnp.minimum(m // per_batch, BATCH)


def _resident(block_shape, index_map):
    return pl.BlockSpec(block_shape, index_map, pipeline_mode=pl.Buffered(1))


def _ada_kernel(s_ref, w_ref, b_ref, o_ref):
    s = s_ref[...]
    s = s * jax.nn.sigmoid(s)
    o_ref[...] = jnp.dot(s.astype(BF16), w_ref[...].astype(BF16), preferred_element_type=F32) + b_ref[...]


def _ada(s, ada_w, ada_b):
    tn = 1024
    n6 = 6 * D
    return pl.pallas_call(
        _ada_kernel,
        grid=(DEPTH, n6 // tn),
        in_specs=[
            pl.BlockSpec((8, D), lambda i, n: (0, 0)),
            pl.BlockSpec((None, D, tn), lambda i, n: (i, 0, n)),
            pl.BlockSpec((None, 1, tn), lambda i, n: (i, 0, n)),
        ],
        out_specs=pl.BlockSpec((None, 8, tn), lambda i, n: (i, 0, n)),
        out_shape=jax.ShapeDtypeStruct((DEPTH, 8, n6), F32),
        compiler_params=_params("parallel", "parallel"),
        name="ada",
    )(s, ada_w, ada_b.reshape(DEPTH, 1, n6))


def _gelu_exact(z):
    return 0.5 * z * (1.0 + lax.erf(z * np.float32(np.sqrt(0.5))))


def _modulate(xf, g, shift, scale):
    y = xf * lax.rsqrt(jnp.mean(xf * xf, axis=-1, keepdims=True) + EPS)
    return y * (g * (1.0 + scale)) + shift


def _modulate_rows(x_ref, h_ref, h_row0, n_rows, g, shift, scale):
    def body(c, carry):
        r = pl.multiple_of(c * PROLOGUE_ROWS, PROLOGUE_ROWS)
        h = _modulate(x_ref[pl.ds(r, PROLOGUE_ROWS), :], g, shift, scale)
        h_ref[pl.ds(h_row0 + r, PROLOGUE_ROWS), :] = h.astype(BF16)
        return carry
    lax.fori_loop(0, n_rows // PROLOGUE_ROWS, body, 0)


def _mod_mm_kernel(x_ref, g_ref, sh_ref, sc_ref, w_ref, o_ref, h_ref, *, tm):
    @pl.when(pl.program_id(1) == 0)
    def _():
        _modulate_rows(x_ref, h_ref, 0, tm, g_ref[...], sh_ref[...], sc_ref[...])

    o_ref[...] = jnp.dot(h_ref[...], w_ref[...], preferred_element_type=F32).astype(o_ref.dtype)


def _mod_matmul(x, norm_g, g_row, mods, w, w_layer, *, tm, tn, out_dtype, name):
    n = w.shape[2]
    brow = _mod_row(tm)
    return pl.pallas_call(
        functools.partial(_mod_mm_kernel, tm=tm),
        grid=(T_ALL // tm, n // tn),
        in_specs=[
            pl.BlockSpec((tm, D), lambda m, j: (m, 0)),
            pl.BlockSpec((None, 1, D), lambda m, j: (g_row, 0, 0)),
            pl.BlockSpec((None, 1, D), lambda m, j: (brow(m), 0, 0)),
            pl.BlockSpec((None, 1, D), lambda m, j: (brow(m), 0, 1)),
            pl.BlockSpec((None, D, tn), lambda m, j: (w_layer, 0, j)),
        ],
        out_specs=pl.BlockSpec((tm, tn), lambda m, j: (m, j)),
        out_shape=jax.ShapeDtypeStruct((T_ALL, n), out_dtype),
        scratch_shapes=[pltpu.VMEM((tm, D), BF16)],
        compiler_params=_params("parallel", "arbitrary"),
        name=name,
    )(x, norm_g, mods, mods, w)


MIXER_OUT_COLS = 512


MIXER_OUT_TM = 512
MIXER_LAT_TILES = T_LAT // MIXER_OUT_TM


def _mixer_out_kernel(*refs, has_ctx):
    if has_ctx:
        a_ref, ac_ref, w_ref, r_ref, gate_ref, o_ref = refs
        a = jnp.where(pl.program_id(0) < MIXER_LAT_TILES, a_ref[...], ac_ref[...])
    else:
        a_ref, w_ref, r_ref, gate_ref, o_ref = refs
        a = a_ref[...]
    for c in range(D // MIXER_OUT_COLS):
        cols = slice(c * MIXER_OUT_COLS, (c + 1) * MIXER_OUT_COLS)
        acc = jnp.dot(a, w_ref[:, cols], preferred_element_type=F32)
        o_ref[:, cols] = r_ref[:, cols] + gate_ref[:, cols] * acc


def _mixer_out(a_lat, a_ctx, w, w_layer, resid, mods):
    tm = MIXER_OUT_TM
    brow = _mod_row(tm)
    has_ctx = a_ctx is not None
    rows = T_ALL if has_ctx else T_LAT
    in_specs = [pl.BlockSpec((tm, D), lambda m: (jnp.minimum(m, MIXER_LAT_TILES - 1), 0))]
    args = [a_lat]
    if has_ctx:
        ctx_tile0 = (a_ctx.shape[0] - T_CTX) // tm
        in_specs.append(pl.BlockSpec((tm, D), lambda m: (jnp.maximum(m - MIXER_LAT_TILES, 0) + ctx_tile0, 0)))
        args.append(a_ctx)
    in_specs += [
        _resident((None, D, D), lambda m: (w_layer, 0, 0)),
        pl.BlockSpec((tm, D), lambda m: (m, 0)),
        pl.BlockSpec((None, 1, D), lambda m: (brow(m), 0, 2)),
    ]
    return pl.pallas_call(
        functools.partial(_mixer_out_kernel, has_ctx=has_ctx),
        grid=(rows // tm,),
        in_specs=in_specs,
        out_specs=pl.BlockSpec((tm, D), lambda m: (m, 0)),
        out_shape=jax.ShapeDtypeStruct((rows, D), F32),
        compiler_params=_params("parallel"),
        name="mixer_out",
    )(*args, w, resid, mods)


FFN_TM = 1024
FFN_TF = 512
FFN_HALO = BF16_ROWS
FFN_OUT_COLS = 512
FFN_ROW_CHUNK = CTX_LEN
FFN_LOOKAHEAD = 4
FFN_ROW_SETS = 4
FFN_SLABS = FFN_TF // LANES


def _ffn_kernel(x_ref, xp_ref, xn_ref, g_ref, sh_ref, sc_ref, gate_ref, wa_ref, wg_ref, cwa_ref, cwg_ref,
                cba_ref, cbg_ref, wo_ref, fg_ref, o_ref, h_ref, ua_ref, ug_ref, act_ref, *, final_norm):
    tm, halo = FFN_TM, FFN_HALO
    f = pl.program_id(1)

    @pl.when(f == 0)
    def _():
        g, sh, sc = g_ref[...], sh_ref[...], sc_ref[...]
        h_ref[:halo, :] = _modulate(xp_ref[...], g, sh, sc).astype(BF16)
        h_ref[halo + tm:, :] = _modulate(xn_ref[...], g, sh, sc).astype(BF16)
        _modulate_rows(x_ref, h_ref, halo, tm, g, sh, sc)
        o_ref[...] = jnp.zeros_like(o_ref)

    ch, n_ch, sets = FFN_ROW_CHUNK, tm // FFN_ROW_CHUNK, FFN_ROW_SETS
    set_rows = ch // sets
    sub = lax.broadcasted_iota(jnp.int32, (set_rows, LANES), 0)

    def hidden(r):
        lo = 0 if r == 0 else halo + r * ch
        hi = tm + 2 * halo if r == n_ch - 1 else halo + (r + 1) * ch
        h = h_ref[lo:hi, :]
        ua = jnp.dot(h, wa_ref[...], preferred_element_type=F32)
        ug = jnp.dot(h, wg_ref[...], preferred_element_type=F32)
        for s in range(FFN_SLABS):
            ua_ref[s, lo:hi, :] = ua[:, s * LANES:(s + 1) * LANES]
            ug_ref[s, lo:hi, :] = ug[:, s * LANES:(s + 1) * LANES]

    def gate_and_project(r):
        base = halo + r * ch
        rows = slice(r * ch, (r + 1) * ch)
        row0 = pl.program_id(0) * tm + r * ch
        seq_mask = jnp.where(row0 < T_LAT, SEQ - 1, CTX_LEN - 1)
        keep_first_up = jnp.where((row0 & seq_mask) == 0, 0.0, 1.0).astype(F32)
        keep_last_dn = jnp.where(((row0 + ch) & seq_mask) == 0, 0.0, 1.0).astype(F32)
        first_up = jnp.where(sub == 0, keep_first_up, 1.0)
        last_dn = jnp.where(sub == set_rows - 1, keep_last_dn, 1.0)

        for s in range(FFN_SLABS):
            lanes = slice(s * LANES, (s + 1) * LANES)
            for j in range(sets):
                def conv(u_ref, cw_ref, cb_ref):
                    up = u_ref[s, pl.ds(base + j - 1, set_rows, stride=sets), :]
                    cur = u_ref[s, pl.ds(base + j, set_rows, stride=sets), :]
                    dn = u_ref[s, pl.ds(base + j + 1, set_rows, stride=sets), :]
                    if j == 0:
                        up = up * first_up
                    if j == sets - 1:
                        dn = dn * last_dn
                    return (up * cw_ref[0:1, lanes] + cur * cw_ref[1:2, lanes] + dn * cw_ref[2:3, lanes]
                            + cb_ref[:, lanes])

                a = conv(ua_ref, cwa_ref, cba_ref)
                gg = conv(ug_ref, cwg_ref, cbg_ref)
                act_ref[s, pl.ds(r * ch + j, set_rows, stride=sets), :] = a * (gg * jax.nn.sigmoid(gg))

        act = jnp.concatenate([act_ref[s, rows, :] for s in range(FFN_SLABS)], axis=1).astype(BF16)
        for c in range(D // FFN_OUT_COLS):
            cols = slice(c * FFN_OUT_COLS, (c + 1) * FFN_OUT_COLS)
            o_ref[rows, cols] += jnp.dot(act, wo_ref[:, cols], preferred_element_type=F32)

    for r in range(min(FFN_LOOKAHEAD, n_ch)):
        hidden(r)
    for r in range(n_ch):
        if r + FFN_LOOKAHEAD < n_ch:
            hidden(r + FFN_LOOKAHEAD)
        gate_and_project(r)

    @pl.when(f == pl.num_programs(1) - 1)
    def _():
        gate = gate_ref[...]

        def body(c, carry):
            rows = pl.ds(pl.multiple_of(c * PROLOGUE_ROWS, PROLOGUE_ROWS), PROLOGUE_ROWS)
            y = x_ref[rows, :] + gate * o_ref[rows, :]
            if final_norm:
                y = (y * lax.rsqrt(jnp.mean(y * y, axis=-1, keepdims=True) + EPS)) * fg_ref[...]
            o_ref[rows, :] = y
            return carry
        lax.fori_loop(0, tm // PROLOGUE_ROWS, body, 0)


def _ffn(x, norm_g, g_row, mods, w_in, conv_w, conv_b, w_out, layer, final_g, *, rows, final_norm):
    tm, tf, halo = FFN_TM, FFN_TF, FFN_HALO
    nf = D_FF // tf
    brow = _mod_row(tm)
    last_halo = rows // halo - 1
    cb = conv_b.reshape(DEPTH, 1, 2 * D_FF)
    return pl.pallas_call(
        functools.partial(_ffn_kernel, final_norm=final_norm),
        grid=(rows // tm, nf),
        in_specs=[
            pl.BlockSpec((tm, D), lambda m, f: (m, 0), pipeline_mode=pl.Buffered(1)),
            pl.BlockSpec((halo, D), lambda m, f: (jnp.maximum(m * (tm // halo) - 1, 0), 0)),
            pl.BlockSpec((halo, D), lambda m, f: (jnp.minimum((m + 1) * (tm // halo), last_halo), 0)),
            pl.BlockSpec((None, 1, D), lambda m, f: (g_row, 0, 0)),
            pl.BlockSpec((None, 1, D), lambda m, f: (brow(m), 0, 3)),
            pl.BlockSpec((None, 1, D), lambda m, f: (brow(m), 0, 4)),
            pl.BlockSpec((None, 1, D), lambda m, f: (brow(m), 0, 5)),
            pl.BlockSpec((None, D, tf), lambda m, f: (layer, 0, f)),
            pl.BlockSpec((None, D, tf), lambda m, f: (layer, 0, f + nf)),
            pl.BlockSpec((None, CONV_W, tf), lambda m, f: (layer, 0, f)),
            pl.BlockSpec((None, CONV_W, tf), lambda m, f: (layer, 0, f + nf)),
            pl.BlockSpec((None, 1, tf), lambda m, f: (layer, 0, f)),
            pl.BlockSpec((None, 1, tf), lambda m, f: (layer, 0, f + nf)),
            pl.BlockSpec((None, tf, D), lambda m, f: (layer, f, 0)),
            pl.BlockSpec((1, D), lambda m, f: (0, 0)),
        ],
        out_specs=pl.BlockSpec((tm, D), lambda m, f: (m, 0)),
        out_shape=jax.ShapeDtypeStruct((rows, D), F32),
        scratch_shapes=[
            pltpu.VMEM((tm + 2 * halo, D), BF16),
            pltpu.VMEM((FFN_SLABS, tm + 2 * halo, LANES), F32),
            pltpu.VMEM((FFN_SLABS, tm + 2 * halo, LANES), F32),
            pltpu.VMEM((FFN_SLABS, tm, LANES), F32),
        ],
        compiler_params=_params("parallel", "arbitrary"),
        name="ffn",
    )(x, x, x, norm_g, mods, mods, mods, w_in, w_in, conv_w, conv_w, cb, cb, w_out, final_g.reshape(1, D))


NA_QROWS = 4
NA_QBLOCK = NA_QROWS * GRID_W
NA_SUB = 8
NA_BAND_ROWS = 12
NA_BAND = NA_BAND_ROWS * GRID_W


ATTN_LOOKAHEAD = 2


def _nt_dot(a, b):
    return lax.dot_general(a, b, (((1,), (1,)), ((), ())), preferred_element_type=F32)


NA_QBLOCKS = GRID_H // NA_QROWS
NA_SCALE = HEAD_DIM ** -0.5
NA_SCALE_LOG2E = np.float32(NA_SCALE * np.log2(np.e))
NA_KINDS = ((lambda j: 0, WIN_H - 1), (lambda j: j, WIN_H // 2 - 1), (lambda j: NA_BAND_ROWS - WIN_H, -1))


def _na_band_start(blk):
    return jnp.clip(blk * NA_QROWS - WIN_H // 2, 0, GRID_H - NA_BAND_ROWS)


def _na_lat_kernel(q_ref, k_ref, v_ref, kc_ref, vc_ref, bias_ref, o_ref):
    i = pl.program_id(2)
    kc = kc_ref[...]
    vc = vc_ref[...]

    def key_start(sb):
        return pl.multiple_of(_na_band_start(i * NA_SUB + sb) * GRID_W, NA_QROWS * GRID_W)

    def logits(sb):
        blk = i * NA_SUB + sb
        kind = jnp.where(blk == 0, 0, jnp.where(blk == NA_QBLOCKS - 1, 2, 1))
        q = q_ref[sb * NA_QBLOCK:(sb + 1) * NA_QBLOCK, :]
        return _nt_dot(q, k_ref[pl.ds(key_start(sb), NA_BAND), :]) + bias_ref[kind], _nt_dot(q, kc)

    ahead = [logits(sb) for sb in range(min(ATTN_LOOKAHEAD, NA_SUB))]
    for sb in range(NA_SUB):
        t, tc = ahead.pop(0)
        if sb + ATTN_LOOKAHEAD < NA_SUB:
            ahead.append(logits(sb + ATTN_LOOKAHEAD))
        mx = jnp.maximum(jnp.max(t, axis=-1, keepdims=True), jnp.max(tc, axis=-1, keepdims=True))
        e = jnp.exp2((t - mx) * NA_SCALE_LOG2E)
        ec = jnp.exp2((tc - mx) * NA_SCALE_LOG2E)
        denom = jnp.sum(e, axis=-1, keepdims=True) + jnp.sum(ec, axis=-1, keepdims=True)
        o = (jnp.dot(e.astype(BF16), v_ref[pl.ds(key_start(sb), NA_BAND), :], preferred_element_type=F32)
             + jnp.dot(ec.astype(BF16), vc, preferred_element_type=F32))
        o_ref[sb * NA_QBLOCK:(sb + 1) * NA_QBLOCK, :] = (o / denom).astype(o_ref.dtype)


def _na_bias_table(rpb):
    n = rpb.shape[0]
    w = np.arange(GRID_W)[:, None]
    k = np.arange(GRID_W)[None, :]
    col0 = np.clip(w - WIN_W // 2, 0, GRID_W - WIN_W)
    in_win = (k >= col0) & (k < col0 + WIN_W)
    col_pick = (np.arange(2 * WIN_W - 1)[:, None, None] == (k - w + WIN_W - 1)[None]) & in_win[None]
    t = jnp.einsum("nrx,xwk->nrwk", rpb, col_pick.astype(np.float32), precision=lax.Precision.HIGHEST)
    t = jnp.where(jnp.asarray(in_win), t, MASK_VALUE) * np.float32(1.0 / NA_SCALE)
    return pl.pallas_call(
        _na_bias_blocks_kernel,
        grid=(n,),
        in_specs=[pl.BlockSpec((None, 2 * WIN_H - 1, GRID_W, GRID_W), lambda i: (i, 0, 0, 0))],
        out_specs=pl.BlockSpec((None, len(NA_KINDS), NA_QBLOCK, NA_BAND), lambda i: (i, 0, 0, 0)),
        out_shape=jax.ShapeDtypeStruct((n, len(NA_KINDS), NA_QBLOCK, NA_BAND), F32),
        compiler_params=_params("parallel"),
        name="na_bias_blocks",
    )(t)


def _na_bias_blocks_kernel(t_ref, o_ref):
    masked = jnp.full((GRID_W, GRID_W), MASK_VALUE / NA_SCALE, F32)
    for kind, (first_valid, offset0) in enumerate(NA_KINDS):
        for j in range(NA_QROWS):
            lo = first_valid(j)
            for c in range(NA_BAND_ROWS):
                tile = t_ref[c - j + offset0] if lo <= c < lo + WIN_H else masked
                o_ref[kind, j * GRID_W:(j + 1) * GRID_W, c * GRID_W:(c + 1) * GRID_W] = tile


def _na_lat_attention(qkv, bias_tab, layer):
    blocks = SEQ // (NA_SUB * NA_QBLOCK)
    qb = NA_SUB * NA_QBLOCK
    return pl.pallas_call(
        _na_lat_kernel,
        grid=(HEADS, BATCH, blocks),
        in_specs=[
            pl.BlockSpec((qb, HEAD_DIM), lambda h, b, i: (b * blocks + i, h)),
            pl.BlockSpec((SEQ, HEAD_DIM), lambda h, b, i: (b, HEADS + h)),
            pl.BlockSpec((SEQ, HEAD_DIM), lambda h, b, i: (b, 2 * HEADS + h)),
            pl.BlockSpec((CTX_LEN, HEAD_DIM), lambda h, b, i: (CTX_ROW_BLOCK0 + b, HEADS + h)),
            pl.BlockSpec((CTX_LEN, HEAD_DIM), lambda h, b, i: (CTX_ROW_BLOCK0 + b, 2 * HEADS + h)),
            pl.BlockSpec((None, len(NA_KINDS), NA_QBLOCK, NA_BAND), lambda h, b, i: (layer * HEADS + h, 0, 0, 0)),
        ],
        out_specs=pl.BlockSpec((qb, HEAD_DIM), lambda h, b, i: (b * blocks + i, h)),
        out_shape=jax.ShapeDtypeStruct((T_LAT, D), BF16),
        compiler_params=_params("parallel", "parallel", "arbitrary"),
        name="na_lat",
    )(qkv, qkv, qkv, qkv, qkv, bias_tab)


def _softmax_pv(s, v):
    e = jnp.exp(s - jnp.max(s, axis=-1, keepdims=True))
    o = jnp.dot(e.astype(BF16), v, preferred_element_type=F32)
    return o / jnp.sum(e, axis=-1, keepdims=True)


def _na_ctx_kernel(q_ref, k_ref, v_ref, o_ref):
    scale = np.float32(HEAD_DIM ** -0.5)
    for h in range(HEADS):
        cols = slice(h * HEAD_DIM, (h + 1) * HEAD_DIM)
        s = _nt_dot(q_ref[:, cols], k_ref[:, cols]) * scale
        o_ref[:, cols] = _softmax_pv(s, v_ref[:, cols]).astype(o_ref.dtype)


def _na_ctx_attention(qkv):
    return pl.pallas_call(
        _na_ctx_kernel,
        grid=(BATCH,),
        in_specs=[
            pl.BlockSpec((CTX_LEN, D), lambda b: (CTX_ROW_BLOCK0 + b, 0)),
            pl.BlockSpec((CTX_LEN, D), lambda b: (CTX_ROW_BLOCK0 + b, 1)),
            pl.BlockSpec((CTX_LEN, D), lambda b: (CTX_ROW_BLOCK0 + b, 2)),
        ],
        out_specs=pl.BlockSpec((CTX_LEN, D), lambda b: (b, 0)),
        out_shape=jax.ShapeDtypeStruct((T_CTX, D), BF16),
        compiler_params=_params("parallel"),
        name="na_ctx",
    )(qkv, qkv, qkv)


MLA_QK = 2 * HEAD_DIM
MLA_QBLOCK = 256
MLA_LOOKAHEAD = 1
MLA_SUB = 8
MLA_ZCOLS = MLA_Q_RANK + MLA_KV_RANK + 2 * HEAD_DIM
MLA_UP_TM = 512


def _rms(z, g):
    return (z * lax.rsqrt(jnp.mean(z * z, axis=-1, keepdims=True) + EPS)) * g


def _mla_up_kernel(z_ref, qn_ref, kvn_ref, cos_ref, sin_ref, wq_ref, wqs_ref, wkv_ref, q_ref, kv_ref, kpe_ref):
    cq = _rms(z_ref[:, :MLA_Q_RANK], qn_ref[...]).astype(BF16)
    ckv = _rms(z_ref[:, MLA_Q_RANK:MLA_Q_RANK + MLA_KV_RANK], kvn_ref[...]).astype(BF16)
    c0 = MLA_Q_RANK + MLA_KV_RANK
    cos = cos_ref[...]
    sin = sin_ref[...]
    kpe_ref[...] = (z_ref[:, c0:c0 + HEAD_DIM] * cos + z_ref[:, c0 + HEAD_DIM:] * sin).astype(BF16)
    for h in range(HEADS):
        q = jnp.dot(cq, wq_ref[h], preferred_element_type=F32)
        qs = jnp.dot(cq, wqs_ref[h], preferred_element_type=F32)
        q_ref[h, :, :HEAD_DIM] = q[:, :HEAD_DIM].astype(BF16)
        q_ref[h, :, HEAD_DIM:] = (q[:, HEAD_DIM:] * cos + qs * sin).astype(BF16)
        kv_ref[h] = jnp.dot(ckv, wkv_ref[h], preferred_element_type=F32).astype(BF16)


def _mla_up(z, q_norm, kv_norm, cos, sin, wq, wqs, wkv):
    tm = MLA_UP_TM
    return pl.pallas_call(
        _mla_up_kernel,
        grid=(T_ALL // tm,),
        in_specs=[
            pl.BlockSpec((tm, MLA_ZCOLS), lambda m: (m, 0)),
            pl.BlockSpec((1, MLA_Q_RANK), lambda m: (0, 0)),
            pl.BlockSpec((1, MLA_KV_RANK), lambda m: (0, 0)),
            pl.BlockSpec((tm, HEAD_DIM), lambda m: (m, 0)),
            pl.BlockSpec((tm, HEAD_DIM), lambda m: (m, 0)),
            _resident((HEADS, MLA_Q_RANK, MLA_QK), lambda m: (0, 0, 0)),
            _resident((HEADS, MLA_Q_RANK, HEAD_DIM), lambda m: (0, 0, 0)),
            _resident((HEADS, MLA_KV_RANK, 2 * HEAD_DIM), lambda m: (0, 0, 0)),
        ],
        out_specs=[
            pl.BlockSpec((HEADS, tm, MLA_QK), lambda m: (0, m, 0)),
            pl.BlockSpec((HEADS, tm, 2 * HEAD_DIM), lambda m: (0, m, 0)),
            pl.BlockSpec((tm, HEAD_DIM), lambda m: (m, 0)),
        ],
        out_shape=[
            jax.ShapeDtypeStruct((HEADS, T_ALL, MLA_QK), BF16),
            jax.ShapeDtypeStruct((HEADS, T_ALL, 2 * HEAD_DIM), BF16),
            jax.ShapeDtypeStruct((T_ALL, HEAD_DIM), BF16),
        ],
        compiler_params=_params("parallel"),
        name="mla_up",
    )(z, q_norm.reshape(1, -1), kv_norm.reshape(1, -1), cos, sin, wq, wqs, wkv)


MLA_SCALE = np.float32((MLA_NOPE + MLA_ROPE) ** -0.5)
MLA_SCALE_LOG2E = np.float32((MLA_NOPE + MLA_ROPE) ** -0.5 * np.log2(np.e))


def _mla_lat_kernel(q_ref, kn_ref, v_ref, kpe_ref, knc_ref, vc_ref, kpec_ref, o_ref, kcat_ref):
    @pl.when(pl.program_id(2) == 0)
    def _():
        kcat_ref[:SEQ, :HEAD_DIM] = kn_ref[...]
        kcat_ref[:SEQ, HEAD_DIM:] = kpe_ref[...]
        kcat_ref[SEQ:, :HEAD_DIM] = knc_ref[...]
        kcat_ref[SEQ:, HEAD_DIM:] = kpec_ref[...]

    def scores(sb):
        return _nt_dot(q_ref[sb * MLA_QBLOCK:(sb + 1) * MLA_QBLOCK, :], kcat_ref[...])

    ahead = [scores(sb) for sb in range(min(MLA_LOOKAHEAD, MLA_SUB))]
    for sb in range(MLA_SUB):
        rows = slice(sb * MLA_QBLOCK, (sb + 1) * MLA_QBLOCK)
        s = ahead.pop(0)
        if sb + MLA_LOOKAHEAD < MLA_SUB:
            ahead.append(scores(sb + MLA_LOOKAHEAD))
        e = jnp.exp2((s - jnp.max(s, axis=-1, keepdims=True)) * MLA_SCALE_LOG2E)
        o = (jnp.dot(e[:, :SEQ].astype(BF16), v_ref[...], preferred_element_type=F32)
             + jnp.dot(e[:, SEQ:].astype(BF16), vc_ref[...], preferred_element_type=F32))
        o_ref[rows, :] = (o / jnp.sum(e, axis=-1, keepdims=True)).astype(o_ref.dtype)


def _mla_lat_attention(q, kv, kpe):
    qb = MLA_SUB * MLA_QBLOCK
    blocks = SEQ // qb
    ctx_q = CTX_ROW_BLOCK0
    return pl.pallas_call(
        _mla_lat_kernel,
        grid=(BATCH, HEADS, blocks),
        in_specs=[
            pl.BlockSpec((None, qb, MLA_QK), lambda b, h, i: (h, b * blocks + i, 0)),
            pl.BlockSpec((None, SEQ, HEAD_DIM), lambda b, h, i: (h, b, 0)),
            pl.BlockSpec((None, SEQ, HEAD_DIM), lambda b, h, i: (h, b, 1)),
            pl.BlockSpec((SEQ, HEAD_DIM), lambda b, h, i: (b, 0)),
            pl.BlockSpec((None, CTX_LEN, HEAD_DIM), lambda b, h, i: (h, ctx_q + b, 0)),
            pl.BlockSpec((None, CTX_LEN, HEAD_DIM), lambda b, h, i: (h, ctx_q + b, 1)),
            pl.BlockSpec((CTX_LEN, HEAD_DIM), lambda b, h, i: (ctx_q + b, 0)),
        ],
        out_specs=pl.BlockSpec((qb, HEAD_DIM), lambda b, h, i: (b * blocks + i, h)),
        out_shape=jax.ShapeDtypeStruct((T_LAT, D), BF16),
        scratch_shapes=[pltpu.VMEM((SEQ + CTX_LEN, MLA_QK), BF16)],
        compiler_params=_params("parallel", "parallel", "arbitrary"),
        name="mla_lat",
    )(q, kv, kv, kpe, kv, kv, kpe)


def _mla_ctx_kernel(q_ref, kv_ref, kpe_ref, o_ref):
    kpe = kpe_ref[...]
    for h in range(HEADS):
        kcat = jnp.concatenate([kv_ref[h, :, :HEAD_DIM], kpe], axis=1)
        s = _nt_dot(q_ref[h], kcat) * MLA_SCALE
        o_ref[:, h * HEAD_DIM:(h + 1) * HEAD_DIM] = _softmax_pv(s, kv_ref[h, :, HEAD_DIM:]).astype(o_ref.dtype)


def _mla_ctx_attention(q, kv, kpe):
    ctx_q = CTX_ROW_BLOCK0
    return pl.pallas_call(
        _mla_ctx_kernel,
        grid=(BATCH,),
        in_specs=[
            pl.BlockSpec((HEADS, CTX_LEN, MLA_QK), lambda b: (0, ctx_q + b, 0)),
            pl.BlockSpec((HEADS, CTX_LEN, 2 * HEAD_DIM), lambda b: (0, ctx_q + b, 0)),
            pl.BlockSpec((CTX_LEN, HEAD_DIM), lambda b: (ctx_q + b, 0)),
        ],
        out_specs=pl.BlockSpec((CTX_LEN, D), lambda b: (b, 0)),
        out_shape=jax.ShapeDtypeStruct((T_CTX, D), BF16),
        compiler_params=_params("parallel"),
        name="mla_ctx",
    )(q, kv, kpe)


def _rope_tables():
    half = MLA_ROPE // 2
    freqs = ROPE_THETA ** (-jnp.arange(0, half, 2, dtype=F32) / half)
    t = jnp.arange(SEQ)
    rows = (t // GRID_W).astype(F32)[:, None] * freqs
    cols = (t % GRID_W).astype(F32)[:, None] * freqs
    cr, sr, cc, sn = jnp.cos(rows), jnp.sin(rows), jnp.cos(cols), jnp.sin(cols)
    pad = jnp.zeros((SEQ, HEAD_DIM - MLA_ROPE), F32)
    cos = jnp.concatenate([cr, cr, cc, cc, pad], axis=1)
    sin = jnp.concatenate([-sr, sr, -sn, sn, pad], axis=1)
    cos_c = jnp.concatenate([jnp.ones((T_CTX, MLA_ROPE), F32), jnp.zeros((T_CTX, HEAD_DIM - MLA_ROPE), F32)], axis=1)
    return (jnp.concatenate([jnp.tile(cos, (BATCH, 1)), cos_c], axis=0),
            jnp.concatenate([jnp.tile(sin, (BATCH, 1)), jnp.zeros((T_CTX, HEAD_DIM), F32)], axis=0))


def _swap_rope_halves(w):
    q = MLA_ROPE // 4
    return jnp.concatenate([w[..., q:2 * q], w[..., :q], w[..., 3 * q:], w[..., 2 * q:3 * q]], axis=-1)


SGU_TM = 512
SGU_COLS = 512


def _sgu_kernel(x_ref, g_ref, sh_ref, sc_ref, w_ref, b_ref, lng_ref, lnb_ref, ws_ref, bs_ref, o_ref,
                h_ref, u_ref, v_ref):
    _modulate_rows(x_ref, h_ref, 0, SGU_TM, g_ref[...], sh_ref[...], sc_ref[...])
    h = h_ref[...]
    for nc in range(2 * D // SGU_COLS):
        cols = slice(nc * SGU_COLS, (nc + 1) * SGU_COLS)
        acc = _gelu_exact(jnp.dot(h, w_ref[:, cols], preferred_element_type=F32) + b_ref[:, cols])
        if nc < D // SGU_COLS:
            u_ref[:, cols] = acc
        else:
            v_ref[:, nc * SGU_COLS - D:(nc + 1) * SGU_COLS - D] = acc
    for r in range(SGU_TM // CHUNK):
        rows = slice(r * CHUNK, (r + 1) * CHUNK)
        v = v_ref[rows, :]
        mu = jnp.mean(v, axis=-1, keepdims=True)
        vc = v - mu
        var = jnp.mean(vc * vc, axis=-1, keepdims=True)
        vn = ((vc * lax.rsqrt(var + EPS)) * lng_ref[...] + lnb_ref[...]).astype(BF16)
        for g in range(SG_GROUPS):
            cols = slice(g * CHUNK, (g + 1) * CHUNK)
            mix = jnp.dot(ws_ref[g], vn[:, cols], preferred_element_type=F32) + bs_ref[:, cols]
            o_ref[rows, cols] = (u_ref[rows, cols] * mix).astype(o_ref.dtype)


def _sgu(x, norm_g, g_row, mods, w_in, b_in, ln_g, ln_b, ws, bs_full, layer):
    tm = SGU_TM
    brow = _mod_row(tm)
    return pl.pallas_call(
        _sgu_kernel,
        grid=(T_ALL // tm,),
        in_specs=[
            pl.BlockSpec((tm, D), lambda m: (m, 0)),
            pl.BlockSpec((None, 1, D), lambda m: (g_row, 0, 0)),
            pl.BlockSpec((None, 1, D), lambda m: (brow(m), 0, 0)),
            pl.BlockSpec((None, 1, D), lambda m: (brow(m), 0, 1)),
            _resident((None, D, 2 * D), lambda m: (layer, 0, 0)),
            pl.BlockSpec((None, 1, 2 * D), lambda m: (layer, 0, 0)),
            pl.BlockSpec((1, D), lambda m: (0, 0)),
            pl.BlockSpec((1, D), lambda m: (0, 0)),
            pl.BlockSpec((SG_GROUPS, CHUNK, CHUNK), lambda m: (0, 0, 0)),
            pl.BlockSpec((CHUNK, D), lambda m: (0, 0)),
        ],
        out_specs=pl.BlockSpec((tm, D), lambda m: (m, 0)),
        out_shape=jax.ShapeDtypeStruct((T_ALL, D), BF16),
        scratch_shapes=[pltpu.VMEM((tm, D), BF16), pltpu.VMEM((tm, D), F32), pltpu.VMEM((tm, D), F32)],
        compiler_params=_params("parallel"),
        name="sgu",
    )(x, norm_g, mods, mods, w_in, b_in.reshape(b_in.shape[0], 1, 2 * D), ln_g.reshape(1, D), ln_b.reshape(1, D), ws,
      bs_full)


def kernel(x, c, ctx, c_ctx, ada_w, ada_b, norm_g, final_g, a_w_qkv, a_w_o, a_rpb, b_w_in, b_q_norm, b_kv_norm,
           b_w_uq, b_w_ukv, b_w_o, c_w_in, c_b_in, c_ln_g, c_ln_b, c_ws, c_bs, c_w_o, f_w_in, f_conv_w, f_conv_b,
           f_w_out):
    xs = jnp.concatenate([x.reshape(T_LAT, D), ctx.reshape(T_CTX, D)], axis=0)
    cond = jnp.concatenate([c, c_ctx[None, :], jnp.zeros((8 - BATCH - 1, D), F32)], axis=0)
    mods_all = _ada(cond, ada_w, ada_b).reshape(DEPTH, 8, 1, 6 * D)
    norm_rows = norm_g.reshape(DEPTH * 2, 1, D)
    na_bias = _na_bias_table(a_rpb.reshape(-1, 2 * WIN_H - 1, 2 * WIN_W - 1))

    a_w_qkv_b, a_w_o_b = a_w_qkv.astype(BF16), a_w_o.astype(BF16)
    b_w_o_b, c_w_in_b, c_w_o_b = b_w_o.astype(BF16), c_w_in.astype(BF16), c_w_o.astype(BF16)
    f_w_in_b, f_w_out_b = f_w_in.astype(BF16), f_w_out.astype(BF16)

    for i in range(DEPTH):
        kind, j = i % N_MIXERS, i // N_MIXERS
        last = i == DEPTH - 1
        mods = mods_all[i]
        rows_out = T_LAT if last else T_ALL

        if kind == 0:
            qkv = _mod_matmul(xs, norm_rows, 2 * i, mods, a_w_qkv_b, j, tm=PROJ_TM, tn=PROJ_TN, out_dtype=BF16,
                              name="na_qkv")
            att = _na_lat_attention(qkv, na_bias, j)
            att_ctx = None if last else _na_ctx_attention(qkv)
            w_o = a_w_o_b
        elif kind == 1:
            w_in = b_w_in[j]
            c0 = MLA_Q_RANK + MLA_KV_RANK
            zpad = jnp.zeros((D, HEAD_DIM - MLA_ROPE), F32)
            w_in_ext = jnp.concatenate([w_in, zpad, _swap_rope_halves(w_in[:, c0:]), zpad], axis=1)
            z = _mod_matmul(xs, norm_rows, 2 * i, mods, w_in_ext.astype(BF16)[None], 0, tm=PROJ_TM,
                            tn=MLA_ZCOLS // 2, out_dtype=F32, name="mla_in")
            wq = b_w_uq[j].reshape(MLA_Q_RANK, HEADS, MLA_NOPE + MLA_ROPE).transpose(1, 0, 2)
            hpad = jnp.zeros((HEADS, MLA_Q_RANK, HEAD_DIM - MLA_ROPE), F32)
            wq_cat = jnp.concatenate([wq, hpad], axis=-1)
            wq_swap = jnp.concatenate([_swap_rope_halves(wq[..., MLA_NOPE:]), hpad], axis=-1)
            wkv = b_w_ukv[j].reshape(MLA_KV_RANK, HEADS, 2 * HEAD_DIM).transpose(1, 0, 2)
            cos, sin = _rope_tables()
            q, kv, kpe = _mla_up(z, b_q_norm[j], b_kv_norm[j], cos, sin, wq_cat.astype(BF16), wq_swap.astype(BF16),
                                 wkv.astype(BF16))
            att = _mla_lat_attention(q, kv, kpe)
            att_ctx = _mla_ctx_attention(q, kv, kpe)
            w_o = b_w_o_b
        else:
            bs_full = jnp.repeat(c_bs[j].T, CHUNK, axis=1)
            att = _sgu(xs, norm_rows, 2 * i, mods, c_w_in_b, c_b_in, c_ln_g[j], c_ln_b[j], c_ws[j].astype(BF16),
                       bs_full, j)
            att_ctx = None if last else att
            w_o = c_w_o_b

        xs = _mixer_out(att, att_ctx, w_o, j, xs, mods)
        xs = _ffn(xs, norm_rows, 2 * i + 1, mods, f_w_in_b, f_conv_w, f_conv_b, f_w_out_b, i, final_g, rows=rows_out,
                  final_norm=last)

    return xs.reshape(BATCH, SEQ, D)
```

```python
import functools

import jax
import jax.numpy as jnp
import numpy as np
from jax import lax
from jax.experimental import pallas as pl
from jax.experimental.pallas import tpu as pltpu

F32 = jnp.float32
BF16 = jnp.bfloat16

D = 2048
BATCH = 4
SEQ = 2048
DEPTH = 4
GRID_W = 64
GRID_H = SEQ // GRID_W
CTX_LEN = 256
N_MIXERS = 3
EPS = 1e-6
ROPE_THETA = 10000.0
HEADS = 16
HEAD_DIM = 128
WIN_H = 8
WIN_W = 16
MLA_NOPE = 128
MLA_ROPE = 64
MLA_Q_RANK = 512
MLA_KV_RANK = 512
CHUNK = 128
SG_GROUPS = 16
D_FF = 5632
CONV_W = 3

T_LAT = BATCH * SEQ
T_CTX = BATCH * CTX_LEN
T_ALL = T_LAT + T_CTX
CTX_ROW_BLOCK0 = T_LAT // CTX_LEN
MASK_VALUE = -1e30
VMEM_LIMIT = 56 * 1024 * 1024
LANES = 128
BF16_ROWS = 16
PROLOGUE_ROWS = 128
PROJ_TM = 1024
PROJ_TN = 1024


def _params(*semantics):
    return pltpu.CompilerParams(dimension_semantics=semantics, vmem_limit_bytes=VMEM_LIMIT)


def _mod_row(tm):
    per_batch = SEQ // tm
    return lambda m: jnp.minimum(m // per_batch, BATCH)


def _resident(block_shape, index_map):
    return pl.BlockSpec(block_shape, index_map, pipeline_mode=pl.Buffered(1))


def _ada_kernel(s_ref, w_ref, b_ref, o_ref):
    s = s_ref[...]
    s = s * jax.nn.sigmoid(s)
    o_ref[...] = jnp.dot(s.astype(BF16), w_ref[...].astype(BF16), preferred_element_type=F32) + b_ref[...]


def _ada(s, ada_w, ada_b):
    tn = 1024
    n6 = 6 * D
    return pl.pallas_call(
        _ada_kernel,
        grid=(DEPTH, n6 // tn),
        in_specs=[
            pl.BlockSpec((8, D), lambda i, n: (0, 0)),
            pl.BlockSpec((None, D, tn), lambda i, n: (i, 0, n)),
            pl.BlockSpec((None, 1, tn), lambda i, n: (i, 0, n)),
        ],
        out_specs=pl.BlockSpec((None, 8, tn), lambda i, n: (i, 0, n)),
        out_shape=jax.ShapeDtypeStruct((DEPTH, 8, n6), F32),
        compiler_params=_params("parallel", "parallel"),
        name="ada",
    )(s, ada_w, ada_b.reshape(DEPTH, 1, n6))


def _gelu_exact(z):
    return 0.5 * z * (1.0 + lax.erf(z * np.float32(np.sqrt(0.5))))


def _modulate(xf, g, shift, scale):
    y = xf * lax.rsqrt(jnp.mean(xf * xf, axis=-1, keepdims=True) + EPS)
    return y * (g * (1.0 + scale)) + shift


def _modulate_rows(x_ref, h_ref, h_row0, n_rows, g, shift, scale):
    def body(c, carry):
        r = pl.multiple_of(c * PROLOGUE_ROWS, PROLOGUE_ROWS)
        h = _modulate(x_ref[pl.ds(r, PROLOGUE_ROWS), :], g, shift, scale)
        h_ref[pl.ds(h_row0 + r, PROLOGUE_ROWS), :] = h.astype(BF16)
        return carry
    lax.fori_loop(0, n_rows // PROLOGUE_ROWS, body, 0)


def _mod_mm_kernel(x_ref, g_ref, sh_ref, sc_ref, w_ref, o_ref, h_ref, *, tm):
    @pl.when(pl.program_id(1) == 0)
    def _():
        _modulate_rows(x_ref, h_ref, 0, tm, g_ref[...], sh_ref[...], sc_ref[...])

    o_ref[...] = jnp.dot(h_ref[...], w_ref[...], preferred_element_type=F32).astype(o_ref.dtype)


def _mod_matmul(x, norm_g, g_row, mods, w, w_layer, *, tm, tn, out_dtype, name):
    n = w.shape[2]
    brow = _mod_row(tm)
    return pl.pallas_call(
        functools.partial(_mod_mm_kernel, tm=tm),
        grid=(T_ALL // tm, n // tn),
        in_specs=[
            pl.BlockSpec((tm, D), lambda m, j: (m, 0)),
            pl.BlockSpec((None, 1, D), lambda m, j: (g_row, 0, 0)),
            pl.BlockSpec((None, 1, D), lambda m, j: (brow(m), 0, 0)),
            pl.BlockSpec((None, 1, D), lambda m, j: (brow(m), 0, 1)),
            pl.BlockSpec((None, D, tn), lambda m, j: (w_layer, 0, j)),
        ],
        out_specs=pl.BlockSpec((tm, tn), lambda m, j: (m, j)),
        out_shape=jax.ShapeDtypeStruct((T_ALL, n), out_dtype),
        scratch_shapes=[pltpu.VMEM((tm, D), BF16)],
        compiler_params=_params("parallel", "arbitrary"),
        name=name,
    )(x, norm_g, mods, mods, w)


MIXER_OUT_COLS = 512


MIXER_OUT_TM = 512
MIXER_LAT_TILES = T_LAT // MIXER_OUT_TM


def _mixer_out_kernel(*refs, has_ctx):
    if has_ctx:
        a_ref, ac_ref, w_ref, r_ref, gate_ref, o_ref = refs
        a = jnp.where(pl.program_id(0) < MIXER_LAT_TILES, a_ref[...], ac_ref[...])
    else:
        a_ref, w_ref, r_ref, gate_ref, o_ref = refs
        a = a_ref[...]
    for c in range(D // MIXER_OUT_COLS):
        cols = slice(c * MIXER_OUT_COLS, (c + 1) * MIXER_OUT_COLS)
        acc = jnp.dot(a, w_ref[:, cols], preferred_element_type=F32)
        o_ref[:, cols] = r_ref[:, cols] + gate_ref[:, cols] * acc


def _mixer_out(a_lat, a_ctx, w, w_layer, resid, mods):
    tm = MIXER_OUT_TM
    brow = _mod_row(tm)
    has_ctx = a_ctx is not None
    rows = T_ALL if has_ctx else T_LAT
    in_specs = [pl.BlockSpec((tm, D), lambda m: (jnp.minimum(m, MIXER_LAT_TILES - 1), 0))]
    args = [a_lat]
    if has_ctx:
        ctx_tile0 = (a_ctx.shape[0] - T_CTX) // tm
        in_specs.append(pl.BlockSpec((tm, D), lambda m: (jnp.maximum(m - MIXER_LAT_TILES, 0) + ctx_tile0, 0)))
        args.append(a_ctx)
    in_specs += [
        _resident((None, D, D), lambda m: (w_layer, 0, 0)),
        pl.BlockSpec((tm, D), lambda m: (m, 0)),
        pl.BlockSpec((None, 1, D), lambda m: (brow(m), 0, 2)),
    ]
    return pl.pallas_call(
        functools.partial(_mixer_out_kernel, has_ctx=has_ctx),
        grid=(rows // tm,),
        in_specs=in_specs,
        out_specs=pl.BlockSpec((tm, D), lambda m: (m, 0)),
        out_shape=jax.ShapeDtypeStruct((rows, D), F32),
        compiler_params=_params("parallel"),
        name="mixer_out",
    )(*args, w, resid, mods)


FFN_TM = 1024
FFN_TF = 512
FFN_HALO = BF16_ROWS
FFN_OUT_COLS = 512
FFN_ROW_CHUNK = CTX_LEN
FFN_LOOKAHEAD = 4
FFN_ROW_SETS = 4
FFN_SLABS = FFN_TF // LANES


def _ffn_kernel(x_ref, xp_ref, xn_ref, g_ref, sh_ref, sc_ref, gate_ref, wa_ref, wg_ref, cwa_ref, cwg_ref,
                cba_ref, cbg_ref, wo_ref, fg_ref, o_ref, h_ref, ua_ref, ug_ref, act_ref, *, final_norm):
    tm, halo = FFN_TM, FFN_HALO
    f = pl.program_id(1)

    @pl.when(f == 0)
    def _():
        g, sh, sc = g_ref[...], sh_ref[...], sc_ref[...]
        h_ref[:halo, :] = _modulate(xp_ref[...], g, sh, sc).astype(BF16)
        h_ref[halo + tm:, :] = _modulate(xn_ref[...], g, sh, sc).astype(BF16)
        _modulate_rows(x_ref, h_ref, halo, tm, g, sh, sc)
        o_ref[...] = jnp.zeros_like(o_ref)

    ch, n_ch, sets = FFN_ROW_CHUNK, tm // FFN_ROW_CHUNK, FFN_ROW_SETS
    set_rows = ch // sets
    sub = lax.broadcasted_iota(jnp.int32, (set_rows, LANES), 0)

    def hidden(r):
        lo = 0 if r == 0 else halo + r * ch
        hi = tm + 2 * halo if r == n_ch - 1 else halo + (r + 1) * ch
        h = h_ref[lo:hi, :]
        ua = jnp.dot(h, wa_ref[...], preferred_element_type=F32)
        ug = jnp.dot(h, wg_ref[...], preferred_element_type=F32)
        for s in range(FFN_SLABS):
            ua_ref[s, lo:hi, :] = ua[:, s * LANES:(s + 1) * LANES]
            ug_ref[s, lo:hi, :] = ug[:, s * LANES:(s + 1) * LANES]

    def gate_and_project(r):
        base = halo + r * ch
        rows = slice(r * ch, (r + 1) * ch)
        row0 = pl.program_id(0) * tm + r * ch
        seq_mask = jnp.where(row0 < T_LAT, SEQ - 1, CTX_LEN - 1)
        keep_first_up = jnp.where((row0 & seq_mask) == 0, 0.0, 1.0).astype(F32)
        keep_last_dn = jnp.where(((row0 + ch) & seq_mask) == 0, 0.0, 1.0).astype(F32)
        first_up = jnp.where(sub == 0, keep_first_up, 1.0)
        last_dn = jnp.where(sub == set_rows - 1, keep_last_dn, 1.0)

        for s in range(FFN_SLABS):
            lanes = slice(s * LANES, (s + 1) * LANES)
            for j in range(sets):
                def conv(u_ref, cw_ref, cb_ref):
                    up = u_ref[s, pl.ds(base + j - 1, set_rows, stride=sets), :]
                    cur = u_ref[s, pl.ds(base + j, set_rows, stride=sets), :]
                    dn = u_ref[s, pl.ds(base + j + 1, set_rows, stride=sets), :]
                    if j == 0:
                        up = up * first_up
                    if j == sets - 1:
                        dn = dn * last_dn
                    return (up * cw_ref[0:1, lanes] + cur * cw_ref[1:2, lanes] + dn * cw_ref[2:3, lanes]
                            + cb_ref[:, lanes])

                a = conv(ua_ref, cwa_ref, cba_ref)
                gg = conv(ug_ref, cwg_ref, cbg_ref)
                act_ref[s, pl.ds(r * ch + j, set_rows, stride=sets), :] = a * (gg * jax.nn.sigmoid(gg))

        act = jnp.concatenate([act_ref[s, rows, :] for s in range(FFN_SLABS)], axis=1).astype(BF16)
        for c in range(D // FFN_OUT_COLS):
            cols = slice(c * FFN_OUT_COLS, (c + 1) * FFN_OUT_COLS)
            o_ref[rows, cols] += jnp.dot(act, wo_ref[:, cols], preferred_element_type=F32)

    for r in range(min(FFN_LOOKAHEAD, n_ch)):
        hidden(r)
    for r in range(n_ch):
        if r + FFN_LOOKAHEAD < n_ch:
            hidden(r + FFN_LOOKAHEAD)
        gate_and_project(r)

    @pl.when(f == pl.num_programs(1) - 1)
    def _():
        gate = gate_ref[...]

        def body(c, carry):
            rows = pl.ds(pl.multiple_of(c * PROLOGUE_ROWS, PROLOGUE_ROWS), PROLOGUE_ROWS)
            y = x_ref[rows, :] + gate * o_ref[rows, :]
            if final_norm:
                y = (y * lax.rsqrt(jnp.mean(y * y, axis=-1, keepdims=True) + EPS)) * fg_ref[...]
            o_ref[rows, :] = y
            return carry
        lax.fori_loop(0, tm // PROLOGUE_ROWS, body, 0)


def _ffn(x, norm_g, g_row, mods, w_in, w_out, w_layer, conv_w, conv_b, layer, final_g, *, rows, final_norm):
    tm, tf, halo = FFN_TM, FFN_TF, FFN_HALO
    nf = D_FF // tf
    brow = _mod_row(tm)
    last_halo = rows // halo - 1
    cb = conv_b.reshape(DEPTH, 1, 2 * D_FF)
    return pl.pallas_call(
        functools.partial(_ffn_kernel, final_norm=final_norm),
        grid=(rows // tm, nf),
        in_specs=[
            pl.BlockSpec((tm, D), lambda m, f: (m, 0), pipeline_mode=pl.Buffered(1)),
            pl.BlockSpec((halo, D), lambda m, f: (jnp.maximum(m * (tm // halo) - 1, 0), 0)),
            pl.BlockSpec((halo, D), lambda m, f: (jnp.minimum((m + 1) * (tm // halo), last_halo), 0)),
            pl.BlockSpec((None, 1, D), lambda m, f: (g_row, 0, 0)),
            pl.BlockSpec((None, 1, D), lambda m, f: (brow(m), 0, 3)),
            pl.BlockSpec((None, 1, D), lambda m, f: (brow(m), 0, 4)),
            pl.BlockSpec((None, 1, D), lambda m, f: (brow(m), 0, 5)),
            pl.BlockSpec((None, D, tf), lambda m, f: (w_layer, 0, f)),
            pl.BlockSpec((None, D, tf), lambda m, f: (w_layer, 0, f + nf)),
            pl.BlockSpec((None, CONV_W, tf), lambda m, f: (layer, 0, f)),
            pl.BlockSpec((None, CONV_W, tf), lambda m, f: (layer, 0, f + nf)),
            pl.BlockSpec((None, 1, tf), lambda m, f: (layer, 0, f)),
            pl.BlockSpec((None, 1, tf), lambda m, f: (layer, 0, f + nf)),
            pl.BlockSpec((None, tf, D), lambda m, f: (w_layer, f, 0)),
            pl.BlockSpec((1, D), lambda m, f: (0, 0)),
        ],
        out_specs=pl.BlockSpec((tm, D), lambda m, f: (m, 0)),
        out_shape=jax.ShapeDtypeStruct((rows, D), F32),
        scratch_shapes=[
            pltpu.VMEM((tm + 2 * halo, D), BF16),
            pltpu.VMEM((FFN_SLABS, tm + 2 * halo, LANES), F32),
            pltpu.VMEM((FFN_SLABS, tm + 2 * halo, LANES), F32),
            pltpu.VMEM((FFN_SLABS, tm, LANES), F32),
        ],
        compiler_params=_params("parallel", "arbitrary"),
        name="ffn",
    )(x, x, x, norm_g, mods, mods, mods, w_in, w_in, conv_w, conv_w, cb, cb, w_out, final_g.reshape(1, D))


NA_QROWS = 4
NA_QBLOCK = NA_QROWS * GRID_W
NA_SUB = 8
NA_BAND_ROWS = 12
NA_BAND = NA_BAND_ROWS * GRID_W


CAST_LAYERS = 2
CAST_STEPS = BATCH * HEADS
CAST_IN_ROWS = D // CAST_STEPS
CAST_OUT_ROWS = 2 * D_FF // CAST_STEPS
assert CAST_IN_ROWS % BF16_ROWS == 0 and CAST_OUT_ROWS % BF16_ROWS == 0 and DEPTH % CAST_LAYERS == 0


def _cast_job(cast, step_of):
    if cast is None:
        return [], [], [], []
    f_w_in, f_w_out, pair = cast
    w_in_spec = pl.BlockSpec((CAST_LAYERS, CAST_IN_ROWS, 2 * D_FF), lambda *g: (pair, step_of(*g), 0))
    w_out_spec = pl.BlockSpec((CAST_LAYERS, CAST_OUT_ROWS, D), lambda *g: (pair, step_of(*g) // 2, 0))
    shapes = [jax.ShapeDtypeStruct((CAST_LAYERS,) + f_w_in.shape[1:], BF16),
              jax.ShapeDtypeStruct((CAST_LAYERS,) + f_w_out.shape[1:], BF16)]
    return [w_in_spec, w_out_spec], [_cast_out_spec(w_in_spec), _cast_out_spec(w_out_spec)], shapes, [f_w_in, f_w_out]


def _cast_out_spec(spec):
    return pl.BlockSpec(spec.block_shape, lambda *g: (0,) + tuple(spec.index_map(*g))[1:])


def _run_cast_job(refs, n_in, with_cast):
    if not with_cast:
        return refs
    w_in_src, w_out_src, o_ref, w_in_dst, w_out_dst = refs[n_in:n_in + 5]
    w_in_dst[...] = w_in_src[...].astype(BF16)
    w_out_dst[...] = w_out_src[...].astype(BF16)
    return refs[:n_in] + (o_ref,) + refs[n_in + 5:]


ATTN_LOOKAHEAD = 2


def _nt_dot(a, b):
    return lax.dot_general(a, b, (((1,), (1,)), ((), ())), preferred_element_type=F32)


NA_QBLOCKS = GRID_H // NA_QROWS
NA_SCALE = HEAD_DIM ** -0.5
NA_SCALE_LOG2E = np.float32(NA_SCALE * np.log2(np.e))
NA_KINDS = ((lambda j: 0, WIN_H - 1), (lambda j: j, WIN_H // 2 - 1), (lambda j: NA_BAND_ROWS - WIN_H, -1))


def _na_band_start(blk):
    return jnp.clip(blk * NA_QROWS - WIN_H // 2, 0, GRID_H - NA_BAND_ROWS)


def _na_lat_kernel(*refs, with_cast):
    q_ref, k_ref, v_ref, kc_ref, vc_ref, bias_ref, o_ref = _run_cast_job(refs, 6, with_cast)
    i = pl.program_id(2)
    kc = kc_ref[...]
    vc = vc_ref[...]

    def key_start(sb):
        return pl.multiple_of(_na_band_start(i * NA_SUB + sb) * GRID_W, NA_QROWS * GRID_W)

    def logits(sb):
        blk = i * NA_SUB + sb
        kind = jnp.where(blk == 0, 0, jnp.where(blk == NA_QBLOCKS - 1, 2, 1))
        q = q_ref[sb * NA_QBLOCK:(sb + 1) * NA_QBLOCK, :]
        return _nt_dot(q, k_ref[pl.ds(key_start(sb), NA_BAND), :]) + bias_ref[kind], _nt_dot(q, kc)

    ahead = [logits(sb) for sb in range(min(ATTN_LOOKAHEAD, NA_SUB))]
    for sb in range(NA_SUB):
        t, tc = ahead.pop(0)
        if sb + ATTN_LOOKAHEAD < NA_SUB:
            ahead.append(logits(sb + ATTN_LOOKAHEAD))
        mx = jnp.maximum(jnp.max(t, axis=-1, keepdims=True), jnp.max(tc, axis=-1, keepdims=True))
        e = jnp.exp2((t - mx) * NA_SCALE_LOG2E)
        ec = jnp.exp2((tc - mx) * NA_SCALE_LOG2E)
        denom = jnp.sum(e, axis=-1, keepdims=True) + jnp.sum(ec, axis=-1, keepdims=True)
        o = (jnp.dot(e.astype(BF16), v_ref[pl.ds(key_start(sb), NA_BAND), :], preferred_element_type=F32)
             + jnp.dot(ec.astype(BF16), vc, preferred_element_type=F32))
        o_ref[sb * NA_QBLOCK:(sb + 1) * NA_QBLOCK, :] = (o / denom).astype(o_ref.dtype)


def _na_bias_table(rpb):
    n = rpb.shape[0]
    w = np.arange(GRID_W)[:, None]
    k = np.arange(GRID_W)[None, :]
    col0 = np.clip(w - WIN_W // 2, 0, GRID_W - WIN_W)
    in_win = (k >= col0) & (k < col0 + WIN_W)
    col_pick = (np.arange(2 * WIN_W - 1)[:, None, None] == (k - w + WIN_W - 1)[None]) & in_win[None]
    t = jnp.einsum("nrx,xwk->nrwk", rpb, col_pick.astype(np.float32), precision=lax.Precision.HIGHEST)
    t = jnp.where(jnp.asarray(in_win), t, MASK_VALUE) * np.float32(1.0 / NA_SCALE)
    return pl.pallas_call(
        _na_bias_blocks_kernel,
        grid=(n,),
        in_specs=[pl.BlockSpec((None, 2 * WIN_H - 1, GRID_W, GRID_W), lambda i: (i, 0, 0, 0))],
        out_specs=pl.BlockSpec((None, len(NA_KINDS), NA_QBLOCK, NA_BAND), lambda i: (i, 0, 0, 0)),
        out_shape=jax.ShapeDtypeStruct((n, len(NA_KINDS), NA_QBLOCK, NA_BAND), F32),
        compiler_params=_params("parallel"),
        name="na_bias_blocks",
    )(t)


def _na_bias_blocks_kernel(t_ref, o_ref):
    masked = jnp.full((GRID_W, GRID_W), MASK_VALUE / NA_SCALE, F32)
    for kind, (first_valid, offset0) in enumerate(NA_KINDS):
        for j in range(NA_QROWS):
            lo = first_valid(j)
            for c in range(NA_BAND_ROWS):
                tile = t_ref[c - j + offset0] if lo <= c < lo + WIN_H else masked
                o_ref[kind, j * GRID_W:(j + 1) * GRID_W, c * GRID_W:(c + 1) * GRID_W] = tile


def _na_lat_attention(qkv, bias_tab, layer, cast=None):
    blocks = SEQ // (NA_SUB * NA_QBLOCK)
    assert blocks == 1
    qb = NA_SUB * NA_QBLOCK
    cast_in, cast_out, cast_shape, cast_args = _cast_job(cast, lambda h, b, i: h * BATCH + b)
    return pl.pallas_call(
        functools.partial(_na_lat_kernel, with_cast=cast is not None),
        grid=(HEADS, BATCH, blocks),
        in_specs=[
            pl.BlockSpec((qb, HEAD_DIM), lambda h, b, i: (b * blocks + i, h)),
            pl.BlockSpec((SEQ, HEAD_DIM), lambda h, b, i: (b, HEADS + h)),
            pl.BlockSpec((SEQ, HEAD_DIM), lambda h, b, i: (b, 2 * HEADS + h)),
            pl.BlockSpec((CTX_LEN, HEAD_DIM), lambda h, b, i: (CTX_ROW_BLOCK0 + b, HEADS + h)),
            pl.BlockSpec((CTX_LEN, HEAD_DIM), lambda h, b, i: (CTX_ROW_BLOCK0 + b, 2 * HEADS + h)),
            pl.BlockSpec((None, len(NA_KINDS), NA_QBLOCK, NA_BAND), lambda h, b, i: (layer * HEADS + h, 0, 0, 0)),
        ] + cast_in,
        out_specs=[pl.BlockSpec((qb, HEAD_DIM), lambda h, b, i: (b * blocks + i, h))] + cast_out,
        out_shape=[jax.ShapeDtypeStruct((T_LAT, D), BF16)] + cast_shape,
        compiler_params=_params("arbitrary", "arbitrary", "arbitrary"),
        name="na_lat",
    )(qkv, qkv, qkv, qkv, qkv, bias_tab, *cast_args)


def _softmax_pv(s, v):
    e = jnp.exp(s - jnp.max(s, axis=-1, keepdims=True))
    o = jnp.dot(e.astype(BF16), v, preferred_element_type=F32)
    return o / jnp.sum(e, axis=-1, keepdims=True)


def _na_ctx_kernel(q_ref, k_ref, v_ref, o_ref):
    scale = np.float32(HEAD_DIM ** -0.5)
    for h in range(HEADS):
        cols = slice(h * HEAD_DIM, (h + 1) * HEAD_DIM)
        s = _nt_dot(q_ref[:, cols], k_ref[:, cols]) * scale
        o_ref[:, cols] = _softmax_pv(s, v_ref[:, cols]).astype(o_ref.dtype)


def _na_ctx_attention(qkv):
    return pl.pallas_call(
        _na_ctx_kernel,
        grid=(BATCH,),
        in_specs=[
            pl.BlockSpec((CTX_LEN, D), lambda b: (CTX_ROW_BLOCK0 + b, 0)),
            pl.BlockSpec((CTX_LEN, D), lambda b: (CTX_ROW_BLOCK0 + b, 1)),
            pl.BlockSpec((CTX_LEN, D), lambda b: (CTX_ROW_BLOCK0 + b, 2)),
        ],
        out_specs=pl.BlockSpec((CTX_LEN, D), lambda b: (b, 0)),
        out_shape=jax.ShapeDtypeStruct((T_CTX, D), BF16),
        compiler_params=_params("parallel"),
        name="na_ctx",
    )(qkv, qkv, qkv)


MLA_QK = 2 * HEAD_DIM
MLA_QBLOCK = 256
MLA_LOOKAHEAD = 1
MLA_SUB = 8
MLA_ZCOLS = MLA_Q_RANK + MLA_KV_RANK + 2 * HEAD_DIM
MLA_UP_TM = 512


def _rms(z, g):
    return (z * lax.rsqrt(jnp.mean(z * z, axis=-1, keepdims=True) + EPS)) * g


def _mla_up_kernel(z_ref, qn_ref, kvn_ref, cos_ref, sin_ref, wq_ref, wqs_ref, wkv_ref, q_ref, kv_ref, kpe_ref):
    cq = _rms(z_ref[:, :MLA_Q_RANK], qn_ref[...]).astype(BF16)
    ckv = _rms(z_ref[:, MLA_Q_RANK:MLA_Q_RANK + MLA_KV_RANK], kvn_ref[...]).astype(BF16)
    c0 = MLA_Q_RANK + MLA_KV_RANK
    cos = cos_ref[...]
    sin = sin_ref[...]
    kpe_ref[...] = (z_ref[:, c0:c0 + HEAD_DIM] * cos + z_ref[:, c0 + HEAD_DIM:] * sin).astype(BF16)
    for h in range(HEADS):
        q = jnp.dot(cq, wq_ref[h], preferred_element_type=F32)
        qs = jnp.dot(cq, wqs_ref[h], preferred_element_type=F32)
        q_ref[h, :, :HEAD_DIM] = q[:, :HEAD_DIM].astype(BF16)
        q_ref[h, :, HEAD_DIM:] = (q[:, HEAD_DIM:] * cos + qs * sin).astype(BF16)
        kv_ref[h] = jnp.dot(ckv, wkv_ref[h], preferred_element_type=F32).astype(BF16)


def _mla_up(z, q_norm, kv_norm, cos, sin, wq, wqs, wkv):
    tm = MLA_UP_TM
    return pl.pallas_call(
        _mla_up_kernel,
        grid=(T_ALL // tm,),
        in_specs=[
            pl.BlockSpec((tm, MLA_ZCOLS), lambda m: (m, 0)),
            pl.BlockSpec((1, MLA_Q_RANK), lambda m: (0, 0)),
            pl.BlockSpec((1, MLA_KV_RANK), lambda m: (0, 0)),
            pl.BlockSpec((tm, HEAD_DIM), lambda m: (m, 0)),
            pl.BlockSpec((tm, HEAD_DIM), lambda m: (m, 0)),
            _resident((HEADS, MLA_Q_RANK, MLA_QK), lambda m: (0, 0, 0)),
            _resident((HEADS, MLA_Q_RANK, HEAD_DIM), lambda m: (0, 0, 0)),
            _resident((HEADS, MLA_KV_RANK, 2 * HEAD_DIM), lambda m: (0, 0, 0)),
        ],
        out_specs=[
            pl.BlockSpec((HEADS, tm, MLA_QK), lambda m: (0, m, 0)),
            pl.BlockSpec((HEADS, tm, 2 * HEAD_DIM), lambda m: (0, m, 0)),
            pl.BlockSpec((tm, HEAD_DIM), lambda m: (m, 0)),
        ],
        out_shape=[
            jax.ShapeDtypeStruct((HEADS, T_ALL, MLA_QK), BF16),
            jax.ShapeDtypeStruct((HEADS, T_ALL, 2 * HEAD_DIM), BF16),
            jax.ShapeDtypeStruct((T_ALL, HEAD_DIM), BF16),
        ],
        compiler_params=_params("parallel"),
        name="mla_up",
    )(z, q_norm.reshape(1, -1), kv_norm.reshape(1, -1), cos, sin, wq, wqs, wkv)


MLA_SCALE = np.float32((MLA_NOPE + MLA_ROPE) ** -0.5)
MLA_SCALE_LOG2E = np.float32((MLA_NOPE + MLA_ROPE) ** -0.5 * np.log2(np.e))


def _mla_lat_kernel(*refs, with_cast):
    q_ref, kn_ref, v_ref, kpe_ref, knc_ref, vc_ref, kpec_ref, o_ref, kcat_ref = _run_cast_job(refs, 7, with_cast)

    @pl.when(pl.program_id(2) == 0)
    def _():
        kcat_ref[:SEQ, :HEAD_DIM] = kn_ref[...]
        kcat_ref[:SEQ, HEAD_DIM:] = kpe_ref[...]
        kcat_ref[SEQ:, :HEAD_DIM] = knc_ref[...]
        kcat_ref[SEQ:, HEAD_DIM:] = kpec_ref[...]

    def scores(sb):
        return _nt_dot(q_ref[sb * MLA_QBLOCK:(sb + 1) * MLA_QBLOCK, :], kcat_ref[...])

    ahead = [scores(sb) for sb in range(min(MLA_LOOKAHEAD, MLA_SUB))]
    for sb in range(MLA_SUB):
        rows = slice(sb * MLA_QBLOCK, (sb + 1) * MLA_QBLOCK)
        s = ahead.pop(0)
        if sb + MLA_LOOKAHEAD < MLA_SUB:
            ahead.append(scores(sb + MLA_LOOKAHEAD))
        e = jnp.exp2((s - jnp.max(s, axis=-1, keepdims=True)) * MLA_SCALE_LOG2E)
        o = (jnp.dot(e[:, :SEQ].astype(BF16), v_ref[...], preferred_element_type=F32)
             + jnp.dot(e[:, SEQ:].astype(BF16), vc_ref[...], preferred_element_type=F32))
        o_ref[rows, :] = (o / jnp.sum(e, axis=-1, keepdims=True)).astype(o_ref.dtype)


def _mla_lat_attention(q, kv, kpe, cast=None):
    qb = MLA_SUB * MLA_QBLOCK
    blocks = SEQ // qb
    assert blocks == 1
    ctx_q = CTX_ROW_BLOCK0
    cast_in, cast_out, cast_shape, cast_args = _cast_job(cast, lambda b, h, i: b * HEADS + h)
    return pl.pallas_call(
        functools.partial(_mla_lat_kernel, with_cast=cast is not None),
        grid=(BATCH, HEADS, blocks),
        in_specs=[
            pl.BlockSpec((None, qb, MLA_QK), lambda b, h, i: (h, b * blocks + i, 0)),
            pl.BlockSpec((None, SEQ, HEAD_DIM), lambda b, h, i: (h, b, 0)),
            pl.BlockSpec((None, SEQ, HEAD_DIM), lambda b, h, i: (h, b, 1)),
            pl.BlockSpec((SEQ, HEAD_DIM), lambda b, h, i: (b, 0)),
            pl.BlockSpec((None, CTX_LEN, HEAD_DIM), lambda b, h, i: (h, ctx_q + b, 0)),
            pl.BlockSpec((None, CTX_LEN, HEAD_DIM), lambda b, h, i: (h, ctx_q + b, 1)),
            pl.BlockSpec((CTX_LEN, HEAD_DIM), lambda b, h, i: (ctx_q + b, 0)),
        ] + cast_in,
        out_specs=[pl.BlockSpec((qb, HEAD_DIM), lambda b, h, i: (b * blocks + i, h))] + cast_out,
        out_shape=[jax.ShapeDtypeStruct((T_LAT, D), BF16)] + cast_shape,
        scratch_shapes=[pltpu.VMEM((SEQ + CTX_LEN, MLA_QK), BF16)],
        compiler_params=_params("arbitrary", "arbitrary", "arbitrary"),
        name="mla_lat",
    )(q, kv, kv, kpe, kv, kv, kpe, *cast_args)


def _mla_ctx_kernel(q_ref, kv_ref, kpe_ref, o_ref):
    kpe = kpe_ref[...]
    for h in range(HEADS):
        kcat = jnp.concatenate([kv_ref[h, :, :HEAD_DIM], kpe], axis=1)
        s = _nt_dot(q_ref[h], kcat) * MLA_SCALE
        o_ref[:, h * HEAD_DIM:(h + 1) * HEAD_DIM] = _softmax_pv(s, kv_ref[h, :, HEAD_DIM:]).astype(o_ref.dtype)


def _mla_ctx_attention(q, kv, kpe):
    ctx_q = CTX_ROW_BLOCK0
    return pl.pallas_call(
        _mla_ctx_kernel,
        grid=(BATCH,),
        in_specs=[
            pl.BlockSpec((HEADS, CTX_LEN, MLA_QK), lambda b: (0, ctx_q + b, 0)),
            pl.BlockSpec((HEADS, CTX_LEN, 2 * HEAD_DIM), lambda b: (0, ctx_q + b, 0)),
            pl.BlockSpec((CTX_LEN, HEAD_DIM), lambda b: (ctx_q + b, 0)),
        ],
        out_specs=pl.BlockSpec((CTX_LEN, D), lambda b: (b, 0)),
        out_shape=jax.ShapeDtypeStruct((T_CTX, D), BF16),
        compiler_params=_params("parallel"),
        name="mla_ctx",
    )(q, kv, kpe)


def _rope_tables():
    half = MLA_ROPE // 2
    freqs = ROPE_THETA ** (-jnp.arange(0, half, 2, dtype=F32) / half)
    t = jnp.arange(SEQ)
    rows = (t // GRID_W).astype(F32)[:, None] * freqs
    cols = (t % GRID_W).astype(F32)[:, None] * freqs
    cr, sr, cc, sn = jnp.cos(rows), jnp.sin(rows), jnp.cos(cols), jnp.sin(cols)
    pad = jnp.zeros((SEQ, HEAD_DIM - MLA_ROPE), F32)
    cos = jnp.concatenate([cr, cr, cc, cc, pad], axis=1)
    sin = jnp.concatenate([-sr, sr, -sn, sn, pad], axis=1)
    cos_c = jnp.concatenate([jnp.ones((T_CTX, MLA_ROPE), F32), jnp.zeros((T_CTX, HEAD_DIM - MLA_ROPE), F32)], axis=1)
    return (jnp.concatenate([jnp.tile(cos, (BATCH, 1)), cos_c], axis=0),
            jnp.concatenate([jnp.tile(sin, (BATCH, 1)), jnp.zeros((T_CTX, HEAD_DIM), F32)], axis=0))


def _swap_rope_halves(w):
    q = MLA_ROPE // 4
    return jnp.concatenate([w[..., q:2 * q], w[..., :q], w[..., 3 * q:], w[..., 2 * q:3 * q]], axis=-1)


SGU_TM = 512
SGU_COLS = 512


def _sgu_kernel(x_ref, g_ref, sh_ref, sc_ref, w_ref, b_ref, lng_ref, lnb_ref, ws_ref, bs_ref, o_ref,
                h_ref, u_ref, v_ref):
    _modulate_rows(x_ref, h_ref, 0, SGU_TM, g_ref[...], sh_ref[...], sc_ref[...])
    h = h_ref[...]
    for nc in range(2 * D // SGU_COLS):
        cols = slice(nc * SGU_COLS, (nc + 1) * SGU_COLS)
        acc = _gelu_exact(jnp.dot(h, w_ref[:, cols], preferred_element_type=F32) + b_ref[:, cols])
        if nc < D // SGU_COLS:
            u_ref[:, cols] = acc
        else:
            v_ref[:, nc * SGU_COLS - D:(nc + 1) * SGU_COLS - D] = acc
    for r in range(SGU_TM // CHUNK):
        rows = slice(r * CHUNK, (r + 1) * CHUNK)
        v = v_ref[rows, :]
        mu = jnp.mean(v, axis=-1, keepdims=True)
        vc = v - mu
        var = jnp.mean(vc * vc, axis=-1, keepdims=True)
        vn = ((vc * lax.rsqrt(var + EPS)) * lng_ref[...] + lnb_ref[...]).astype(BF16)
        for g in range(SG_GROUPS):
            cols = slice(g * CHUNK, (g + 1) * CHUNK)
            mix = jnp.dot(ws_ref[g], vn[:, cols], preferred_element_type=F32) + bs_ref[:, cols]
            o_ref[rows, cols] = (u_ref[rows, cols] * mix).astype(o_ref.dtype)


def _sgu(x, norm_g, g_row, mods, w_in, b_in, ln_g, ln_b, ws, bs_full, layer):
    tm = SGU_TM
    brow = _mod_row(tm)
    return pl.pallas_call(
        _sgu_kernel,
        grid=(T_ALL // tm,),
        in_specs=[
            pl.BlockSpec((tm, D), lambda m: (m, 0)),
            pl.BlockSpec((None, 1, D), lambda m: (g_row, 0, 0)),
            pl.BlockSpec((None, 1, D), lambda m: (brow(m), 0, 0)),
            pl.BlockSpec((None, 1, D), lambda m: (brow(m), 0, 1)),
            _resident((None, D, 2 * D), lambda m: (layer, 0, 0)),
            pl.BlockSpec((None, 1, 2 * D), lambda m: (layer, 0, 0)),
            pl.BlockSpec((1, D), lambda m: (0, 0)),
            pl.BlockSpec((1, D), lambda m: (0, 0)),
            pl.BlockSpec((SG_GROUPS, CHUNK, CHUNK), lambda m: (0, 0, 0)),
            pl.BlockSpec((CHUNK, D), lambda m: (0, 0)),
        ],
        out_specs=pl.BlockSpec((tm, D), lambda m: (m, 0)),
        out_shape=jax.ShapeDtypeStruct((T_ALL, D), BF16),
        scratch_shapes=[pltpu.VMEM((tm, D), BF16), pltpu.VMEM((tm, D), F32), pltpu.VMEM((tm, D), F32)],
        compiler_params=_params("parallel"),
        name="sgu",
    )(x, norm_g, mods, mods, w_in, b_in.reshape(b_in.shape[0], 1, 2 * D), ln_g.reshape(1, D), ln_b.reshape(1, D), ws,
      bs_full)


def kernel(x, c, ctx, c_ctx, ada_w, ada_b, norm_g, final_g, a_w_qkv, a_w_o, a_rpb, b_w_in, b_q_norm, b_kv_norm,
           b_w_uq, b_w_ukv, b_w_o, c_w_in, c_b_in, c_ln_g, c_ln_b, c_ws, c_bs, c_w_o, f_w_in, f_conv_w, f_conv_b,
           f_w_out):
    xs = jnp.concatenate([x.reshape(T_LAT, D), ctx.reshape(T_CTX, D)], axis=0)
    cond = jnp.concatenate([c, c_ctx[None, :], jnp.zeros((8 - BATCH - 1, D), F32)], axis=0)
    mods_all = _ada(cond, ada_w, ada_b).reshape(DEPTH, 8, 1, 6 * D)
    norm_rows = norm_g.reshape(DEPTH * 2, 1, D)
    na_bias = _na_bias_table(a_rpb.reshape(-1, 2 * WIN_H - 1, 2 * WIN_W - 1))

    a_w_qkv_b, a_w_o_b = a_w_qkv.astype(BF16), a_w_o.astype(BF16)
    b_w_o_b, c_w_in_b, c_w_o_b = b_w_o.astype(BF16), c_w_in.astype(BF16), c_w_o.astype(BF16)
    ffn_weights = {}

    for i in range(DEPTH):
        kind, j = i % N_MIXERS, i // N_MIXERS
        last = i == DEPTH - 1
        mods = mods_all[i]
        rows_out = T_LAT if last else T_ALL
        cast = (f_w_in, f_w_out, i) if i < DEPTH // CAST_LAYERS else None

        if kind == 0:
            qkv = _mod_matmul(xs, norm_rows, 2 * i, mods, a_w_qkv_b, j, tm=PROJ_TM, tn=PROJ_TN, out_dtype=BF16,
                              name="na_qkv")
            att, *cast_out = _na_lat_attention(qkv, na_bias, j, cast)
            att_ctx = None if last else _na_ctx_attention(qkv)
            w_o = a_w_o_b
        elif kind == 1:
            w_in = b_w_in[j]
            c0 = MLA_Q_RANK + MLA_KV_RANK
            zpad = jnp.zeros((D, HEAD_DIM - MLA_ROPE), F32)
            w_in_ext = jnp.concatenate([w_in, zpad, _swap_rope_halves(w_in[:, c0:]), zpad], axis=1)
            z = _mod_matmul(xs, norm_rows, 2 * i, mods, w_in_ext.astype(BF16)[None], 0, tm=PROJ_TM,
                            tn=MLA_ZCOLS // 2, out_dtype=F32, name="mla_in")
            wq = b_w_uq[j].reshape(MLA_Q_RANK, HEADS, MLA_NOPE + MLA_ROPE).transpose(1, 0, 2)
            hpad = jnp.zeros((HEADS, MLA_Q_RANK, HEAD_DIM - MLA_ROPE), F32)
            wq_cat = jnp.concatenate([wq, hpad], axis=-1)
            wq_swap = jnp.concatenate([_swap_rope_halves(wq[..., MLA_NOPE:]), hpad], axis=-1)
            wkv = b_w_ukv[j].reshape(MLA_KV_RANK, HEADS, 2 * HEAD_DIM).transpose(1, 0, 2)
            cos, sin = _rope_tables()
            q, kv, kpe = _mla_up(z, b_q_norm[j], b_kv_norm[j], cos, sin, wq_cat.astype(BF16), wq_swap.astype(BF16),
                                 wkv.astype(BF16))
            att, *cast_out = _mla_lat_attention(q, kv, kpe, cast)
            att_ctx = _mla_ctx_attention(q, kv, kpe)
            w_o = b_w_o_b
        else:
            assert cast is None
            bs_full = jnp.repeat(c_bs[j].T, CHUNK, axis=1)
            att = _sgu(xs, norm_rows, 2 * i, mods, c_w_in_b, c_b_in, c_ln_g[j], c_ln_b[j], c_ws[j].astype(BF16),
                       bs_full, j)
            att_ctx = None if last else att
            w_o = c_w_o_b

        if cast is not None:
            ffn_weights[i] = cast_out
        xs = _mixer_out(att, att_ctx, w_o, j, xs, mods)
        f_w_in_b, f_w_out_b = ffn_weights[i // CAST_LAYERS]
        xs = _ffn(xs, norm_rows, 2 * i + 1, mods, f_w_in_b, f_w_out_b, i % CAST_LAYERS, f_conv_w, f_conv_b, i, final_g,
                  rows=rows_out, final_norm=last)

    return xs.reshape(BATCH, SEQ, D)
```

```python
import functools

import jax
import jax.numpy as jnp
import numpy as np
from jax import lax
from jax.experimental import pallas as pl
from jax.experimental.pallas import tpu as pltpu

F32 = jnp.float32
BF16 = jnp.bfloat16

D = 2048
BATCH = 4
SEQ = 2048
DEPTH = 4
GRID_W = 64
GRID_H = SEQ // GRID_W
CTX_LEN = 256
N_MIXERS = 3
EPS = 1e-6
ROPE_THETA = 10000.0
HEADS = 16
HEAD_DIM = 128
WIN_H = 8
WIN_W = 16
MLA_NOPE = 128
MLA_ROPE = 64
MLA_Q_RANK = 512
MLA_KV_RANK = 512
CHUNK = 128
SG_GROUPS = 16
D_FF = 5632
CONV_W = 3

T_LAT = BATCH * SEQ
T_CTX = BATCH * CTX_LEN
T_ALL = T_LAT + T_CTX
CTX_ROW_BLOCK0 = T_LAT // CTX_LEN
MASK_VALUE = -1e30
VMEM_LIMIT = 56 * 1024 * 1024
LANES = 128
BF16_ROWS = 16
PROLOGUE_ROWS = 128
PROJ_TM = 1024
PROJ_TN = 1024


def _params(*semantics):
    return pltpu.CompilerParams(dimension_semantics=semantics, vmem_limit_bytes=VMEM_LIMIT)


def _mod_row(tm):
    per_batch = SEQ // tm
    return lambda m: jnp.minimum(m // per_batch, BATCH)


def _resident(block_shape, index_map):
    return pl.BlockSpec(block_shape, index_map, pipeline_mode=pl.Buffered(1))


def _ada_kernel(s_ref, w_ref, b_ref, o_ref):
    s = s_ref[...]
    s = s * jax.nn.sigmoid(s)
    o_ref[...] = jnp.dot(s.astype(BF16), w_ref[...].astype(BF16), preferred_element_type=F32) + b_ref[...]


def _ada(s, ada_w, ada_b):
    tn = 1024
    n6 = 6 * D
    return pl.pallas_call(
        _ada_kernel,
        grid=(DEPTH, n6 // tn),
        in_specs=[
            pl.BlockSpec((8, D), lambda i, n: (0, 0)),
            pl.BlockSpec((None, D, tn), lambda i, n: (i, 0, n)),
            pl.BlockSpec((None, 1, tn), lambda i, n: (i, 0, n)),
        ],
        out_specs=pl.BlockSpec((None, 8, tn), lambda i, n: (i, 0, n)),
        out_shape=jax.ShapeDtypeStruct((DEPTH, 8, n6), F32),
        compiler_params=_params("parallel", "parallel"),
        name="ada",
    )(s, ada_w, ada_b.reshape(DEPTH, 1, n6))


def _gelu_exact(z):
    return 0.5 * z * (1.0 + lax.erf(z * np.float32(np.sqrt(0.5))))


def _modulate(xf, g, shift, scale):
    y = xf * lax.rsqrt(jnp.mean(xf * xf, axis=-1, keepdims=True) + EPS)
    return y * (g * (1.0 + scale)) + shift


def _modulate_rows(x_ref, h_ref, h_row0, n_rows, g, shift, scale):
    def body(c, carry):
        r = pl.multiple_of(c * PROLOGUE_ROWS, PROLOGUE_ROWS)
        h = _modulate(x_ref[pl.ds(r, PROLOGUE_ROWS), :], g, shift, scale)
        h_ref[pl.ds(h_row0 + r, PROLOGUE_ROWS), :] = h.astype(BF16)
        return carry
    lax.fori_loop(0, n_rows // PROLOGUE_ROWS, body, 0)


def _mod_mm_kernel(*refs, tm, split_x):
    x_refs, (g_ref, sh_ref, sc_ref, w_ref, o_ref, h_ref) = refs[:1 + split_x], refs[1 + split_x:]

    @pl.when(pl.program_id(1) == 0)
    def _():
        g, sh, sc = g_ref[...], sh_ref[...], sc_ref[...]
        if split_x:
            is_lat = pl.program_id(0) < T_LAT // tm
            pl.when(is_lat)(lambda: _modulate_rows(x_refs[0], h_ref, 0, tm, g, sh, sc))
            pl.when(jnp.logical_not(is_lat))(lambda: _modulate_rows(x_refs[1], h_ref, 0, tm, g, sh, sc))
        else:
            _modulate_rows(x_refs[0], h_ref, 0, tm, g, sh, sc)

    o_ref[...] = jnp.dot(h_ref[...], w_ref[...], preferred_element_type=F32).astype(o_ref.dtype)


def _row_tile_specs(x, tm, index_args):
    lat_tiles = T_LAT // tm

    def spec(index):
        return pl.BlockSpec((tm, D), lambda *g: (index(index_args(*g)), 0))

    if isinstance(x, tuple):
        return [spec(lambda m: jnp.minimum(m, lat_tiles - 1)), spec(lambda m: jnp.maximum(m - lat_tiles, 0))], list(x)
    return [spec(lambda m: m)], [x]


def _mod_matmul(x, norm_g, g_row, mods, w, w_layer, *, tm, tn, out_dtype, name):
    n = w.shape[2]
    brow = _mod_row(tm)
    x_specs, x_args = _row_tile_specs(x, tm, lambda m, j: m)
    return pl.pallas_call(
        functools.partial(_mod_mm_kernel, tm=tm, split_x=len(x_args) == 2),
        grid=(T_ALL // tm, n // tn),
        in_specs=x_specs + [
            pl.BlockSpec((None, 1, D), lambda m, j: (g_row, 0, 0)),
            pl.BlockSpec((None, 1, D), lambda m, j: (brow(m), 0, 0)),
            pl.BlockSpec((None, 1, D), lambda m, j: (brow(m), 0, 1)),
            pl.BlockSpec((None, D, tn), lambda m, j: (w_layer, 0, j)),
        ],
        out_specs=pl.BlockSpec((tm, tn), lambda m, j: (m, j)),
        out_shape=jax.ShapeDtypeStruct((T_ALL, n), out_dtype),
        scratch_shapes=[pltpu.VMEM((tm, D), BF16)],
        compiler_params=_params("parallel", "arbitrary"),
        name=name,
    )(*x_args, norm_g, mods, mods, w)


MIXER_OUT_COLS = 512


MIXER_OUT_TM = 512
MIXER_LAT_TILES = T_LAT // MIXER_OUT_TM


def _mixer_out_kernel(*refs, has_ctx, split_resid):
    a_refs, refs = refs[:1 + has_ctx], refs[1 + has_ctx:]
    w_ref, refs = refs[0], refs[1:]
    r_refs, (gate_ref, o_ref) = refs[:1 + split_resid], refs[1 + split_resid:]
    is_lat = pl.program_id(0) < MIXER_LAT_TILES
    a = jnp.where(is_lat, a_refs[0][...], a_refs[1][...]) if has_ctx else a_refs[0][...]
    for c in range(D // MIXER_OUT_COLS):
        cols = slice(c * MIXER_OUT_COLS, (c + 1) * MIXER_OUT_COLS)
        acc = jnp.dot(a, w_ref[:, cols], preferred_element_type=F32)
        r = jnp.where(is_lat, r_refs[0][:, cols], r_refs[1][:, cols]) if split_resid else r_refs[0][:, cols]
        o_ref[:, cols] = r + gate_ref[:, cols] * acc


def _mixer_out(a_lat, a_ctx, w, w_layer, resid, mods):
    tm = MIXER_OUT_TM
    brow = _mod_row(tm)
    has_ctx = a_ctx is not None
    rows = T_ALL if has_ctx else T_LAT
    in_specs = [pl.BlockSpec((tm, D), lambda m: (jnp.minimum(m, MIXER_LAT_TILES - 1), 0))]
    args = [a_lat]
    if has_ctx:
        ctx_tile0 = (a_ctx.shape[0] - T_CTX) // tm
        in_specs.append(pl.BlockSpec((tm, D), lambda m: (jnp.maximum(m - MIXER_LAT_TILES, 0) + ctx_tile0, 0)))
        args.append(a_ctx)
    r_specs, r_args = _row_tile_specs(resid, tm, lambda m: m)
    in_specs += [_resident((None, D, D), lambda m: (w_layer, 0, 0))] + r_specs + [
        pl.BlockSpec((None, 1, D), lambda m: (brow(m), 0, 2))]
    return pl.pallas_call(
        functools.partial(_mixer_out_kernel, has_ctx=has_ctx, split_resid=len(r_args) == 2),
        grid=(rows // tm,),
        in_specs=in_specs,
        out_specs=pl.BlockSpec((tm, D), lambda m: (m, 0)),
        out_shape=jax.ShapeDtypeStruct((rows, D), F32),
        compiler_params=_params("parallel"),
        name="mixer_out",
    )(*args, w, *r_args, mods)


FFN_TM = 1024
FFN_TF = 512
FFN_HALO = BF16_ROWS
FFN_OUT_COLS = 512
FFN_ROW_CHUNK = CTX_LEN
FFN_LOOKAHEAD = 4
FFN_ROW_SETS = 4
FFN_SLABS = FFN_TF // LANES


def _ffn_kernel(x_ref, xp_ref, xn_ref, g_ref, sh_ref, sc_ref, gate_ref, wa_ref, wg_ref, cwa_ref, cwg_ref,
                cba_ref, cbg_ref, wo_ref, fg_ref, o_ref, h_ref, ua_ref, ug_ref, act_ref, *, final_norm):
    tm, halo = FFN_TM, FFN_HALO
    f = pl.program_id(1)

    @pl.when(f == 0)
    def _():
        g, sh, sc = g_ref[...], sh_ref[...], sc_ref[...]
        h_ref[:halo, :] = _modulate(xp_ref[...], g, sh, sc).astype(BF16)
        h_ref[halo + tm:, :] = _modulate(xn_ref[...], g, sh, sc).astype(BF16)
        _modulate_rows(x_ref, h_ref, halo, tm, g, sh, sc)
        o_ref[...] = jnp.zeros_like(o_ref)

    ch, n_ch, sets = FFN_ROW_CHUNK, tm // FFN_ROW_CHUNK, FFN_ROW_SETS
    set_rows = ch // sets
    sub = lax.broadcasted_iota(jnp.int32, (set_rows, LANES), 0)

    def hidden(r):
        lo = 0 if r == 0 else halo + r * ch
        hi = tm + 2 * halo if r == n_ch - 1 else halo + (r + 1) * ch
        h = h_ref[lo:hi, :]
        ua = jnp.dot(h, wa_ref[...], preferred_element_type=F32)
        ug = jnp.dot(h, wg_ref[...], preferred_element_type=F32)
        for s in range(FFN_SLABS):
            ua_ref[s, lo:hi, :] = ua[:, s * LANES:(s + 1) * LANES]
            ug_ref[s, lo:hi, :] = ug[:, s * LANES:(s + 1) * LANES]

    def gate_and_project(r):
        base = halo + r * ch
        rows = slice(r * ch, (r + 1) * ch)
        row0 = pl.program_id(0) * tm + r * ch
        seq_mask = jnp.where(row0 < T_LAT, SEQ - 1, CTX_LEN - 1)
        keep_first_up = jnp.where((row0 & seq_mask) == 0, 0.0, 1.0).astype(F32)
        keep_last_dn = jnp.where(((row0 + ch) & seq_mask) == 0, 0.0, 1.0).astype(F32)
        first_up = jnp.where(sub == 0, keep_first_up, 1.0)
        last_dn = jnp.where(sub == set_rows - 1, keep_last_dn, 1.0)

        for s in range(FFN_SLABS):
            lanes = slice(s * LANES, (s + 1) * LANES)
            for j in range(sets):
                def conv(u_ref, cw_ref, cb_ref):
                    up = u_ref[s, pl.ds(base + j - 1, set_rows, stride=sets), :]
                    cur = u_ref[s, pl.ds(base + j, set_rows, stride=sets), :]
                    dn = u_ref[s, pl.ds(base + j + 1, set_rows, stride=sets), :]
                    if j == 0:
                        up = up * first_up
                    if j == sets - 1:
                        dn = dn * last_dn
                    return (up * cw_ref[0:1, lanes] + cur * cw_ref[1:2, lanes] + dn * cw_ref[2:3, lanes]
                            + cb_ref[:, lanes])

                a = conv(ua_ref, cwa_ref, cba_ref)
                gg = conv(ug_ref, cwg_ref, cbg_ref)
                act_ref[s, pl.ds(r * ch + j, set_rows, stride=sets), :] = a * (gg * jax.nn.sigmoid(gg))

        act = jnp.concatenate([act_ref[s, rows, :] for s in range(FFN_SLABS)], axis=1).astype(BF16)
        for c in range(D // FFN_OUT_COLS):
            cols = slice(c * FFN_OUT_COLS, (c + 1) * FFN_OUT_COLS)
            o_ref[rows, cols] += jnp.dot(act, wo_ref[:, cols], preferred_element_type=F32)

    for r in range(min(FFN_LOOKAHEAD, n_ch)):
        hidden(r)
    for r in range(n_ch):
        if r + FFN_LOOKAHEAD < n_ch:
            hidden(r + FFN_LOOKAHEAD)
        gate_and_project(r)

    @pl.when(f == pl.num_programs(1) - 1)
    def _():
        gate = gate_ref[...]

        def body(c, carry):
            rows = pl.ds(pl.multiple_of(c * PROLOGUE_ROWS, PROLOGUE_ROWS), PROLOGUE_ROWS)
            y = x_ref[rows, :] + gate * o_ref[rows, :]
            if final_norm:
                y = (y * lax.rsqrt(jnp.mean(y * y, axis=-1, keepdims=True) + EPS)) * fg_ref[...]
            o_ref[rows, :] = y
            return carry
        lax.fori_loop(0, tm // PROLOGUE_ROWS, body, 0)


def _ffn(x, norm_g, g_row, mods, w_in, w_out, w_layer, conv_w, conv_b, layer, final_g, *, rows, final_norm):
    tm, tf, halo = FFN_TM, FFN_TF, FFN_HALO
    nf = D_FF // tf
    brow = _mod_row(tm)
    last_halo = rows // halo - 1
    cb = conv_b.reshape(DEPTH, 1, 2 * D_FF)
    return pl.pallas_call(
        functools.partial(_ffn_kernel, final_norm=final_norm),
        grid=(rows // tm, nf),
        in_specs=[
            pl.BlockSpec((tm, D), lambda m, f: (m, 0), pipeline_mode=pl.Buffered(1)),
            pl.BlockSpec((halo, D), lambda m, f: (jnp.maximum(m * (tm // halo) - 1, 0), 0)),
            pl.BlockSpec((halo, D), lambda m, f: (jnp.minimum((m + 1) * (tm // halo), last_halo), 0)),
            pl.BlockSpec((None, 1, D), lambda m, f: (g_row, 0, 0)),
            pl.BlockSpec((None, 1, D), lambda m, f: (brow(m), 0, 3)),
            pl.BlockSpec((None, 1, D), lambda m, f: (brow(m), 0, 4)),
            pl.BlockSpec((None, 1, D), lambda m, f: (brow(m), 0, 5)),
            pl.BlockSpec((None, D, tf), lambda m, f: (w_layer, 0, f)),
            pl.BlockSpec((None, D, tf), lambda m, f: (w_layer, 0, f + nf)),
            pl.BlockSpec((None, CONV_W, tf), lambda m, f: (layer, 0, f)),
            pl.BlockSpec((None, CONV_W, tf), lambda m, f: (layer, 0, f + nf)),
            pl.BlockSpec((None, 1, tf), lambda m, f: (layer, 0, f)),
            pl.BlockSpec((None, 1, tf), lambda m, f: (layer, 0, f + nf)),
            pl.BlockSpec((None, tf, D), lambda m, f: (w_layer, f, 0)),
            pl.BlockSpec((1, D), lambda m, f: (0, 0)),
        ],
        out_specs=pl.BlockSpec((tm, D), lambda m, f: (m, 0)),
        out_shape=jax.ShapeDtypeStruct((rows, D), F32),
        scratch_shapes=[
            pltpu.VMEM((tm + 2 * halo, D), BF16),
            pltpu.VMEM((FFN_SLABS, tm + 2 * halo, LANES), F32),
            pltpu.VMEM((FFN_SLABS, tm + 2 * halo, LANES), F32),
            pltpu.VMEM((FFN_SLABS, tm, LANES), F32),
        ],
        compiler_params=_params("parallel", "arbitrary"),
        name="ffn",
    )(x, x, x, norm_g, mods, mods, mods, w_in, w_in, conv_w, conv_w, cb, cb, w_out, final_g.reshape(1, D))


NA_QROWS = 4
NA_QBLOCK = NA_QROWS * GRID_W
NA_SUB = 8
NA_BAND_ROWS = 12
NA_BAND = NA_BAND_ROWS * GRID_W


CAST_LAYERS = 2
CAST_STEPS = BATCH * HEADS
CAST_IN_ROWS = D // CAST_STEPS
CAST_OUT_ROWS = 2 * D_FF // CAST_STEPS
assert CAST_IN_ROWS % BF16_ROWS == 0 and CAST_OUT_ROWS % BF16_ROWS == 0 and DEPTH % CAST_LAYERS == 0


def _cast_job(cast, step_of):
    if cast is None:
        return [], [], [], []
    f_w_in, f_w_out, pair = cast
    w_in_spec = pl.BlockSpec((CAST_LAYERS, CAST_IN_ROWS, 2 * D_FF), lambda *g: (pair, step_of(*g), 0))
    w_out_spec = pl.BlockSpec((CAST_LAYERS, CAST_OUT_ROWS, D), lambda *g: (pair, step_of(*g) // 2, 0))
    shapes = [jax.ShapeDtypeStruct((CAST_LAYERS,) + f_w_in.shape[1:], BF16),
              jax.ShapeDtypeStruct((CAST_LAYERS,) + f_w_out.shape[1:], BF16)]
    return [w_in_spec, w_out_spec], [_cast_out_spec(w_in_spec), _cast_out_spec(w_out_spec)], shapes, [f_w_in, f_w_out]


def _cast_out_spec(spec):
    return pl.BlockSpec(spec.block_shape, lambda *g: (0,) + tuple(spec.index_map(*g))[1:])


def _run_cast_job(refs, n_in, with_cast):
    if not with_cast:
        return refs
    w_in_src, w_out_src, o_ref, w_in_dst, w_out_dst = refs[n_in:n_in + 5]
    w_in_dst[...] = w_in_src[...].astype(BF16)
    w_out_dst[...] = w_out_src[...].astype(BF16)
    return refs[:n_in] + (o_ref,) + refs[n_in + 5:]


ATTN_LOOKAHEAD = 2


def _nt_dot(a, b):
    return lax.dot_general(a, b, (((1,), (1,)), ((), ())), preferred_element_type=F32)


NA_QBLOCKS = GRID_H // NA_QROWS
NA_SCALE = HEAD_DIM ** -0.5
NA_SCALE_LOG2E = np.float32(NA_SCALE * np.log2(np.e))
NA_KINDS = ((lambda j: 0, WIN_H - 1), (lambda j: j, WIN_H // 2 - 1), (lambda j: NA_BAND_ROWS - WIN_H, -1))


def _na_band_start(blk):
    return jnp.clip(blk * NA_QROWS - WIN_H // 2, 0, GRID_H - NA_BAND_ROWS)


def _na_lat_kernel(*refs, with_cast):
    q_ref, k_ref, v_ref, kc_ref, vc_ref, bias_ref, o_ref = _run_cast_job(refs, 6, with_cast)
    i = pl.program_id(2)
    kc = kc_ref[...]
    vc = vc_ref[...]

    def key_start(sb):
        return pl.multiple_of(_na_band_start(i * NA_SUB + sb) * GRID_W, NA_QROWS * GRID_W)

    def logits(sb):
        blk = i * NA_SUB + sb
        kind = jnp.where(blk == 0, 0, jnp.where(blk == NA_QBLOCKS - 1, 2, 1))
        q = q_ref[sb * NA_QBLOCK:(sb + 1) * NA_QBLOCK, :]
        return _nt_dot(q, k_ref[pl.ds(key_start(sb), NA_BAND), :]) + bias_ref[kind], _nt_dot(q, kc)

    ahead = [logits(sb) for sb in range(min(ATTN_LOOKAHEAD, NA_SUB))]
    for sb in range(NA_SUB):
        t, tc = ahead.pop(0)
        if sb + ATTN_LOOKAHEAD < NA_SUB:
            ahead.append(logits(sb + ATTN_LOOKAHEAD))
        mx = jnp.maximum(jnp.max(t, axis=-1, keepdims=True), jnp.max(tc, axis=-1, keepdims=True))
        e = jnp.exp2((t - mx) * NA_SCALE_LOG2E)
        ec = jnp.exp2((tc - mx) * NA_SCALE_LOG2E)
        denom = jnp.sum(e, axis=-1, keepdims=True) + jnp.sum(ec, axis=-1, keepdims=True)
        o = (jnp.dot(e.astype(BF16), v_ref[pl.ds(key_start(sb), NA_BAND), :], preferred_element_type=F32)
             + jnp.dot(ec.astype(BF16), vc, preferred_element_type=F32))
        o_ref[sb * NA_QBLOCK:(sb + 1) * NA_QBLOCK, :] = (o / denom).astype(o_ref.dtype)


def _na_bias_table(rpb):
    n = rpb.shape[0]
    w = np.arange(GRID_W)[:, None]
    k = np.arange(GRID_W)[None, :]
    col0 = np.clip(w - WIN_W // 2, 0, GRID_W - WIN_W)
    in_win = (k >= col0) & (k < col0 + WIN_W)
    col_pick = (np.arange(2 * WIN_W - 1)[:, None, None] == (k - w + WIN_W - 1)[None]) & in_win[None]
    t = jnp.einsum("nrx,xwk->nrwk", rpb, col_pick.astype(np.float32), precision=lax.Precision.HIGHEST)
    t = jnp.where(jnp.asarray(in_win), t, MASK_VALUE) * np.float32(1.0 / NA_SCALE)
    return pl.pallas_call(
        _na_bias_blocks_kernel,
        grid=(n,),
        in_specs=[pl.BlockSpec((None, 2 * WIN_H - 1, GRID_W, GRID_W), lambda i: (i, 0, 0, 0))],
        out_specs=pl.BlockSpec((None, len(NA_KINDS), NA_QBLOCK, NA_BAND), lambda i: (i, 0, 0, 0)),
        out_shape=jax.ShapeDtypeStruct((n, len(NA_KINDS), NA_QBLOCK, NA_BAND), F32),
        compiler_params=_params("parallel"),
        name="na_bias_blocks",
    )(t)


def _na_bias_blocks_kernel(t_ref, o_ref):
    masked = jnp.full((GRID_W, GRID_W), MASK_VALUE / NA_SCALE, F32)
    for kind, (first_valid, offset0) in enumerate(NA_KINDS):
        for j in range(NA_QROWS):
            lo = first_valid(j)
            for c in range(NA_BAND_ROWS):
                tile = t_ref[c - j + offset0] if lo <= c < lo + WIN_H else masked
                o_ref[kind, j * GRID_W:(j + 1) * GRID_W, c * GRID_W:(c + 1) * GRID_W] = tile


def _na_lat_attention(qkv, bias_tab, layer, cast=None):
    blocks = SEQ // (NA_SUB * NA_QBLOCK)
    assert blocks == 1
    qb = NA_SUB * NA_QBLOCK
    cast_in, cast_out, cast_shape, cast_args = _cast_job(cast, lambda h, b, i: h * BATCH + b)
    return pl.pallas_call(
        functools.partial(_na_lat_kernel, with_cast=cast is not None),
        grid=(HEADS, BATCH, blocks),
        in_specs=[
            pl.BlockSpec((qb, HEAD_DIM), lambda h, b, i: (b * blocks + i, h)),
            pl.BlockSpec((SEQ, HEAD_DIM), lambda h, b, i: (b, HEADS + h)),
            pl.BlockSpec((SEQ, HEAD_DIM), lambda h, b, i: (b, 2 * HEADS + h)),
            pl.BlockSpec((CTX_LEN, HEAD_DIM), lambda h, b, i: (CTX_ROW_BLOCK0 + b, HEADS + h)),
            pl.BlockSpec((CTX_LEN, HEAD_DIM), lambda h, b, i: (CTX_ROW_BLOCK0 + b, 2 * HEADS + h)),
            pl.BlockSpec((None, len(NA_KINDS), NA_QBLOCK, NA_BAND), lambda h, b, i: (layer * HEADS + h, 0, 0, 0)),
        ] + cast_in,
        out_specs=[pl.BlockSpec((qb, HEAD_DIM), lambda h, b, i: (b * blocks + i, h))] + cast_out,
        out_shape=[jax.ShapeDtypeStruct((T_LAT, D), BF16)] + cast_shape,
        compiler_params=_params("arbitrary", "arbitrary", "arbitrary"),
        name="na_lat",
    )(qkv, qkv, qkv, qkv, qkv, bias_tab, *cast_args)


def _softmax_pv(s, v):
    e = jnp.exp(s - jnp.max(s, axis=-1, keepdims=True))
    o = jnp.dot(e.astype(BF16), v, preferred_element_type=F32)
    return o / jnp.sum(e, axis=-1, keepdims=True)


def _na_ctx_kernel(q_ref, k_ref, v_ref, o_ref):
    scale = np.float32(HEAD_DIM ** -0.5)
    for h in range(HEADS):
        cols = slice(h * HEAD_DIM, (h + 1) * HEAD_DIM)
        s = _nt_dot(q_ref[:, cols], k_ref[:, cols]) * scale
        o_ref[:, cols] = _softmax_pv(s, v_ref[:, cols]).astype(o_ref.dtype)


def _na_ctx_attention(qkv):
    return pl.pallas_call(
        _na_ctx_kernel,
        grid=(BATCH,),
        in_specs=[
            pl.BlockSpec((CTX_LEN, D), lambda b: (CTX_ROW_BLOCK0 + b, 0)),
            pl.BlockSpec((CTX_LEN, D), lambda b: (CTX_ROW_BLOCK0 + b, 1)),
            pl.BlockSpec((CTX_LEN, D), lambda b: (CTX_ROW_BLOCK0 + b, 2)),
        ],
        out_specs=pl.BlockSpec((CTX_LEN, D), lambda b: (b, 0)),
        out_shape=jax.ShapeDtypeStruct((T_CTX, D), BF16),
        compiler_params=_params("parallel"),
        name="na_ctx",
    )(qkv, qkv, qkv)


MLA_QK = 2 * HEAD_DIM
MLA_QBLOCK = 256
MLA_LOOKAHEAD = 1
MLA_SUB = 8
MLA_ZCOLS = MLA_Q_RANK + MLA_KV_RANK + 2 * HEAD_DIM
MLA_UP_TM = 512


def _rms(z, g):
    return (z * lax.rsqrt(jnp.mean(z * z, axis=-1, keepdims=True) + EPS)) * g


def _mla_up_kernel(z_ref, qn_ref, kvn_ref, cos_ref, sin_ref, wq_ref, wqs_ref, wkv_ref, q_ref, kv_ref, kpe_ref):
    cq = _rms(z_ref[:, :MLA_Q_RANK], qn_ref[...]).astype(BF16)
    ckv = _rms(z_ref[:, MLA_Q_RANK:MLA_Q_RANK + MLA_KV_RANK], kvn_ref[...]).astype(BF16)
    c0 = MLA_Q_RANK + MLA_KV_RANK
    cos = cos_ref[...]
    sin = sin_ref[...]
    kpe_ref[...] = (z_ref[:, c0:c0 + HEAD_DIM] * cos + z_ref[:, c0 + HEAD_DIM:] * sin).astype(BF16)
    for h in range(HEADS):
        q = jnp.dot(cq, wq_ref[h], preferred_element_type=F32)
        qs = jnp.dot(cq, wqs_ref[h], preferred_element_type=F32)
        q_ref[h, :, :HEAD_DIM] = q[:, :HEAD_DIM].astype(BF16)
        q_ref[h, :, HEAD_DIM:] = (q[:, HEAD_DIM:] * cos + qs * sin).astype(BF16)
        kv_ref[h] = jnp.dot(ckv, wkv_ref[h], preferred_element_type=F32).astype(BF16)


def _mla_up(z, q_norm, kv_norm, cos, sin, wq, wqs, wkv):
    tm = MLA_UP_TM
    return pl.pallas_call(
        _mla_up_kernel,
        grid=(T_ALL // tm,),
        in_specs=[
            pl.BlockSpec((tm, MLA_ZCOLS), lambda m: (m, 0)),
            pl.BlockSpec((1, MLA_Q_RANK), lambda m: (0, 0)),
            pl.BlockSpec((1, MLA_KV_RANK), lambda m: (0, 0)),
            pl.BlockSpec((tm, HEAD_DIM), lambda m: (m, 0)),
            pl.BlockSpec((tm, HEAD_DIM), lambda m: (m, 0)),
            _resident((HEADS, MLA_Q_RANK, MLA_QK), lambda m: (0, 0, 0)),
            _resident((HEADS, MLA_Q_RANK, HEAD_DIM), lambda m: (0, 0, 0)),
            _resident((HEADS, MLA_KV_RANK, 2 * HEAD_DIM), lambda m: (0, 0, 0)),
        ],
        out_specs=[
            pl.BlockSpec((HEADS, tm, MLA_QK), lambda m: (0, m, 0)),
            pl.BlockSpec((HEADS, tm, 2 * HEAD_DIM), lambda m: (0, m, 0)),
            pl.BlockSpec((tm, HEAD_DIM), lambda m: (m, 0)),
        ],
        out_shape=[
            jax.ShapeDtypeStruct((HEADS, T_ALL, MLA_QK), BF16),
            jax.ShapeDtypeStruct((HEADS, T_ALL, 2 * HEAD_DIM), BF16),
            jax.ShapeDtypeStruct((T_ALL, HEAD_DIM), BF16),
        ],
        compiler_params=_params("parallel"),
        name="mla_up",
    )(z, q_norm.reshape(1, -1), kv_norm.reshape(1, -1), cos, sin, wq, wqs, wkv)


MLA_SCALE = np.float32((MLA_NOPE + MLA_ROPE) ** -0.5)
MLA_SCALE_LOG2E = np.float32((MLA_NOPE + MLA_ROPE) ** -0.5 * np.log2(np.e))


def _mla_lat_kernel(*refs, with_cast):
    q_ref, kn_ref, v_ref, kpe_ref, knc_ref, vc_ref, kpec_ref, o_ref, kcat_ref = _run_cast_job(refs, 7, with_cast)

    @pl.when(pl.program_id(2) == 0)
    def _():
        kcat_ref[:SEQ, :HEAD_DIM] = kn_ref[...]
        kcat_ref[:SEQ, HEAD_DIM:] = kpe_ref[...]
        kcat_ref[SEQ:, :HEAD_DIM] = knc_ref[...]
        kcat_ref[SEQ:, HEAD_DIM:] = kpec_ref[...]

    def scores(sb):
        return _nt_dot(q_ref[sb * MLA_QBLOCK:(sb + 1) * MLA_QBLOCK, :], kcat_ref[...])

    ahead = [scores(sb) for sb in range(min(MLA_LOOKAHEAD, MLA_SUB))]
    for sb in range(MLA_SUB):
        rows = slice(sb * MLA_QBLOCK, (sb + 1) * MLA_QBLOCK)
        s = ahead.pop(0)
        if sb + MLA_LOOKAHEAD < MLA_SUB:
            ahead.append(scores(sb + MLA_LOOKAHEAD))
        e = jnp.exp2((s - jnp.max(s, axis=-1, keepdims=True)) * MLA_SCALE_LOG2E)
        o = (jnp.dot(e[:, :SEQ].astype(BF16), v_ref[...], preferred_element_type=F32)
             + jnp.dot(e[:, SEQ:].astype(BF16), vc_ref[...], preferred_element_type=F32))
        o_ref[rows, :] = (o / jnp.sum(e, axis=-1, keepdims=True)).astype(o_ref.dtype)


def _mla_lat_attention(q, kv, kpe, cast=None):
    qb = MLA_SUB * MLA_QBLOCK
    blocks = SEQ // qb
    assert blocks == 1
    ctx_q = CTX_ROW_BLOCK0
    cast_in, cast_out, cast_shape, cast_args = _cast_job(cast, lambda b, h, i: b * HEADS + h)
    return pl.pallas_call(
        functools.partial(_mla_lat_kernel, with_cast=cast is not None),
        grid=(BATCH, HEADS, blocks),
        in_specs=[
            pl.BlockSpec((None, qb, MLA_QK), lambda b, h, i: (h, b * blocks + i, 0)),
            pl.BlockSpec((None, SEQ, HEAD_DIM), lambda b, h, i: (h, b, 0)),
            pl.BlockSpec((None, SEQ, HEAD_DIM), lambda b, h, i: (h, b, 1)),
            pl.BlockSpec((SEQ, HEAD_DIM), lambda b, h, i: (b, 0)),
            pl.BlockSpec((None, CTX_LEN, HEAD_DIM), lambda b, h, i: (h, ctx_q + b, 0)),
            pl.BlockSpec((None, CTX_LEN, HEAD_DIM), lambda b, h, i: (h, ctx_q + b, 1)),
            pl.BlockSpec((CTX_LEN, HEAD_DIM), lambda b, h, i: (ctx_q + b, 0)),
        ] + cast_in,
        out_specs=[pl.BlockSpec((qb, HEAD_DIM), lambda b, h, i: (b * blocks + i, h))] + cast_out,
        out_shape=[jax.ShapeDtypeStruct((T_LAT, D), BF16)] + cast_shape,
        scratch_shapes=[pltpu.VMEM((SEQ + CTX_LEN, MLA_QK), BF16)],
        compiler_params=_params("arbitrary", "arbitrary", "arbitrary"),
        name="mla_lat",
    )(q, kv, kv, kpe, kv, kv, kpe, *cast_args)


def _mla_ctx_kernel(q_ref, kv_ref, kpe_ref, o_ref):
    kpe = kpe_ref[...]
    for h in range(HEADS):
        kcat = jnp.concatenate([kv_ref[h, :, :HEAD_DIM], kpe], axis=1)
        s = _nt_dot(q_ref[h], kcat) * MLA_SCALE
        o_ref[:, h * HEAD_DIM:(h + 1) * HEAD_DIM] = _softmax_pv(s, kv_ref[h, :, HEAD_DIM:]).astype(o_ref.dtype)


def _mla_ctx_attention(q, kv, kpe):
    ctx_q = CTX_ROW_BLOCK0
    return pl.pallas_call(
        _mla_ctx_kernel,
        grid=(BATCH,),
        in_specs=[
            pl.BlockSpec((HEADS, CTX_LEN, MLA_QK), lambda b: (0, ctx_q + b, 0)),
            pl.BlockSpec((HEADS, CTX_LEN, 2 * HEAD_DIM), lambda b: (0, ctx_q + b, 0)),
            pl.BlockSpec((CTX_LEN, HEAD_DIM), lambda b: (ctx_q + b, 0)),
        ],
        out_specs=pl.BlockSpec((CTX_LEN, D), lambda b: (b, 0)),
        out_shape=jax.ShapeDtypeStruct((T_CTX, D), BF16),
        compiler_params=_params("parallel"),
        name="mla_ctx",
    )(q, kv, kpe)


def _rope_tables():
    half = MLA_ROPE // 2
    freqs = ROPE_THETA ** (-jnp.arange(0, half, 2, dtype=F32) / half)
    t = jnp.arange(SEQ)
    rows = (t // GRID_W).astype(F32)[:, None] * freqs
    cols = (t % GRID_W).astype(F32)[:, None] * freqs
    cr, sr, cc, sn = jnp.cos(rows), jnp.sin(rows), jnp.cos(cols), jnp.sin(cols)
    pad = jnp.zeros((SEQ, HEAD_DIM - MLA_ROPE), F32)
    cos = jnp.concatenate([cr, cr, cc, cc, pad], axis=1)
    sin = jnp.concatenate([-sr, sr, -sn, sn, pad], axis=1)
    cos_c = jnp.concatenate([jnp.ones((T_CTX, MLA_ROPE), F32), jnp.zeros((T_CTX, HEAD_DIM - MLA_ROPE), F32)], axis=1)
    return (jnp.concatenate([jnp.tile(cos, (BATCH, 1)), cos_c], axis=0),
            jnp.concatenate([jnp.tile(sin, (BATCH, 1)), jnp.zeros((T_CTX, HEAD_DIM), F32)], axis=0))


def _swap_rope_halves(w):
    q = MLA_ROPE // 4
    return jnp.concatenate([w[..., q:2 * q], w[..., :q], w[..., 3 * q:], w[..., 2 * q:3 * q]], axis=-1)


SGU_TM = 512
SGU_COLS = 512


def _sgu_kernel(x_ref, g_ref, sh_ref, sc_ref, w_ref, b_ref, lng_ref, lnb_ref, ws_ref, bs_ref, o_ref,
                h_ref, u_ref, v_ref):
    _modulate_rows(x_ref, h_ref, 0, SGU_TM, g_ref[...], sh_ref[...], sc_ref[...])
    h = h_ref[...]
    for nc in range(2 * D // SGU_COLS):
        cols = slice(nc * SGU_COLS, (nc + 1) * SGU_COLS)
        acc = _gelu_exact(jnp.dot(h, w_ref[:, cols], preferred_element_type=F32) + b_ref[:, cols])
        if nc < D // SGU_COLS:
            u_ref[:, cols] = acc
        else:
            v_ref[:, nc * SGU_COLS - D:(nc + 1) * SGU_COLS - D] = acc
    for r in range(SGU_TM // CHUNK):
        rows = slice(r * CHUNK, (r + 1) * CHUNK)
        v = v_ref[rows, :]
        mu = jnp.mean(v, axis=-1, keepdims=True)
        vc = v - mu
        var = jnp.mean(vc * vc, axis=-1, keepdims=True)
        vn = ((vc * lax.rsqrt(var + EPS)) * lng_ref[...] + lnb_ref[...]).astype(BF16)
        for g in range(SG_GROUPS):
            cols = slice(g * CHUNK, (g + 1) * CHUNK)
            mix = jnp.dot(ws_ref[g], vn[:, cols], preferred_element_type=F32) + bs_ref[:, cols]
            o_ref[rows, cols] = (u_ref[rows, cols] * mix).astype(o_ref.dtype)


def _sgu(x, norm_g, g_row, mods, w_in, b_in, ln_g, ln_b, ws, bs_full, layer):
    tm = SGU_TM
    brow = _mod_row(tm)
    return pl.pallas_call(
        _sgu_kernel,
        grid=(T_ALL // tm,),
        in_specs=[
            pl.BlockSpec((tm, D), lambda m: (m, 0)),
            pl.BlockSpec((None, 1, D), lambda m: (g_row, 0, 0)),
            pl.BlockSpec((None, 1, D), lambda m: (brow(m), 0, 0)),
            pl.BlockSpec((None, 1, D), lambda m: (brow(m), 0, 1)),
            _resident((None, D, 2 * D), lambda m: (layer, 0, 0)),
            pl.BlockSpec((None, 1, 2 * D), lambda m: (layer, 0, 0)),
            pl.BlockSpec((1, D), lambda m: (0, 0)),
            pl.BlockSpec((1, D), lambda m: (0, 0)),
            pl.BlockSpec((SG_GROUPS, CHUNK, CHUNK), lambda m: (0, 0, 0)),
            pl.BlockSpec((CHUNK, D), lambda m: (0, 0)),
        ],
        out_specs=pl.BlockSpec((tm, D), lambda m: (m, 0)),
        out_shape=jax.ShapeDtypeStruct((T_ALL, D), BF16),
        scratch_shapes=[pltpu.VMEM((tm, D), BF16), pltpu.VMEM((tm, D), F32), pltpu.VMEM((tm, D), F32)],
        compiler_params=_params("parallel"),
        name="sgu",
    )(x, norm_g, mods, mods, w_in, b_in.reshape(b_in.shape[0], 1, 2 * D), ln_g.reshape(1, D), ln_b.reshape(1, D), ws,
      bs_full)


def kernel(x, c, ctx, c_ctx, ada_w, ada_b, norm_g, final_g, a_w_qkv, a_w_o, a_rpb, b_w_in, b_q_norm, b_kv_norm,
           b_w_uq, b_w_ukv, b_w_o, c_w_in, c_b_in, c_ln_g, c_ln_b, c_ws, c_bs, c_w_o, f_w_in, f_conv_w, f_conv_b,
           f_w_out):
    xs = (x.reshape(T_LAT, D), ctx.reshape(T_CTX, D))
    cond = jnp.concatenate([c, c_ctx[None, :], jnp.zeros((8 - BATCH - 1, D), F32)], axis=0)
    mods_all = _ada(cond, ada_w, ada_b).reshape(DEPTH, 8, 1, 6 * D)
    norm_rows = norm_g.reshape(DEPTH * 2, 1, D)
    na_bias = _na_bias_table(a_rpb.reshape(-1, 2 * WIN_H - 1, 2 * WIN_W - 1))

    a_w_qkv_b, a_w_o_b = a_w_qkv.astype(BF16), a_w_o.astype(BF16)
    b_w_o_b, c_w_in_b, c_w_o_b = b_w_o.astype(BF16), c_w_in.astype(BF16), c_w_o.astype(BF16)
    ffn_weights = {}

    for i in range(DEPTH):
        kind, j = i % N_MIXERS, i // N_MIXERS
        last = i == DEPTH - 1
        mods = mods_all[i]
        rows_out = T_LAT if last else T_ALL
        cast = (f_w_in, f_w_out, i) if i < DEPTH // CAST_LAYERS else None

        if kind == 0:
            qkv = _mod_matmul(xs, norm_rows, 2 * i, mods, a_w_qkv_b, j, tm=PROJ_TM, tn=PROJ_TN, out_dtype=BF16,
                              name="na_qkv")
            att, *cast_out = _na_lat_attention(qkv, na_bias, j, cast)
            att_ctx = None if last else _na_ctx_attention(qkv)
            w_o = a_w_o_b
        elif kind == 1:
            w_in = b_w_in[j]
            c0 = MLA_Q_RANK + MLA_KV_RANK
            zpad = jnp.zeros((D, HEAD_DIM - MLA_ROPE), F32)
            w_in_ext = jnp.concatenate([w_in, zpad, _swap_rope_halves(w_in[:, c0:]), zpad], axis=1)
            z = _mod_matmul(xs, norm_rows, 2 * i, mods, w_in_ext.astype(BF16)[None], 0, tm=PROJ_TM,
                            tn=MLA_ZCOLS // 2, out_dtype=F32, name="mla_in")
            wq = b_w_uq[j].reshape(MLA_Q_RANK, HEADS, MLA_NOPE + MLA_ROPE).transpose(1, 0, 2)
            hpad = jnp.zeros((HEADS, MLA_Q_RANK, HEAD_DIM - MLA_ROPE), F32)
            wq_cat = jnp.concatenate([wq, hpad], axis=-1)
            wq_swap = jnp.concatenate([_swap_rope_halves(wq[..., MLA_NOPE:]), hpad], axis=-1)
            wkv = b_w_ukv[j].reshape(MLA_KV_RANK, HEADS, 2 * HEAD_DIM).transpose(1, 0, 2)
            cos, sin = _rope_tables()
            q, kv, kpe = _mla_up(z, b_q_norm[j], b_kv_norm[j], cos, sin, wq_cat.astype(BF16), wq_swap.astype(BF16),
                                 wkv.astype(BF16))
            att, *cast_out = _mla_lat_attention(q, kv, kpe, cast)
            att_ctx = _mla_ctx_attention(q, kv, kpe)
            w_o = b_w_o_b
        else:
            assert cast is None
            bs_full = jnp.repeat(c_bs[j].T, CHUNK, axis=1)
            att = _sgu(xs, norm_rows, 2 * i, mods, c_w_in_b, c_b_in, c_ln_g[j], c_ln_b[j], c_ws[j].astype(BF16),
                       bs_full, j)
            att_ctx = None if last else att
            w_o = c_w_o_b

        if cast is not None:
            ffn_weights[i] = cast_out
        xs = _mixer_out(att, att_ctx, w_o, j, xs, mods)
        f_w_in_b, f_w_out_b = ffn_weights[i // CAST_LAYERS]
        xs = _ffn(xs, norm_rows, 2 * i + 1, mods, f_w_in_b, f_w_out_b, i % CAST_LAYERS, f_conv_w, f_conv_b, i, final_g,
                  rows=rows_out, final_norm=last)

    return xs.reshape(BATCH, SEQ, D)
```

```python
import functools

import jax
import jax.numpy as jnp
import numpy as np
from jax import lax
from jax.experimental import pallas as pl
from jax.experimental.pallas import tpu as pltpu

F32 = jnp.float32
BF16 = jnp.bfloat16

D = 2048
BATCH = 4
SEQ = 2048
DEPTH = 4
GRID_W = 64
GRID_H = SEQ // GRID_W
CTX_LEN = 256
N_MIXERS = 3
EPS = 1e-6
ROPE_THETA = 10000.0
HEADS = 16
HEAD_DIM = 128
WIN_H = 8
WIN_W = 16
MLA_NOPE = 128
MLA_ROPE = 64
MLA_Q_RANK = 512
MLA_KV_RANK = 512
CHUNK = 128
SG_GROUPS = 16
D_FF = 5632
CONV_W = 3

T_LAT = BATCH * SEQ
T_CTX = BATCH * CTX_LEN
T_ALL = T_LAT + T_CTX
CTX_ROW_BLOCK0 = T_LAT // CTX_LEN
MASK_VALUE = -1e30
VMEM_LIMIT = 56 * 1024 * 1024
LANES = 128
BF16_ROWS = 16
PROLOGUE_ROWS = 128
PROJ_TM = 1024
PROJ_TN = 1024


def _params(*semantics):
    return pltpu.CompilerParams(dimension_semantics=semantics, vmem_limit_bytes=VMEM_LIMIT)


def _mod_row(tm):
    per_batch = SEQ // tm
    return lambda m: jnp.minimum(m // per_batch, BATCH)


def _resident(block_shape, index_map):
    return pl.BlockSpec(block_shape, index_map, pipeline_mode=pl.Buffered(1))


def _ada_kernel(s_ref, w_ref, b_ref, o_ref):
    s = s_ref[...]
    s = s * jax.nn.sigmoid(s)
    o_ref[...] = jnp.dot(s.astype(BF16), w_ref[...].astype(BF16), preferred_element_type=F32) + b_ref[...]


def _ada(s, ada_w, ada_b):
    tn = 1024
    n6 = 6 * D
    return pl.pallas_call(
        _ada_kernel,
        grid=(DEPTH, n6 // tn),
        in_specs=[
            pl.BlockSpec((8, D), lambda i, n: (0, 0)),
            pl.BlockSpec((None, D, tn), lambda i, n: (i, 0, n)),
            pl.BlockSpec((None, 1, tn), lambda i, n: (i, 0, n)),
        ],
        out_specs=pl.BlockSpec((None, 8, tn), lambda i, n: (i, 0, n)),
        out_shape=jax.ShapeDtypeStruct((DEPTH, 8, n6), F32),
        compiler_params=_params("parallel", "parallel"),
        name="ada",
    )(s, ada_w, ada_b.reshape(DEPTH, 1, n6))


def _gelu_exact(z):
    return 0.5 * z * (1.0 + lax.erf(z * np.float32(np.sqrt(0.5))))


def _modulate(xf, g, shift, scale):
    y = xf * lax.rsqrt(jnp.mean(xf * xf, axis=-1, keepdims=True) + EPS)
    return y * (g * (1.0 + scale)) + shift


def _modulate_rows(x_ref, h_ref, h_row0, n_rows, g, shift, scale):
    def body(c, carry):
        r = pl.multiple_of(c * PROLOGUE_ROWS, PROLOGUE_ROWS)
        h = _modulate(x_ref[pl.ds(r, PROLOGUE_ROWS), :], g, shift, scale)
        h_ref[pl.ds(h_row0 + r, PROLOGUE_ROWS), :] = h.astype(BF16)
        return carry
    lax.fori_loop(0, n_rows // PROLOGUE_ROWS, body, 0)


def _mod_mm_kernel(*refs, tm, split_x):
    x_refs, (g_ref, sh_ref, sc_ref, w_ref, o_ref, h_ref) = refs[:1 + split_x], refs[1 + split_x:]

    @pl.when(pl.program_id(1) == 0)
    def _():
        g, sh, sc = g_ref[...], sh_ref[...], sc_ref[...]
        if split_x:
            is_lat = pl.program_id(0) < T_LAT // tm
            pl.when(is_lat)(lambda: _modulate_rows(x_refs[0], h_ref, 0, tm, g, sh, sc))
            pl.when(jnp.logical_not(is_lat))(lambda: _modulate_rows(x_refs[1], h_ref, 0, tm, g, sh, sc))
        else:
            _modulate_rows(x_refs[0], h_ref, 0, tm, g, sh, sc)

    o_ref[...] = jnp.dot(h_ref[...], w_ref[...], preferred_element_type=F32).astype(o_ref.dtype)


def _row_tile_specs(x, tm, index_args):
    lat_tiles = T_LAT // tm

    def spec(index):
        return pl.BlockSpec((tm, D), lambda *g: (index(index_args(*g)), 0))

    if isinstance(x, tuple):
        return [spec(lambda m: jnp.minimum(m, lat_tiles - 1)), spec(lambda m: jnp.maximum(m - lat_tiles, 0))], list(x)
    return [spec(lambda m: m)], [x]


def _mod_matmul(x, norm_g, g_row, mods, w, w_layer, *, tm, tn, out_dtype, name):
    n = w.shape[2]
    brow = _mod_row(tm)
    x_specs, x_args = _row_tile_specs(x, tm, lambda m, j: m)
    return pl.pallas_call(
        functools.partial(_mod_mm_kernel, tm=tm, split_x=len(x_args) == 2),
        grid=(T_ALL // tm, n // tn),
        in_specs=x_specs + [
            pl.BlockSpec((None, 1, D), lambda m, j: (g_row, 0, 0)),
            pl.BlockSpec((None, 1, D), lambda m, j: (brow(m), 0, 0)),
            pl.BlockSpec((None, 1, D), lambda m, j: (brow(m), 0, 1)),
            pl.BlockSpec((None, D, tn), lambda m, j: (w_layer, 0, j)),
        ],
        out_specs=pl.BlockSpec((tm, tn), lambda m, j: (m, j)),
        out_shape=jax.ShapeDtypeStruct((T_ALL, n), out_dtype),
        scratch_shapes=[pltpu.VMEM((tm, D), BF16)],
        compiler_params=_params("parallel", "arbitrary"),
        name=name,
    )(*x_args, norm_g, mods, mods, w)


MIXER_OUT_COLS = 512


MIXER_OUT_TM = 512
MIXER_LAT_TILES = T_LAT // MIXER_OUT_TM


def _mixer_out_kernel(*refs, has_ctx, split_resid):
    a_refs, refs = refs[:1 + has_ctx], refs[1 + has_ctx:]
    w_ref, refs = refs[0], refs[1:]
    r_refs, (gate_ref, o_ref) = refs[:1 + split_resid], refs[1 + split_resid:]
    is_lat = pl.program_id(0) < MIXER_LAT_TILES
    a = jnp.where(is_lat, a_refs[0][...], a_refs[1][...]) if has_ctx else a_refs[0][...]
    for c in range(D // MIXER_OUT_COLS):
        cols = slice(c * MIXER_OUT_COLS, (c + 1) * MIXER_OUT_COLS)
        acc = jnp.dot(a, w_ref[:, cols], preferred_element_type=F32)
        r = jnp.where(is_lat, r_refs[0][:, cols], r_refs[1][:, cols]) if split_resid else r_refs[0][:, cols]
        o_ref[:, cols] = r + gate_ref[:, cols] * acc


def _mixer_out(a_lat, a_ctx, w, w_layer, resid, mods):
    tm = MIXER_OUT_TM
    brow = _mod_row(tm)
    has_ctx = a_ctx is not None
    rows = T_ALL if has_ctx else T_LAT
    in_specs = [pl.BlockSpec((tm, D), lambda m: (jnp.minimum(m, MIXER_LAT_TILES - 1), 0))]
    args = [a_lat]
    if has_ctx:
        ctx_tile0 = (a_ctx.shape[0] - T_CTX) // tm
        in_specs.append(pl.BlockSpec((tm, D), lambda m: (jnp.maximum(m - MIXER_LAT_TILES, 0) + ctx_tile0, 0)))
        args.append(a_ctx)
    r_specs, r_args = _row_tile_specs(resid, tm, lambda m: m)
    in_specs += [_resident((None, D, D), lambda m: (w_layer, 0, 0))] + r_specs + [
        pl.BlockSpec((None, 1, D), lambda m: (brow(m), 0, 2))]
    return pl.pallas_call(
        functools.partial(_mixer_out_kernel, has_ctx=has_ctx, split_resid=len(r_args) == 2),
        grid=(rows // tm,),
        in_specs=in_specs,
        out_specs=pl.BlockSpec((tm, D), lambda m: (m, 0)),
        out_shape=jax.ShapeDtypeStruct((rows, D), F32),
        compiler_params=_params("parallel"),
        name="mixer_out",
    )(*args, w, *r_args, mods)


FFN_TM = 1024
FFN_TF = 512
FFN_HALO = BF16_ROWS
FFN_OUT_COLS = 512
FFN_ROW_CHUNK = CTX_LEN
FFN_LOOKAHEAD = 4
FFN_ROW_SETS = 4
FFN_SLABS = FFN_TF // LANES


def _ffn_kernel(x_ref, xp_ref, xn_ref, g_ref, sh_ref, sc_ref, gate_ref, wa_ref, wg_ref, cwa_ref, cwg_ref,
                cba_ref, cbg_ref, wo_ref, fg_ref, o_ref, h_ref, ua_ref, ug_ref, act_ref, *, final_norm):
    tm, halo = FFN_TM, FFN_HALO
    f = pl.program_id(1)

    @pl.when(f == 0)
    def _():
        g, sh, sc = g_ref[...], sh_ref[...], sc_ref[...]
        h_ref[:halo, :] = _modulate(xp_ref[...], g, sh, sc).astype(BF16)
        h_ref[halo + tm:, :] = _modulate(xn_ref[...], g, sh, sc).astype(BF16)
        _modulate_rows(x_ref, h_ref, halo, tm, g, sh, sc)
        o_ref[...] = jnp.zeros_like(o_ref)

    ch, n_ch, sets = FFN_ROW_CHUNK, tm // FFN_ROW_CHUNK, FFN_ROW_SETS
    set_rows = ch // sets
    sub = lax.broadcasted_iota(jnp.int32, (set_rows, LANES), 0)

    def hidden(r):
        lo = 0 if r == 0 else halo + r * ch
        hi = tm + 2 * halo if r == n_ch - 1 else halo + (r + 1) * ch
        h = h_ref[lo:hi, :]
        ua = jnp.dot(h, wa_ref[...], preferred_element_type=F32)
        ug = jnp.dot(h, wg_ref[...], preferred_element_type=F32)
        for s in range(FFN_SLABS):
            ua_ref[s, lo:hi, :] = ua[:, s * LANES:(s + 1) * LANES]
            ug_ref[s, lo:hi, :] = ug[:, s * LANES:(s + 1) * LANES]

    def gate_and_project(r):
        base = halo + r * ch
        rows = slice(r * ch, (r + 1) * ch)
        row0 = pl.program_id(0) * tm + r * ch
        seq_mask = jnp.where(row0 < T_LAT, SEQ - 1, CTX_LEN - 1)
        keep_first_up = jnp.where((row0 & seq_mask) == 0, 0.0, 1.0).astype(F32)
        keep_last_dn = jnp.where(((row0 + ch) & seq_mask) == 0, 0.0, 1.0).astype(F32)
        first_up = jnp.where(sub == 0, keep_first_up, 1.0)
        last_dn = jnp.where(sub == set_rows - 1, keep_last_dn, 1.0)

        for s in range(FFN_SLABS):
            lanes = slice(s * LANES, (s + 1) * LANES)
            for j in range(sets):
                def conv(u_ref, cw_ref, cb_ref):
                    up = u_ref[s, pl.ds(base + j - 1, set_rows, stride=sets), :]
                    cur = u_ref[s, pl.ds(base + j, set_rows, stride=sets), :]
                    dn = u_ref[s, pl.ds(base + j + 1, set_rows, stride=sets), :]
                    if j == 0:
                        up = up * first_up
                    if j == sets - 1:
                        dn = dn * last_dn
                    return (up * cw_ref[0:1, lanes] + cur * cw_ref[1:2, lanes] + dn * cw_ref[2:3, lanes]
                            + cb_ref[:, lanes])

                a = conv(ua_ref, cwa_ref, cba_ref)
                gg = conv(ug_ref, cwg_ref, cbg_ref)
                act_ref[s, pl.ds(r * ch + j, set_rows, stride=sets), :] = a * (gg * jax.nn.sigmoid(gg))

        act = jnp.concatenate([act_ref[s, rows, :] for s in range(FFN_SLABS)], axis=1).astype(BF16)
        for c in range(D // FFN_OUT_COLS):
            cols = slice(c * FFN_OUT_COLS, (c + 1) * FFN_OUT_COLS)
            o_ref[rows, cols] += jnp.dot(act, wo_ref[:, cols], preferred_element_type=F32)

    for r in range(min(FFN_LOOKAHEAD, n_ch)):
        hidden(r)
    for r in range(n_ch):
        if r + FFN_LOOKAHEAD < n_ch:
            hidden(r + FFN_LOOKAHEAD)
        gate_and_project(r)

    @pl.when(f == pl.num_programs(1) - 1)
    def _():
        gate = gate_ref[...]

        def body(c, carry):
            rows = pl.ds(pl.multiple_of(c * PROLOGUE_ROWS, PROLOGUE_ROWS), PROLOGUE_ROWS)
            y = x_ref[rows, :] + gate * o_ref[rows, :]
            if final_norm:
                y = (y * lax.rsqrt(jnp.mean(y * y, axis=-1, keepdims=True) + EPS)) * fg_ref[...]
            o_ref[rows, :] = y
            return carry
        lax.fori_loop(0, tm // PROLOGUE_ROWS, body, 0)


def _ffn(x, norm_g, g_row, mods, w_in, w_out, w_layer, conv_w, conv_b, layer, final_g, *, rows, final_norm):
    tm, tf, halo = FFN_TM, FFN_TF, FFN_HALO
    nf = D_FF // tf
    brow = _mod_row(tm)
    last_halo = rows // halo - 1
    cb = conv_b.reshape(DEPTH, 1, 2 * D_FF)
    return pl.pallas_call(
        functools.partial(_ffn_kernel, final_norm=final_norm),
        grid=(rows // tm, nf),
        in_specs=[
            pl.BlockSpec((tm, D), lambda m, f: (m, 0), pipeline_mode=pl.Buffered(1)),
            pl.BlockSpec((halo, D), lambda m, f: (jnp.maximum(m * (tm // halo) - 1, 0), 0)),
            pl.BlockSpec((halo, D), lambda m, f: (jnp.minimum((m + 1) * (tm // halo), last_halo), 0)),
            pl.BlockSpec((None, 1, D), lambda m, f: (g_row, 0, 0)),
            pl.BlockSpec((None, 1, D), lambda m, f: (brow(m), 0, 3)),
            pl.BlockSpec((None, 1, D), lambda m, f: (brow(m), 0, 4)),
            pl.BlockSpec((None, 1, D), lambda m, f: (brow(m), 0, 5)),
            pl.BlockSpec((None, D, tf), lambda m, f: (w_layer, 0, f)),
            pl.BlockSpec((None, D, tf), lambda m, f: (w_layer, 0, f + nf)),
            pl.BlockSpec((None, CONV_W, tf), lambda m, f: (layer, 0, f)),
            pl.BlockSpec((None, CONV_W, tf), lambda m, f: (layer, 0, f + nf)),
            pl.BlockSpec((None, 1, tf), lambda m, f: (layer, 0, f)),
            pl.BlockSpec((None, 1, tf), lambda m, f: (layer, 0, f + nf)),
            pl.BlockSpec((None, tf, D), lambda m, f: (w_layer, f, 0)),
            pl.BlockSpec((1, D), lambda m, f: (0, 0)),
        ],
        out_specs=pl.BlockSpec((tm, D), lambda m, f: (m, 0)),
        out_shape=jax.ShapeDtypeStruct((rows, D), F32),
        scratch_shapes=[
            pltpu.VMEM((tm + 2 * halo, D), BF16),
            pltpu.VMEM((FFN_SLABS, tm + 2 * halo, LANES), F32),
            pltpu.VMEM((FFN_SLABS, tm + 2 * halo, LANES), F32),
            pltpu.VMEM((FFN_SLABS, tm, LANES), F32),
        ],
        compiler_params=_params("parallel", "arbitrary"),
        name="ffn",
    )(x, x, x, norm_g, mods, mods, mods, w_in, w_in, conv_w, conv_w, cb, cb, w_out, final_g.reshape(1, D))


NA_QROWS = 4
NA_QBLOCK = NA_QROWS * GRID_W
NA_SUB = 8
NA_BAND_ROWS = 12
NA_BAND = NA_BAND_ROWS * GRID_W


CAST_LAYERS = 2
CAST_STEPS = BATCH * HEADS
CAST_IN_ROWS = D // CAST_STEPS
CAST_OUT_ROWS = 2 * D_FF // CAST_STEPS
assert CAST_IN_ROWS % BF16_ROWS == 0 and CAST_OUT_ROWS % BF16_ROWS == 0 and DEPTH % CAST_LAYERS == 0


def _cast_job(cast, step_of):
    if cast is None:
        return [], [], [], []
    f_w_in, f_w_out, pair = cast
    w_in_spec = pl.BlockSpec((CAST_LAYERS, CAST_IN_ROWS, 2 * D_FF), lambda *g: (pair, step_of(*g), 0))
    w_out_spec = pl.BlockSpec((CAST_LAYERS, CAST_OUT_ROWS, D), lambda *g: (pair, step_of(*g) // 2, 0))
    shapes = [jax.ShapeDtypeStruct((CAST_LAYERS,) + f_w_in.shape[1:], BF16),
              jax.ShapeDtypeStruct((CAST_LAYERS,) + f_w_out.shape[1:], BF16)]
    return [w_in_spec, w_out_spec], [_cast_out_spec(w_in_spec), _cast_out_spec(w_out_spec)], shapes, [f_w_in, f_w_out]


def _cast_out_spec(spec):
    return pl.BlockSpec(spec.block_shape, lambda *g: (0,) + tuple(spec.index_map(*g))[1:])


def _run_cast_job(refs, n_in, with_cast):
    if not with_cast:
        return refs
    w_in_src, w_out_src, o_ref, w_in_dst, w_out_dst = refs[n_in:n_in + 5]
    w_in_dst[...] = w_in_src[...].astype(BF16)
    w_out_dst[...] = w_out_src[...].astype(BF16)
    return refs[:n_in] + (o_ref,) + refs[n_in + 5:]


ATTN_LOOKAHEAD = 2


def _nt_dot(a, b):
    return lax.dot_general(a, b, (((1,), (1,)), ((), ())), preferred_element_type=F32)


NA_QBLOCKS = GRID_H // NA_QROWS
NA_SCALE = HEAD_DIM ** -0.5
NA_SCALE_LOG2E = np.float32(NA_SCALE * np.log2(np.e))
NA_KINDS = ((lambda j: 0, WIN_H - 1), (lambda j: j, WIN_H // 2 - 1), (lambda j: NA_BAND_ROWS - WIN_H, -1))


def _na_band_start(blk):
    return jnp.clip(blk * NA_QROWS - WIN_H // 2, 0, GRID_H - NA_BAND_ROWS)


def _na_lat_kernel(*refs, with_cast):
    q_ref, k_ref, v_ref, kc_ref, vc_ref, bias_ref, o_ref = _run_cast_job(refs, 6, with_cast)
    i = pl.program_id(2)
    kc = kc_ref[...]
    vc = vc_ref[...]

    def key_start(sb):
        return pl.multiple_of(_na_band_start(i * NA_SUB + sb) * GRID_W, NA_QROWS * GRID_W)

    def logits(sb):
        blk = i * NA_SUB + sb
        kind = jnp.where(blk == 0, 0, jnp.where(blk == NA_QBLOCKS - 1, 2, 1))
        q = q_ref[sb * NA_QBLOCK:(sb + 1) * NA_QBLOCK, :]
        return _nt_dot(q, k_ref[pl.ds(key_start(sb), NA_BAND), :]) + bias_ref[kind], _nt_dot(q, kc)

    ahead = [logits(sb) for sb in range(min(ATTN_LOOKAHEAD, NA_SUB))]
    for sb in range(NA_SUB):
        t, tc = ahead.pop(0)
        if sb + ATTN_LOOKAHEAD < NA_SUB:
            ahead.append(logits(sb + ATTN_LOOKAHEAD))
        mx = jnp.maximum(jnp.max(t, axis=-1, keepdims=True), jnp.max(tc, axis=-1, keepdims=True))
        e = jnp.exp2((t - mx) * NA_SCALE_LOG2E)
        ec = jnp.exp2((tc - mx) * NA_SCALE_LOG2E)
        denom = jnp.sum(e, axis=-1, keepdims=True) + jnp.sum(ec, axis=-1, keepdims=True)
        o = (jnp.dot(e.astype(BF16), v_ref[pl.ds(key_start(sb), NA_BAND), :], preferred_element_type=F32)
             + jnp.dot(ec.astype(BF16), vc, preferred_element_type=F32))
        o_ref[sb * NA_QBLOCK:(sb + 1) * NA_QBLOCK, :] = (o / denom).astype(o_ref.dtype)


def _na_bias_table(rpb):
    n = rpb.shape[0]
    w = np.arange(GRID_W)[:, None]
    k = np.arange(GRID_W)[None, :]
    col0 = np.clip(w - WIN_W // 2, 0, GRID_W - WIN_W)
    in_win = (k >= col0) & (k < col0 + WIN_W)
    col_pick = (np.arange(2 * WIN_W - 1)[:, None, None] == (k - w + WIN_W - 1)[None]) & in_win[None]
    t = jnp.einsum("nrx,xwk->nrwk", rpb, col_pick.astype(np.float32), precision=lax.Precision.HIGHEST)
    t = jnp.where(jnp.asarray(in_win), t, MASK_VALUE) * np.float32(1.0 / NA_SCALE)
    return pl.pallas_call(
        _na_bias_blocks_kernel,
        grid=(n,),
        in_specs=[pl.BlockSpec((None, 2 * WIN_H - 1, GRID_W, GRID_W), lambda i: (i, 0, 0, 0))],
        out_specs=pl.BlockSpec((None, len(NA_KINDS), NA_QBLOCK, NA_BAND), lambda i: (i, 0, 0, 0)),
        out_shape=jax.ShapeDtypeStruct((n, len(NA_KINDS), NA_QBLOCK, NA_BAND), F32),
        compiler_params=_params("parallel"),
        name="na_bias_blocks",
    )(t)


def _na_bias_blocks_kernel(t_ref, o_ref):
    masked = jnp.full((GRID_W, GRID_W), MASK_VALUE / NA_SCALE, F32)
    for kind, (first_valid, offset0) in enumerate(NA_KINDS):
        for j in range(NA_QROWS):
            lo = first_valid(j)
            for c in range(NA_BAND_ROWS):
                tile = t_ref[c - j + offset0] if lo <= c < lo + WIN_H else masked
                o_ref[kind, j * GRID_W:(j + 1) * GRID_W, c * GRID_W:(c + 1) * GRID_W] = tile


def _na_lat_attention(qkv, bias_tab, layer, cast=None):
    blocks = SEQ // (NA_SUB * NA_QBLOCK)
    assert blocks == 1
    qb = NA_SUB * NA_QBLOCK
    cast_in, cast_out, cast_shape, cast_args = _cast_job(cast, lambda h, b, i: h * BATCH + b)
    return pl.pallas_call(
        functools.partial(_na_lat_kernel, with_cast=cast is not None),
        grid=(HEADS, BATCH, blocks),
        in_specs=[
            pl.BlockSpec((qb, HEAD_DIM), lambda h, b, i: (b * blocks + i, h)),
            pl.BlockSpec((SEQ, HEAD_DIM), lambda h, b, i: (b, HEADS + h)),
            pl.BlockSpec((SEQ, HEAD_DIM), lambda h, b, i: (b, 2 * HEADS + h)),
            pl.BlockSpec((CTX_LEN, HEAD_DIM), lambda h, b, i: (CTX_ROW_BLOCK0 + b, HEADS + h)),
            pl.BlockSpec((CTX_LEN, HEAD_DIM), lambda h, b, i: (CTX_ROW_BLOCK0 + b, 2 * HEADS + h)),
            pl.BlockSpec((None, len(NA_KINDS), NA_QBLOCK, NA_BAND), lambda h, b, i: (layer * HEADS + h, 0, 0, 0)),
        ] + cast_in,
        out_specs=[pl.BlockSpec((qb, HEAD_DIM), lambda h, b, i: (b * blocks + i, h))] + cast_out,
        out_shape=[jax.ShapeDtypeStruct((T_LAT, D), BF16)] + cast_shape,
        compiler_params=_params("arbitrary", "arbitrary", "arbitrary"),
        name="na_lat",
    )(qkv, qkv, qkv, qkv, qkv, bias_tab, *cast_args)


def _softmax_pv(s, v):
    e = jnp.exp(s - jnp.max(s, axis=-1, keepdims=True))
    o = jnp.dot(e.astype(BF16), v, preferred_element_type=F32)
    return o / jnp.sum(e, axis=-1, keepdims=True)


def _na_ctx_kernel(q_ref, k_ref, v_ref, o_ref):
    scale = np.float32(HEAD_DIM ** -0.5)
    for h in range(HEADS):
        cols = slice(h * HEAD_DIM, (h + 1) * HEAD_DIM)
        s = _nt_dot(q_ref[:, cols], k_ref[:, cols]) * scale
        o_ref[:, cols] = _softmax_pv(s, v_ref[:, cols]).astype(o_ref.dtype)


def _na_ctx_attention(qkv):
    return pl.pallas_call(
        _na_ctx_kernel,
        grid=(BATCH,),
        in_specs=[
            pl.BlockSpec((CTX_LEN, D), lambda b: (CTX_ROW_BLOCK0 + b, 0)),
            pl.BlockSpec((CTX_LEN, D), lambda b: (CTX_ROW_BLOCK0 + b, 1)),
            pl.BlockSpec((CTX_LEN, D), lambda b: (CTX_ROW_BLOCK0 + b, 2)),
        ],
        out_specs=pl.BlockSpec((CTX_LEN, D), lambda b: (b, 0)),
        out_shape=jax.ShapeDtypeStruct((T_CTX, D), BF16),
        compiler_params=_params("parallel"),
        name="na_ctx",
    )(qkv, qkv, qkv)


MLA_QK = 2 * HEAD_DIM
MLA_QBLOCK = 256
MLA_SCORE_CHUNK = 768
MLA_SUB = 8
MLA_ZCOLS = MLA_Q_RANK + MLA_KV_RANK + 2 * HEAD_DIM
MLA_UP_TM = 512


def _rms(z, g):
    return (z * lax.rsqrt(jnp.mean(z * z, axis=-1, keepdims=True) + EPS)) * g


def _mla_up_kernel(z_ref, qn_ref, kvn_ref, cos_ref, sin_ref, wq_ref, wqs_ref, wkv_ref, q_ref, kv_ref, kpe_ref):
    cq = _rms(z_ref[:, :MLA_Q_RANK], qn_ref[...]).astype(BF16)
    ckv = _rms(z_ref[:, MLA_Q_RANK:MLA_Q_RANK + MLA_KV_RANK], kvn_ref[...]).astype(BF16)
    c0 = MLA_Q_RANK + MLA_KV_RANK
    cos = cos_ref[...]
    sin = sin_ref[...]
    kpe_ref[...] = (z_ref[:, c0:c0 + HEAD_DIM] * cos + z_ref[:, c0 + HEAD_DIM:] * sin).astype(BF16)
    for h in range(HEADS):
        q = jnp.dot(cq, wq_ref[h], preferred_element_type=F32)
        qs = jnp.dot(cq, wqs_ref[h], preferred_element_type=F32)
        q_ref[h, :, :HEAD_DIM] = q[:, :HEAD_DIM].astype(BF16)
        q_ref[h, :, HEAD_DIM:] = (q[:, HEAD_DIM:] * cos + qs * sin).astype(BF16)
        kv_ref[h] = jnp.dot(ckv, wkv_ref[h], preferred_element_type=F32).astype(BF16)


def _mla_up(z, q_norm, kv_norm, cos, sin, wq, wqs, wkv):
    tm = MLA_UP_TM
    return pl.pallas_call(
        _mla_up_kernel,
        grid=(T_ALL // tm,),
        in_specs=[
            pl.BlockSpec((tm, MLA_ZCOLS), lambda m: (m, 0)),
            pl.BlockSpec((1, MLA_Q_RANK), lambda m: (0, 0)),
            pl.BlockSpec((1, MLA_KV_RANK), lambda m: (0, 0)),
            pl.BlockSpec((tm, HEAD_DIM), lambda m: (m, 0)),
            pl.BlockSpec((tm, HEAD_DIM), lambda m: (m, 0)),
            _resident((HEADS, MLA_Q_RANK, MLA_QK), lambda m: (0, 0, 0)),
            _resident((HEADS, MLA_Q_RANK, HEAD_DIM), lambda m: (0, 0, 0)),
            _resident((HEADS, MLA_KV_RANK, 2 * HEAD_DIM), lambda m: (0, 0, 0)),
        ],
        out_specs=[
            pl.BlockSpec((HEADS, tm, MLA_QK), lambda m: (0, m, 0)),
            pl.BlockSpec((HEADS, tm, 2 * HEAD_DIM), lambda m: (0, m, 0)),
            pl.BlockSpec((tm, HEAD_DIM), lambda m: (m, 0)),
        ],
        out_shape=[
            jax.ShapeDtypeStruct((HEADS, T_ALL, MLA_QK), BF16),
            jax.ShapeDtypeStruct((HEADS, T_ALL, 2 * HEAD_DIM), BF16),
            jax.ShapeDtypeStruct((T_ALL, HEAD_DIM), BF16),
        ],
        compiler_params=_params("parallel"),
        name="mla_up",
    )(z, q_norm.reshape(1, -1), kv_norm.reshape(1, -1), cos, sin, wq, wqs, wkv)


MLA_SCALE = np.float32((MLA_NOPE + MLA_ROPE) ** -0.5)
MLA_SCALE_LOG2E = np.float32((MLA_NOPE + MLA_ROPE) ** -0.5 * np.log2(np.e))


def _mla_lat_kernel(*refs, with_cast):
    (q_ref, kn_ref, v_ref, kpe_ref, knc_ref, vc_ref, kpec_ref, o_ref, kcat_ref, s_ref,
     p_ref) = _run_cast_job(refs, 7, with_cast)

    @pl.when(pl.program_id(2) == 0)
    def _():
        kcat_ref[:SEQ, :HEAD_DIM] = kn_ref[...]
        kcat_ref[:SEQ, HEAD_DIM:] = kpe_ref[...]
        kcat_ref[SEQ:, :HEAD_DIM] = knc_ref[...]
        kcat_ref[SEQ:, HEAD_DIM:] = kpec_ref[...]

    chunks = [slice(c, c + MLA_SCORE_CHUNK) for c in range(0, SEQ + CTX_LEN, MLA_SCORE_CHUNK)]

    def scores(sb):
        q = q_ref[sb * MLA_QBLOCK:(sb + 1) * MLA_QBLOCK, :]
        for c in chunks:
            s_ref[sb % 2, :, c] = _nt_dot(q, kcat_ref[c, :])

    scores(0)
    for sb in range(MLA_SUB):
        rows = slice(sb * MLA_QBLOCK, (sb + 1) * MLA_QBLOCK)
        slot = sb % 2
        if sb + 1 < MLA_SUB:
            scores(sb + 1)
        mx = functools.reduce(jnp.maximum, [jnp.max(s_ref[slot, :, c], axis=-1, keepdims=True) for c in chunks])
        denom = 0.0
        for c in chunks:
            e = jnp.exp2((s_ref[slot, :, c] - mx) * MLA_SCALE_LOG2E)
            denom = denom + jnp.sum(e, axis=-1, keepdims=True)
            p_ref[slot, :, c] = e.astype(BF16)
        o = (jnp.dot(p_ref[slot, :, :SEQ], v_ref[...], preferred_element_type=F32)
             + jnp.dot(p_ref[slot, :, SEQ:], vc_ref[...], preferred_element_type=F32))
        o_ref[rows, :] = (o / denom).astype(o_ref.dtype)


def _mla_lat_attention(q, kv, kpe, cast=None):
    qb = MLA_SUB * MLA_QBLOCK
    blocks = SEQ // qb
    assert blocks == 1
    ctx_q = CTX_ROW_BLOCK0
    cast_in, cast_out, cast_shape, cast_args = _cast_job(cast, lambda b, h, i: b * HEADS + h)
    return pl.pallas_call(
        functools.partial(_mla_lat_kernel, with_cast=cast is not None),
        grid=(BATCH, HEADS, blocks),
        in_specs=[
            pl.BlockSpec((None, qb, MLA_QK), lambda b, h, i: (h, b * blocks + i, 0)),
            pl.BlockSpec((None, SEQ, HEAD_DIM), lambda b, h, i: (h, b, 0)),
            pl.BlockSpec((None, SEQ, HEAD_DIM), lambda b, h, i: (h, b, 1)),
            pl.BlockSpec((SEQ, HEAD_DIM), lambda b, h, i: (b, 0)),
            pl.BlockSpec((None, CTX_LEN, HEAD_DIM), lambda b, h, i: (h, ctx_q + b, 0)),
            pl.BlockSpec((None, CTX_LEN, HEAD_DIM), lambda b, h, i: (h, ctx_q + b, 1)),
            pl.BlockSpec((CTX_LEN, HEAD_DIM), lambda b, h, i: (ctx_q + b, 0)),
        ] + cast_in,
        out_specs=[pl.BlockSpec((qb, HEAD_DIM), lambda b, h, i: (b * blocks + i, h))] + cast_out,
        out_shape=[jax.ShapeDtypeStruct((T_LAT, D), BF16)] + cast_shape,
        scratch_shapes=[pltpu.VMEM((SEQ + CTX_LEN, MLA_QK), BF16),
                        pltpu.VMEM((2, MLA_QBLOCK, SEQ + CTX_LEN), F32),
                        pltpu.VMEM((2, MLA_QBLOCK, SEQ + CTX_LEN), BF16)],
        compiler_params=_params("arbitrary", "arbitrary", "arbitrary"),
        name="mla_lat",
    )(q, kv, kv, kpe, kv, kv, kpe, *cast_args)


def _mla_ctx_kernel(q_ref, kv_ref, kpe_ref, o_ref):
    kpe = kpe_ref[...]
    for h in range(HEADS):
        kcat = jnp.concatenate([kv_ref[h, :, :HEAD_DIM], kpe], axis=1)
        s = _nt_dot(q_ref[h], kcat) * MLA_SCALE
        o_ref[:, h * HEAD_DIM:(h + 1) * HEAD_DIM] = _softmax_pv(s, kv_ref[h, :, HEAD_DIM:]).astype(o_ref.dtype)


def _mla_ctx_attention(q, kv, kpe):
    ctx_q = CTX_ROW_BLOCK0
    return pl.pallas_call(
        _mla_ctx_kernel,
        grid=(BATCH,),
        in_specs=[
            pl.BlockSpec((HEADS, CTX_LEN, MLA_QK), lambda b: (0, ctx_q + b, 0)),
            pl.BlockSpec((HEADS, CTX_LEN, 2 * HEAD_DIM), lambda b: (0, ctx_q + b, 0)),
            pl.BlockSpec((CTX_LEN, HEAD_DIM), lambda b: (ctx_q + b, 0)),
        ],
        out_specs=pl.BlockSpec((CTX_LEN, D), lambda b: (b, 0)),
        out_shape=jax.ShapeDtypeStruct((T_CTX, D), BF16),
        compiler_params=_params("parallel"),
        name="mla_ctx",
    )(q, kv, kpe)


def _rope_tables():
    half = MLA_ROPE // 2
    freqs = ROPE_THETA ** (-jnp.arange(0, half, 2, dtype=F32) / half)
    t = jnp.arange(SEQ)
    rows = (t // GRID_W).astype(F32)[:, None] * freqs
    cols = (t % GRID_W).astype(F32)[:, None] * freqs
    cr, sr, cc, sn = jnp.cos(rows), jnp.sin(rows), jnp.cos(cols), jnp.sin(cols)
    pad = jnp.zeros((SEQ, HEAD_DIM - MLA_ROPE), F32)
    cos = jnp.concatenate([cr, cr, cc, cc, pad], axis=1)
    sin = jnp.concatenate([-sr, sr, -sn, sn, pad], axis=1)
    cos_c = jnp.concatenate([jnp.ones((T_CTX, MLA_ROPE), F32), jnp.zeros((T_CTX, HEAD_DIM - MLA_ROPE), F32)], axis=1)
    return (jnp.concatenate([jnp.tile(cos, (BATCH, 1)), cos_c], axis=0),
            jnp.concatenate([jnp.tile(sin, (BATCH, 1)), jnp.zeros((T_CTX, HEAD_DIM), F32)], axis=0))


def _swap_rope_halves(w):
    q = MLA_ROPE // 4
    return jnp.concatenate([w[..., q:2 * q], w[..., :q], w[..., 3 * q:], w[..., 2 * q:3 * q]], axis=-1)


SGU_TM = 512
SGU_COLS = 512


def _sgu_kernel(x_ref, g_ref, sh_ref, sc_ref, w_ref, b_ref, lng_ref, lnb_ref, ws_ref, bs_ref, o_ref,
                h_ref, u_ref, v_ref):
    _modulate_rows(x_ref, h_ref, 0, SGU_TM, g_ref[...], sh_ref[...], sc_ref[...])
    h = h_ref[...]
    for nc in range(2 * D // SGU_COLS):
        cols = slice(nc * SGU_COLS, (nc + 1) * SGU_COLS)
        acc = _gelu_exact(jnp.dot(h, w_ref[:, cols], preferred_element_type=F32) + b_ref[:, cols])
        if nc < D // SGU_COLS:
            u_ref[:, cols] = acc
        else:
            v_ref[:, nc * SGU_COLS - D:(nc + 1) * SGU_COLS - D] = acc
    for r in range(SGU_TM // CHUNK):
        rows = slice(r * CHUNK, (r + 1) * CHUNK)
        v = v_ref[rows, :]
        mu = jnp.mean(v, axis=-1, keepdims=True)
        vc = v - mu
        var = jnp.mean(vc * vc, axis=-1, keepdims=True)
        vn = ((vc * lax.rsqrt(var + EPS)) * lng_ref[...] + lnb_ref[...]).astype(BF16)
        for g in range(SG_GROUPS):
            cols = slice(g * CHUNK, (g + 1) * CHUNK)
            mix = jnp.dot(ws_ref[g], vn[:, cols], preferred_element_type=F32) + bs_ref[:, cols]
            o_ref[rows, cols] = (u_ref[rows, cols] * mix).astype(o_ref.dtype)


def _sgu(x, norm_g, g_row, mods, w_in, b_in, ln_g, ln_b, ws, bs_full, layer):
    tm = SGU_TM
    brow = _mod_row(tm)
    return pl.pallas_call(
        _sgu_kernel,
        grid=(T_ALL // tm,),
        in_specs=[
            pl.BlockSpec((tm, D), lambda m: (m, 0)),
            pl.BlockSpec((None, 1, D), lambda m: (g_row, 0, 0)),
            pl.BlockSpec((None, 1, D), lambda m: (brow(m), 0, 0)),
            pl.BlockSpec((None, 1, D), lambda m: (brow(m), 0, 1)),
            _resident((None, D, 2 * D), lambda m: (layer, 0, 0)),
            pl.BlockSpec((None, 1, 2 * D), lambda m: (layer, 0, 0)),
            pl.BlockSpec((1, D), lambda m: (0, 0)),
            pl.BlockSpec((1, D), lambda m: (0, 0)),
            pl.BlockSpec((SG_GROUPS, CHUNK, CHUNK), lambda m: (0, 0, 0)),
            pl.BlockSpec((CHUNK, D), lambda m: (0, 0)),
        ],
        out_specs=pl.BlockSpec((tm, D), lambda m: (m, 0)),
        out_shape=jax.ShapeDtypeStruct((T_ALL, D), BF16),
        scratch_shapes=[pltpu.VMEM((tm, D), BF16), pltpu.VMEM((tm, D), F32), pltpu.VMEM((tm, D), F32)],
        compiler_params=_params("parallel"),
        name="sgu",
    )(x, norm_g, mods, mods, w_in, b_in.reshape(b_in.shape[0], 1, 2 * D), ln_g.reshape(1, D), ln_b.reshape(1, D), ws,
      bs_full)


def kernel(x, c, ctx, c_ctx, ada_w, ada_b, norm_g, final_g, a_w_qkv, a_w_o, a_rpb, b_w_in, b_q_norm, b_kv_norm,
           b_w_uq, b_w_ukv, b_w_o, c_w_in, c_b_in, c_ln_g, c_ln_b, c_ws, c_bs, c_w_o, f_w_in, f_conv_w, f_conv_b,
           f_w_out):
    xs = (x.reshape(T_LAT, D), ctx.reshape(T_CTX, D))
    cond = jnp.concatenate([c, c_ctx[None, :], jnp.zeros((8 - BATCH - 1, D), F32)], axis=0)
    mods_all = _ada(cond, ada_w, ada_b).reshape(DEPTH, 8, 1, 6 * D)
    norm_rows = norm_g.reshape(DEPTH * 2, 1, D)
    na_bias = _na_bias_table(a_rpb.reshape(-1, 2 * WIN_H - 1, 2 * WIN_W - 1))

    a_w_qkv_b, a_w_o_b = a_w_qkv.astype(BF16), a_w_o.astype(BF16)
    b_w_o_b, c_w_in_b, c_w_o_b = b_w_o.astype(BF16), c_w_in.astype(BF16), c_w_o.astype(BF16)
    ffn_weights = {}

    for i in range(DEPTH):
        kind, j = i % N_MIXERS, i // N_MIXERS
        last = i == DEPTH - 1
        mods = mods_all[i]
        rows_out = T_LAT if last else T_ALL
        cast = (f_w_in, f_w_out, i) if i < DEPTH // CAST_LAYERS else None

        if kind == 0:
            qkv = _mod_matmul(xs, norm_rows, 2 * i, mods, a_w_qkv_b, j, tm=PROJ_TM, tn=PROJ_TN, out_dtype=BF16,
                              name="na_qkv")
            att, *cast_out = _na_lat_attention(qkv, na_bias, j, cast)
            att_ctx = None if last else _na_ctx_attention(qkv)
            w_o = a_w_o_b
        elif kind == 1:
            w_in = b_w_in[j]
            c0 = MLA_Q_RANK + MLA_KV_RANK
            zpad = jnp.zeros((D, HEAD_DIM - MLA_ROPE), F32)
            w_in_ext = jnp.concatenate([w_in, zpad, _swap_rope_halves(w_in[:, c0:]), zpad], axis=1)
            z = _mod_matmul(xs, norm_rows, 2 * i, mods, w_in_ext.astype(BF16)[None], 0, tm=PROJ_TM,
                            tn=MLA_ZCOLS // 2, out_dtype=F32, name="mla_in")
            wq = b_w_uq[j].reshape(MLA_Q_RANK, HEADS, MLA_NOPE + MLA_ROPE).transpose(1, 0, 2)
            hpad = jnp.zeros((HEADS, MLA_Q_RANK, HEAD_DIM - MLA_ROPE), F32)
            wq_cat = jnp.concatenate([wq, hpad], axis=-1)
            wq_swap = jnp.concatenate([_swap_rope_halves(wq[..., MLA_NOPE:]), hpad], axis=-1)
            wkv = b_w_ukv[j].reshape(MLA_KV_RANK, HEADS, 2 * HEAD_DIM).transpose(1, 0, 2)
            cos, sin = _rope_tables()
            q, kv, kpe = _mla_up(z, b_q_norm[j], b_kv_norm[j], cos, sin, wq_cat.astype(BF16), wq_swap.astype(BF16),
                                 wkv.astype(BF16))
            att, *cast_out = _mla_lat_attention(q, kv, kpe, cast)
            att_ctx = _mla_ctx_attention(q, kv, kpe)
            w_o = b_w_o_b
        else:
            assert cast is None
            bs_full = jnp.repeat(c_bs[j].T, CHUNK, axis=1)
            att = _sgu(xs, norm_rows, 2 * i, mods, c_w_in_b, c_b_in, c_ln_g[j], c_ln_b[j], c_ws[j].astype(BF16),
                       bs_full, j)
            att_ctx = None if last else att
            w_o = c_w_o_b

        if cast is not None:
            ffn_weights[i] = cast_out
        xs = _mixer_out(att, att_ctx, w_o, j, xs, mods)
        f_w_in_b, f_w_out_b = ffn_weights[i // CAST_LAYERS]
        xs = _ffn(xs, norm_rows, 2 * i + 1, mods, f_w_in_b, f_w_out_b, i % CAST_LAYERS, f_conv_w, f_conv_b, i, final_g,
                  rows=rows_out, final_norm=last)

    return xs.reshape(BATCH, SEQ, D)
```
